```python
import jax, jax.numpy as jnp
from jax import lax
import numpy as np

D_MODEL = 1024
BATCH = 2
SEQ = 8192
DEPTH = 4
DEC_BATCH = 128
DEC_SEQ = 4
PAST_LEN = 8192
PAGE_SIZE = 128

N_A = DEPTH // 2
N_B = DEPTH - N_A
N_DENSE = (DEPTH + 1) // 2
N_MOE = DEPTH // 2
RWKV_HEAD = 64
RWKV_HEADS = D_MODEL // RWKV_HEAD
DECAY_LORA = 64
AAA_LORA = 64
MV_LORA = 32
GATE_LORA = 128
GN_EPS = 64e-5
HEAD_DIM = 64
N_HEADS = D_MODEL // HEAD_DIM
N_KV = 4
GROUP = N_HEADS // N_KV
WINDOW = 128
BLOCK = 128
ROPE_DIM = HEAD_DIM // 4
ROPE_THETA = 500000.0
D_FF = 2816
N_EXPERTS = 8
TOP_K = 2
EXPERT_FF = D_MODEL
RMS_EPS = 1e-6
NEG = -1e30

kernel_name = 'hybrid_rwkv7_swa_sink_yoco_step'


def _rms(x, g):
    xf = x.astype(jnp.float32)
    y = xf * lax.rsqrt(jnp.mean(xf * xf, axis=-1, keepdims=True) + RMS_EPS)
    return (y * g.astype(jnp.float32)).astype(x.dtype)


def _modulate(h, shift, scale):
    return h * (1 + scale[:, None, :]) + shift[:, None, :]


def _wkv7_scan(S0, r, w, k, v, a, b):
    def step(S, inp):
        r_t, w_t, k_t, v_t, a_t, b_t = inp
        sa = jnp.einsum('bhvk,bhk->bhv', S, a_t)
        S = S * w_t[:, :, None, :] + sa[..., None] * b_t[:, :, None, :] + v_t[..., None] * k_t[:, :, None, :]
        return S, jnp.einsum('bhvk,bhk->bhv', S, r_t)
    xs = tuple(jnp.swapaxes(t, 0, 1) for t in (r, w, k, v, a, b))
    S, ys = lax.scan(step, S0.astype(jnp.float32), xs)
    return jnp.swapaxes(ys, 0, 1), S


def _rwkv7_time_mix(h, h_last, S0, v_first, vecs, w_rkv, w_o, w1, w2, a1, a2, g1, g2, r_k, vres):
    f32 = jnp.float32
    B, T, D = h.shape
    xx = jnp.concatenate([h_last[:, None, :].astype(h.dtype), h[:, :-1]], axis=1) - h
    xr, xw, xk, xv, xa, xg = [h + xx * vecs[i] for i in range(6)]
    r = xr @ w_rkv[0]
    k = xk @ w_rkv[1]
    v = xv @ w_rkv[2]
    w_log = -jax.nn.softplus(-(vecs[6] + jnp.tanh(xw @ w1) @ w2).astype(f32)) - 0.5
    decay = jnp.exp(-jnp.exp(w_log))
    a = jax.nn.sigmoid((vecs[7] + (xa @ a1) @ a2).astype(f32))
    g = jax.nn.sigmoid(xg @ g1) @ g2
    if vres is None:
        v_first = v
    else:
        v0, v1, v2 = vres
        v = v + (v_first - v) * jax.nn.sigmoid(v0 + (xv @ v1) @ v2)
    hs = lambda t: t.reshape(B, T, RWKV_HEADS, RWKV_HEAD)
    kk = hs(k.astype(f32) * vecs[8].astype(f32))
    kk = kk / jnp.maximum(jnp.sqrt(jnp.sum(kk * kk, axis=-1, keepdims=True)), 1e-12)
    k_h = hs(k.astype(f32) * (1 + (a - 1) * vecs[9].astype(f32)))
    a_h = hs(a)
    r_h = hs(r.astype(f32))
    v_h = hs(v.astype(f32))
    y, S = _wkv7_scan(S0, r_h, hs(decay), k_h, v_h, -kk, kk * a_h)
    mu = jnp.mean(y, axis=-1, keepdims=True)
    var = jnp.mean(jnp.square(y - mu), axis=-1, keepdims=True)
    y = ((y - mu) * lax.rsqrt(var + GN_EPS)).reshape(B, T, D) * vecs[10].astype(f32) + vecs[11].astype(f32)
    bonus = jnp.sum(r_h * k_h * r_k.astype(f32), axis=-1, keepdims=True) * v_h
    y = y + bonus.reshape(B, T, D)
    out = (y.astype(h.dtype) * g) @ w_o
    return out, v_first, S.astype(S0.dtype), h[:, -1]


def _head_rms(x, g):
    xf = x.astype(jnp.float32)
    y = xf * lax.rsqrt(jnp.mean(xf * xf, axis=-1, keepdims=True) + RMS_EPS)
    return (y * g.astype(jnp.float32)).astype(x.dtype)


def _rope(x, pos):
    half = ROPE_DIM // 2
    inv = ROPE_THETA ** (-jnp.arange(half, dtype=jnp.float32) * 2.0 / ROPE_DIM)
    ang = pos.astype(jnp.float32)[..., None] * inv
    cos = jnp.cos(ang)[..., None, :]
    sin = jnp.sin(ang)[..., None, :]
    xr = x[..., :ROPE_DIM].astype(jnp.float32)
    x1, x2 = xr[..., :half], xr[..., half:]
    rot = jnp.concatenate([x1 * cos - x2 * sin, x2 * cos + x1 * sin], axis=-1).astype(x.dtype)
    return jnp.concatenate([rot, x[..., ROPE_DIM:]], axis=-1)


def _sink_attention(q, k, v, q_pos, k_pos, sinks):
    qg = q.reshape(q.shape[:-2] + (N_KV, GROUP, HEAD_DIM))
    s = jnp.einsum('...qkgd,...skd->...kgqs', qg, k, preferred_element_type=jnp.float32) * (HEAD_DIM ** -0.5)
    dq = q_pos[..., :, None]
    dk = k_pos[..., None, :]
    mask = (dk <= dq) & (dq - dk < WINDOW) & (dk >= 0)
    s = jnp.where(mask[..., None, None, :, :], s, NEG)
    sink = sinks.astype(jnp.float32).reshape(N_KV, GROUP, 1, 1)
    m = jnp.maximum(jnp.max(s, axis=-1, keepdims=True), sink)
    p = jnp.exp(s - m)
    p = p / (jnp.sum(p, axis=-1, keepdims=True) + jnp.exp(sink - m))
    o = jnp.einsum('...kgqs,...skd->...qkgd', p.astype(v.dtype), v, preferred_element_type=jnp.float32)
    return o.reshape(q.shape).astype(q.dtype)


def _swa_banded(q, k, v, sinks):
    B, T = q.shape[:2]
    nb = T // BLOCK
    qb = q.reshape(B, nb, BLOCK, N_HEADS, HEAD_DIM)
    pad = jnp.zeros((B, BLOCK, N_KV, HEAD_DIM), k.dtype)
    def band(t):
        tp = jnp.concatenate([pad, t], axis=1)
        prev = tp[:, :T].reshape(B, nb, BLOCK, N_KV, HEAD_DIM)
        cur = t.reshape(B, nb, BLOCK, N_KV, HEAD_DIM)
        return jnp.concatenate([prev, cur], axis=2)
    q_pos = jnp.arange(T).reshape(nb, BLOCK)
    k_pos = (jnp.arange(nb)[:, None] - 1) * BLOCK + jnp.arange(2 * BLOCK)[None, :]
    o = _sink_attention(qb, band(k), band(v), q_pos, k_pos, sinks)
    return o.reshape(B, T, N_HEADS, HEAD_DIM)


def _shared_kv(x, c, pos, kv_ada_w, kv_ada_b, kv_norm_g, w_kv, k_norm_g):
    B, T, _ = x.shape
    shift, scale = jnp.split(jax.nn.silu(c) @ kv_ada_w + kv_ada_b, 2, axis=-1)
    h = _modulate(_rms(x, kv_norm_g), shift, scale)
    k, v = jnp.split(h @ w_kv, 2, axis=-1)
    k = _rope(_head_rms(k.reshape(B, T, N_KV, HEAD_DIM), k_norm_g), pos)
    return k, v.reshape(B, T, N_KV, HEAD_DIM)


def _swiglu(h, w_gu, w_down):
    gate, up = jnp.split(h @ w_gu, 2, axis=-1)
    return (jax.nn.silu(gate) * up) @ w_down


def _moe(h, router_w, router_b, w_gu, w_down):
    logits = (h @ router_w).astype(jnp.float32) + router_b.astype(jnp.float32)
    top_v, top_i = lax.top_k(logits, TOP_K)
    gates = jax.nn.softmax(top_v, axis=-1)
    combine = jnp.sum(jax.nn.one_hot(top_i, N_EXPERTS, dtype=jnp.float32) * gates[..., None], axis=-2)
    out = jnp.zeros(h.shape, jnp.float32)
    for e in range(N_EXPERTS):
        out = out + combine[..., e:e + 1] * _swiglu(h, w_gu[e], w_down[e]).astype(jnp.float32)
    return out.astype(h.dtype)


def _trunk(x, c, pos, rwkv_states, kv_cache, P):
    B, T, _ = x.shape
    new_rwkv = []
    v_first = None
    kv_out = None
    for l in range(DEPTH):
        sh1, sc1, g1, sh2, sc2, g2 = jnp.split(jax.nn.silu(c) @ P['ada_w'][l] + P['ada_b'][l], 6, axis=-1)
        h = _modulate(_rms(x, P['norm_g'][l, 0]), sh1, sc1)
        if l < N_A:
            vres = None if l == 0 else (P['rwkv_v0'][l - 1], P['rwkv_v1'][l - 1], P['rwkv_v2'][l - 1])
            S0, hl0 = rwkv_states[l]
            out, v_first, S, hl = _rwkv7_time_mix(
                h, hl0, S0, v_first, P['rwkv_vecs'][l], P['rwkv_w_rkv'][l], P['rwkv_w_o'][l],
                P['rwkv_w1'][l], P['rwkv_w2'][l], P['rwkv_a1'][l], P['rwkv_a2'][l],
                P['rwkv_g1'][l], P['rwkv_g2'][l], P['rwkv_r_k'][l], vres)
            new_rwkv.append((S, hl))
        else:
            j = l - N_A
            if j == 0:
                k_new, v_new = _shared_kv(x, c, pos, P['kv_ada_w'], P['kv_ada_b'], P['kv_norm_g'],
                                          P['attn_w_kv'], P['k_norm_g'])
                if kv_cache is None:
                    buf = min(WINDOW, T)
                    kv_out = (k_new[:, T - buf:], v_new[:, T - buf:])
                else:
                    ck, cv = kv_cache
                    buf = ck.shape[1]
                    k_all = jnp.concatenate([ck.astype(k_new.dtype), k_new], axis=1)
                    v_all = jnp.concatenate([cv.astype(v_new.dtype), v_new], axis=1)
                    k_pos = jnp.concatenate([PAST_LEN - buf + jnp.arange(buf), pos])
                    kv_out = (k_all[:, -buf:], v_all[:, -buf:])
            q = (h @ P['attn_w_q'][j]).reshape(B, T, N_HEADS, HEAD_DIM)
            q = _rope(_head_rms(q, P['q_norm_g'][j]), pos)
            if kv_cache is None:
                o = _swa_banded(q, k_new, v_new, P['attn_sinks'][j])
            else:
                o = _sink_attention(q, k_all, v_all, pos, k_pos, P['attn_sinks'][j])
            out = o.reshape(B, T, N_HEADS * HEAD_DIM) @ P['attn_w_o'][j]
        x = x + (g1[:, None, :] * out).astype(x.dtype)
        h = _modulate(_rms(x, P['norm_g'][l, 1]), sh2, sc2)
        if l % 2 == 0:
            f = _swiglu(h, P['ffn_w_gu'][l // 2], P['ffn_w_down'][l // 2])
        else:
            f = _moe(h, P['moe_router_w'][l // 2], P['moe_router_b'][l // 2],
                     P['moe_w_gu'][l // 2], P['moe_w_down'][l // 2])
        x = x + (g2[:, None, :] * f).astype(x.dtype)
    return x, new_rwkv, kv_out


def setup_inputs(seed: int = 0) -> dict:
    key = jax.random.key(seed)
    ks = iter(jax.random.split(key, 64))
    nrm = lambda shape, s: jax.random.normal(next(ks), shape, jnp.float32) * s
    uni = lambda shape, lo, hi: jax.random.uniform(next(ks), shape, jnp.float32, lo, hi)
    D = D_MODEL
    W_BUF = min(WINDOW, PAST_LEN)
    rwkv_vecs = jnp.concatenate([
        uni((N_A, 6, D), 0.0, 1.0),
        uni((N_A, 1, D), -6.5, -1.5),
        nrm((N_A, 1, D), 0.1),
        0.85 + nrm((N_A, 1, D), 0.05),
        1.0 + nrm((N_A, 1, D), 0.05),
        1.0 + nrm((N_A, 1, D), 0.05),
        nrm((N_A, 1, D), 0.02),
    ], axis=1)
    return dict(
        x_prompt=nrm((BATCH, SEQ, D), 1.0),
        x_sample=nrm((DEC_BATCH, DEC_SEQ, D), 1.0),
        c_prompt=nrm((BATCH, D), 1.0),
        c_sample=nrm((DEC_BATCH, D), 1.0),
        state_wkv_0=nrm((DEC_BATCH, RWKV_HEADS, RWKV_HEAD, RWKV_HEAD), 0.5),
        state_shift_0=nrm((DEC_BATCH, D), 1.0),
        state_wkv_1=nrm((DEC_BATCH, RWKV_HEADS, RWKV_HEAD, RWKV_HEAD), 0.5),
        state_shift_1=nrm((DEC_BATCH, D), 1.0),
        cache_k=nrm((DEC_BATCH, W_BUF, N_KV, HEAD_DIM), 1.0),
        cache_v=nrm((DEC_BATCH, W_BUF, N_KV, HEAD_DIM), 1.0),
        ada_w=nrm((DEPTH, D, 6 * D), 0.5 * D ** -0.5),
        ada_b=nrm((DEPTH, 6 * D), 0.02),
        norm_g=1.0 + nrm((DEPTH, 2, D), 0.05),
        rwkv_vecs=rwkv_vecs,
        rwkv_w_rkv=nrm((N_A, 3, D, D), D ** -0.5),
        rwkv_w_o=nrm((N_A, D, D), D ** -0.5),
        rwkv_w1=nrm((N_A, D, DECAY_LORA), D ** -0.5),
        rwkv_w2=nrm((N_A, DECAY_LORA, D), 0.1 * DECAY_LORA ** -0.5),
        rwkv_a1=nrm((N_A, D, AAA_LORA), D ** -0.5),
        rwkv_a2=nrm((N_A, AAA_LORA, D), 0.5 * AAA_LORA ** -0.5),
        rwkv_g1=nrm((N_A, D, GATE_LORA), D ** -0.5),
        rwkv_g2=nrm((N_A, GATE_LORA, D), GATE_LORA ** -0.5),
        rwkv_r_k=nrm((N_A, RWKV_HEADS, RWKV_HEAD), 0.1),
        rwkv_v0=1.0 + nrm((N_A - 1, D), 0.05),
        rwkv_v1=nrm((N_A - 1, D, MV_LORA), D ** -0.5),
        rwkv_v2=nrm((N_A - 1, MV_LORA, D), 0.5 * MV_LORA ** -0.5),
        kv_ada_w=nrm((D, 2 * D), 0.5 * D ** -0.5),
        kv_ada_b=nrm((2 * D,), 0.02),
        kv_norm_g=1.0 + nrm((D,), 0.05),
        attn_w_kv=nrm((D, 2 * N_KV * HEAD_DIM), D ** -0.5),
        k_norm_g=1.0 + nrm((HEAD_DIM,), 0.05),
        attn_w_q=nrm((N_B, D, N_HEADS * HEAD_DIM), D ** -0.5),
        q_norm_g=1.0 + nrm((N_B, HEAD_DIM), 0.05),
        attn_w_o=nrm((N_B, N_HEADS * HEAD_DIM, D), (N_HEADS * HEAD_DIM) ** -0.5),
        attn_sinks=nrm((N_B, N_HEADS), 0.5),
        ffn_w_gu=nrm((N_DENSE, D, 2 * D_FF), D ** -0.5),
        ffn_w_down=nrm((N_DENSE, D_FF, D), D_FF ** -0.5),
        moe_router_w=nrm((N_MOE, D, N_EXPERTS), D ** -0.5),
        moe_router_b=nrm((N_MOE, N_EXPERTS), 0.01),
        moe_w_gu=nrm((N_MOE, N_EXPERTS, D, 2 * EXPERT_FF), D ** -0.5),
        moe_w_down=nrm((N_MOE, N_EXPERTS, EXPERT_FF, D), EXPERT_FF ** -0.5),
    )


def reference(x_prompt, x_sample, c_prompt, c_sample, state_wkv_0, state_shift_0, state_wkv_1, state_shift_1,
              cache_k, cache_v, ada_w, ada_b, norm_g, rwkv_vecs, rwkv_w_rkv, rwkv_w_o, rwkv_w1, rwkv_w2,
              rwkv_a1, rwkv_a2, rwkv_g1, rwkv_g2, rwkv_r_k, rwkv_v0, rwkv_v1, rwkv_v2, kv_ada_w, kv_ada_b,
              kv_norm_g, attn_w_kv, k_norm_g, attn_w_q, q_norm_g, attn_w_o, attn_sinks, ffn_w_gu, ffn_w_down,
              moe_router_w, moe_router_b, moe_w_gu, moe_w_down):
    P = dict(ada_w=ada_w, ada_b=ada_b, norm_g=norm_g, rwkv_vecs=rwkv_vecs, rwkv_w_rkv=rwkv_w_rkv,
             rwkv_w_o=rwkv_w_o, rwkv_w1=rwkv_w1, rwkv_w2=rwkv_w2, rwkv_a1=rwkv_a1, rwkv_a2=rwkv_a2,
             rwkv_g1=rwkv_g1, rwkv_g2=rwkv_g2, rwkv_r_k=rwkv_r_k, rwkv_v0=rwkv_v0, rwkv_v1=rwkv_v1,
             rwkv_v2=rwkv_v2, kv_ada_w=kv_ada_w, kv_ada_b=kv_ada_b, kv_norm_g=kv_norm_g,
             attn_w_kv=attn_w_kv, k_norm_g=k_norm_g, attn_w_q=attn_w_q, q_norm_g=q_norm_g,
             attn_w_o=attn_w_o, attn_sinks=attn_sinks, ffn_w_gu=ffn_w_gu, ffn_w_down=ffn_w_down,
             moe_router_w=moe_router_w, moe_router_b=moe_router_b, moe_w_gu=moe_w_gu,
             moe_w_down=moe_w_down)
    Bp, Tp, _ = x_prompt.shape
    zero_states = [(jnp.zeros((Bp, RWKV_HEADS, RWKV_HEAD, RWKV_HEAD), jnp.float32),
                    jnp.zeros((Bp, D_MODEL), x_prompt.dtype)) for _ in range(N_A)]
    y_prompt, rw_p, (k_p, v_p) = _trunk(x_prompt, c_prompt, jnp.arange(Tp), zero_states, None, P)
    y_sample, rw_s, (k_s, v_s) = _trunk(
        x_sample, c_sample, PAST_LEN + jnp.arange(x_sample.shape[1]),
        [(state_wkv_0, state_shift_0), (state_wkv_1, state_shift_1)], (cache_k, cache_v), P)
    return (y_prompt, y_sample, rw_p[0][0], rw_s[0][0], rw_p[0][1], rw_s[0][1],
            rw_p[1][0], rw_s[1][0], rw_p[1][1], rw_s[1][1], k_p, k_s, v_p, v_s)
```

```python
import functools
import math

import jax
import jax.numpy as jnp
from jax import lax
from jax.experimental import pallas as pl
from jax.experimental.pallas import tpu as pltpu

F32 = jnp.float32
BF16 = jnp.bfloat16

D_MODEL = 1024
HEAD = 64
N_HEADS = D_MODEL // HEAD
N_KV = 4
GROUP = N_HEADS // N_KV
KV_DIM = N_KV * HEAD
WINDOW = 128
ROPE_DIM = HEAD // 4
ROPE_THETA = 500000.0
D_FF = 2816
N_EXPERTS = 8
EXPERT_FF = D_MODEL
RMS_EPS = 1e-6
GN_EPS = 64e-5
NEG = -1e30
LANES = 128
PAIRS = D_MODEL // LANES
WKV_ROWS = 64
VMEM_LIMIT = 56 * 1024 * 1024
TM = 512
TM_RWKV = 256
PAST_LEN = 8192


def _cparams(*sem):
    return pltpu.CompilerParams(dimension_semantics=sem, vmem_limit_bytes=VMEM_LIMIT)


def _dot(a, b):
    return jnp.dot(a.astype(BF16), b.astype(BF16), preferred_element_type=F32)


def _dot_nt(a, b):
    return lax.dot_general(a.astype(BF16), b.astype(BF16), (((1,), (1,)), ((), ())),
                           preferred_element_type=F32)


def _dot_tn(a, b):
    return lax.dot_general(a.astype(BF16), b.astype(BF16), (((0,), (0,)), ((), ())),
                           preferred_element_type=F32)


def _split_dot(m01, x, parts):
    acc = None
    rem = x
    for p in range(parts):
        piece = rem.astype(BF16)
        term = jnp.dot(m01, piece, preferred_element_type=F32)
        acc = term if acc is None else acc + term
        if p + 1 < parts:
            rem = rem - piece.astype(F32)
    return acc


def _head_ones():
    r = lax.broadcasted_iota(jnp.int32, (LANES, LANES), 0) // HEAD
    c = lax.broadcasted_iota(jnp.int32, (LANES, LANES), 1) // HEAD
    return jnp.where(r == c, 1.0, 0.0).astype(BF16)


def _seg_sum(x):
    rows, width = x.shape
    ncol = width // LANES
    ones = _head_ones()
    stacked = jnp.concatenate([x[:, c * LANES:(c + 1) * LANES] for c in range(ncol)], axis=0)
    s = _split_dot_right(stacked, ones)
    return jnp.concatenate([s[c * rows:(c + 1) * rows] for c in range(ncol)], axis=1)


def _split_dot_right(x, m01):
    hi = x.astype(BF16)
    lo = (x - hi.astype(F32)).astype(BF16)
    return (jnp.dot(hi, m01, preferred_element_type=F32)
            + jnp.dot(lo, m01, preferred_element_type=F32))


def _sigmoid(x):
    return 1.0 / (1.0 + jnp.exp(-x))


def _rms_mod(x, g, sh, sc):
    ms = jnp.mean(x * x, axis=-1, keepdims=True)
    y = x * lax.rsqrt(ms + RMS_EPS) * g
    return y * (1.0 + sc) + sh


def _rope_cols(x, cos, sin_up, sin_dn):
    cols = []
    for c in range(x.shape[1] // LANES):
        xc = x[:, c * LANES:(c + 1) * LANES]
        up = pltpu.roll(xc, LANES - ROPE_DIM // 2, 1)
        dn = pltpu.roll(xc, ROPE_DIM // 2, 1)
        cols.append(xc * cos + up * sin_up + dn * sin_dn)
    return jnp.concatenate(cols, axis=1)


def _full_spec(shape):
    nd = len(shape)
    return pl.BlockSpec(shape, lambda *_: (0,) * nd)


def _row_spec(tm, width):
    return pl.BlockSpec((tm, width), lambda i, *_: (i, 0))


def _mod_spec(mod, tiles_per_mod):
    return pl.BlockSpec((None,) + mod.shape[1:], lambda i, *_: (i // tiles_per_mod, 0, 0))


def _mod_operands(vecs, seq_len, tm):
    if seq_len >= tm:
        return [v[:, None, :] for v in vecs], seq_len // tm
    ops = []
    for v in vecs:
        rep = jnp.repeat(v, seq_len, axis=0)
        ops.append(rep.reshape(rep.shape[0] // tm, tm, v.shape[-1]))
    return ops, 1


def _ada_kernel(c_ref, w_ref, b_ref, o_ref):
    c = c_ref[...]
    o_ref[...] = _dot(c * _sigmoid(c), w_ref[...]) + b_ref[...]


def _ada(c, w, b, tn=1024):
    m = c.shape[0]
    nl, _, n = w.shape
    tn = min(tn, n)
    return pl.pallas_call(
        _ada_kernel,
        grid=(nl, n // tn),
        in_specs=[pl.BlockSpec((m, D_MODEL), lambda l, j: (0, 0)),
                  pl.BlockSpec((None, D_MODEL, tn), lambda l, j: (l, 0, j)),
                  pl.BlockSpec((None, 1, tn), lambda l, j: (l, 0, j))],
        out_specs=pl.BlockSpec((None, m, tn), lambda l, j: (l, 0, j)),
        out_shape=jax.ShapeDtypeStruct((nl, m, n), F32),
        compiler_params=_cparams("arbitrary", "arbitrary"),
        name="ada",
    )(c, w, b)


def _rwkv_proj_kernel(*refs, seq_len, tiles_per_seq, has_vres):
    it = iter(refs)
    x_ref, sh_ref, sc_ref, ng_ref, vecs_ref, hb_ref = (next(it) for _ in range(6))
    wr_ref, wk_ref, wv_ref, w1_ref, w2_ref, a1_ref, a2_ref, g1_ref, g2_ref = (next(it) for _ in range(9))
    if has_vres:
        v0_ref, v1_ref, v2_ref, vf_ref = (next(it) for _ in range(4))
    r_o, wl_o, k_o, v_o, a_o, b_o, g_o, hl_o = (next(it) for _ in range(8))
    carry = next(it)

    i = pl.program_id(0)
    tm = x_ref.shape[0]
    h = _rms_mod(x_ref[...], ng_ref[...], sh_ref[...], sc_ref[...])
    hl_o[...] = h[tm - hl_o.shape[0]:, :]
    row = lax.broadcasted_iota(jnp.int32, h.shape, 0)
    prev = pltpu.roll(h, 1, 0)
    if seq_len >= tm:
        @pl.when(i % tiles_per_seq == 0)
        def _():
            carry[0:1, :] = hb_ref[...]
        prev = jnp.where(row == 0, carry[0:1, :], prev)
        carry[0:1, :] = h[tm - 1:tm, :]
    else:
        prev = jnp.where(row % seq_len == 0, hb_ref[...], prev)
    xx = prev - h

    def mix(j):
        return (h + xx * vecs_ref[j:j + 1, :]).astype(BF16)

    r = _dot(mix(0), wr_ref[...])
    k = _dot(mix(2), wk_ref[...])
    xv = mix(3)
    v = _dot(xv, wv_ref[...])
    wz = vecs_ref[6:7, :] + _dot(jnp.tanh(_dot(mix(1), w1_ref[...])), w2_ref[...])
    wl_o[...] = -math.exp(-0.5) * _sigmoid(wz)
    a = _sigmoid(vecs_ref[7:8, :] + _dot(_dot(mix(4), a1_ref[...]), a2_ref[...]))
    g_o[...] = _dot(_sigmoid(_dot(mix(5), g1_ref[...])), g2_ref[...])
    if has_vres:
        mv = _sigmoid(v0_ref[...] + _dot(_dot(xv, v1_ref[...]), v2_ref[...]))
        v = v + (vf_ref[...] - v) * mv
    kk = k * vecs_ref[8:9, :]
    nrm = jnp.sqrt(_seg_sum(kk * kk))
    kk = kk / jnp.maximum(nrm, 1e-12)
    r_o[...] = r
    k_o[...] = k * (1.0 + (a - 1.0) * vecs_ref[9:10, :])
    v_o[...] = v
    a_o[...] = -kk
    b_o[...] = kk * a


def _rwkv_proj(x, sh, sc, ng, vecs, h_last, w, vres, *, seq_len, tm):
    n = x.shape[0]
    tm = min(tm, n)
    nt = n // tm
    (sh, sc, hb), mod_tiles = _mod_operands([sh, sc, h_last], seq_len, tm)
    if seq_len >= tm:
        hl_rows, hl_shape = 1, (n // seq_len, 1, D_MODEL)
    else:
        hl_rows, hl_shape = tm, (nt, tm, D_MODEL)
    in_specs = [_row_spec(tm, D_MODEL), _mod_spec(sh, mod_tiles), _mod_spec(sc, mod_tiles),
                _full_spec(ng.shape), _full_spec(vecs.shape), _mod_spec(hb, mod_tiles)]
    in_specs += [_full_spec(t.shape) for t in w]
    args = [x, sh, sc, ng, vecs, hb, *w]
    if vres is not None:
        v0, v1, v2, vf = vres
        in_specs += [_full_spec(v0.shape), _full_spec(v1.shape), _full_spec(v2.shape), _row_spec(tm, D_MODEL)]
        args += [v0, v1, v2, vf]
    tok = jax.ShapeDtypeStruct((n, D_MODEL), F32)
    *outs, hl = pl.pallas_call(
        functools.partial(_rwkv_proj_kernel, seq_len=seq_len, tiles_per_seq=mod_tiles,
                          has_vres=vres is not None),
        grid=(nt,),
        in_specs=in_specs,
        out_specs=[_row_spec(tm, D_MODEL)] * 7
                  + [pl.BlockSpec((None, hl_rows, D_MODEL), lambda i: (i // mod_tiles, 0, 0))],
        out_shape=[tok] * 7 + [jax.ShapeDtypeStruct(hl_shape, F32)],
        scratch_shapes=[pltpu.VMEM((8, D_MODEL), F32)],
        compiler_params=_cparams("arbitrary"),
        name="rwkv_proj",
    )(*args)
    if seq_len >= tm:
        hl = hl[:, 0, :]
    else:
        hl = hl.reshape(n // seq_len, seq_len, D_MODEL)[:, -1]
    return (*outs, hl)


def _wkv_kernel(r_ref, wl_ref, k_ref, v_ref, a_ref, b_ref, s0_ref, y_ref, so_ref, s_scr, *, nb, c):
    ci = pl.program_id(1)
    nrows = 2 * WKV_ROWS
    lane_head = lax.broadcasted_iota(jnp.int32, (c, LANES), 1) // HEAD
    ri = lax.broadcasted_iota(jnp.int32, (nrows, nrows), 0)
    cj = lax.broadcasted_iota(jnp.int32, (nrows, nrows), 1)
    same = (ri // c) == (cj // c)
    strict = jnp.where(same & (ri > cj), 1.0, 0.0)
    incl = jnp.where(same & (ri >= cj), 1.0, 0.0)
    eye = jnp.where(ri == cj, 1.0, 0.0)
    blk = jnp.where((ri // HEAD) == (cj // HEAD), 1.0, 0.0)
    ti = lax.broadcasted_iota(jnp.int32, (WKV_ROWS, WKV_ROWS), 0)
    tj = lax.broadcasted_iota(jnp.int32, (WKV_ROWS, WKV_ROWS), 1)
    cum01 = jnp.where(((ti // c) == (tj // c)) & (ti >= tj), 1.0, 0.0).astype(BF16)
    tot01 = jnp.where((ti // c) == (tj // c), 1.0, 0.0).astype(BF16)
    zero_h = jnp.zeros((HEAD, HEAD), F32)

    def stack(x):
        parts = []
        for g in range(nb):
            xg = x[g * c:(g + 1) * c]
            parts.append(jnp.where(lane_head == 0, xg, 0.0))
            parts.append(jnp.where(lane_head == 1, xg, 0.0))
        return jnp.concatenate(parts, axis=0)

    def dup(x):
        parts = []
        for g in range(nb):
            xg = x[g * c:(g + 1) * c]
            parts += [xg, xg]
        return jnp.concatenate(parts, axis=0)

    def fold(x2):
        parts = []
        for g in range(nb):
            parts.append(x2[2 * g * c:(2 * g + 1) * c] + x2[(2 * g + 1) * c:(2 * g + 2) * c])
        return jnp.concatenate(parts, axis=0)

    for j in range(PAIRS):
        ls = slice(j * LANES, (j + 1) * LANES)

        @pl.when(ci == 0)
        def _():
            for g in range(nb):
                top = jnp.concatenate([s0_ref[g, 2 * j], zero_h], axis=1)
                bot = jnp.concatenate([zero_h, s0_ref[g, 2 * j + 1]], axis=1)
                s_scr[g * PAIRS + j] = jnp.concatenate([top, bot], axis=0)

        r = r_ref[:, :, ls].reshape(WKV_ROWS, LANES)
        wl = wl_ref[:, :, ls].reshape(WKV_ROWS, LANES)
        k = k_ref[:, :, ls].reshape(WKV_ROWS, LANES)
        v = v_ref[:, :, ls].reshape(WKV_ROWS, LANES)
        a = a_ref[:, :, ls].reshape(WKV_ROWS, LANES)
        b = b_ref[:, :, ls].reshape(WKV_ROWS, LANES)

        gc = _split_dot(cum01, wl, 3)
        gt = _split_dot(tot01, wl, 3)
        e_in = jnp.exp(gc)
        e_out = jnp.exp(-gc)
        e_end = jnp.exp(gt - gc)
        at = a * jnp.exp(gc - wl)
        bt = b * e_out
        kt = k * e_out
        rt = r * e_in
        bh = b * e_end
        kh = k * e_end

        rt2 = stack(rt)
        at2 = stack(at)
        q = _dot_nt(jnp.concatenate([at2, rt2], axis=0),
                    jnp.concatenate([dup(bt), dup(kt)], axis=0))
        m_ab = q[:nrows, :nrows] * strict
        m_ak = q[:nrows, nrows:] * strict
        a_rb = q[nrows:, :nrows] * incl
        a_rk = q[nrows:, nrows:] * incl

        p = m_ab
        tinv = eye + m_ab
        span = 2
        while span < c:
            p = _dot(p, p)
            tinv = tinv + _dot(tinv, p)
            span *= 2
        v2 = stack(v)
        w2 = _dot(tinv, at2)
        uv = _dot(tinv, _dot(m_ak, v2))
        yv = _dot(a_rk, v2)

        u_parts, rs_parts = [], []
        for g in range(nb):
            rows = slice(2 * g * c, 2 * (g + 1) * c)
            s = s_scr[g * PAIRS + j]
            u_parts.append(_dot_nt(w2[rows], s))
            rs_parts.append(_dot_nt(rt2[rows], s))
        u2 = (u_parts[0] if nb == 1 else jnp.concatenate(u_parts, axis=0)) + uv
        rs = rs_parts[0] if nb == 1 else jnp.concatenate(rs_parts, axis=0)
        y2 = rs + _dot(a_rb, u2) + yv
        y_ref[:, :, ls] = fold(y2).reshape(nb, c, LANES)

        u = fold(u2)
        for g in range(nb):
            rows = slice(g * c, (g + 1) * c)
            upd = _dot_tn(jnp.concatenate([u[rows], v[rows]], axis=0),
                          jnp.concatenate([bh[rows], kh[rows]], axis=0))
            decay = jnp.exp(gt[g * c:g * c + 1, :])
            s_new = s_scr[g * PAIRS + j] * decay + upd * blk
            s_scr[g * PAIRS + j] = s_new

            @pl.when(ci == pl.num_programs(1) - 1)
            def _():
                so_ref[g, 2 * j] = s_new[:HEAD, :HEAD]
                so_ref[g, 2 * j + 1] = s_new[HEAD:, HEAD:]


def _wkv(r, wl, k, v, a, b, s0):
    bsz, t, _ = r.shape
    c = min(t, WKV_ROWS)
    nb = WKV_ROWS // c
    seq_spec = pl.BlockSpec((nb, c, D_MODEL), lambda i, j: (i, j, 0))
    st_spec = pl.BlockSpec((nb, N_HEADS, HEAD, HEAD), lambda i, j: (i, 0, 0, 0))
    return pl.pallas_call(
        functools.partial(_wkv_kernel, nb=nb, c=c),
        grid=(bsz // nb, t // c),
        in_specs=[seq_spec] * 6 + [st_spec],
        out_specs=[seq_spec, st_spec],
        out_shape=[jax.ShapeDtypeStruct((bsz, t, D_MODEL), F32),
                   jax.ShapeDtypeStruct((bsz, N_HEADS, HEAD, HEAD), F32)],
        scratch_shapes=[pltpu.VMEM((nb * PAIRS, LANES, LANES), F32)],
        compiler_params=_cparams("arbitrary", "arbitrary"),
        name="wkv",
    )(r, wl, k, v, a, b, s0)


def _rwkv_post_kernel(y_ref, r_ref, k_ref, v_ref, g_ref, x_ref, gate_ref, vecs_ref, rk_ref, wo_ref, o_ref):
    y = y_ref[...]
    inv = 1.0 / HEAD
    mu = _seg_sum(y) * inv
    d = y - mu
    var = _seg_sum(d * d) * inv
    yn = d * lax.rsqrt(var + GN_EPS) * vecs_ref[10:11, :] + vecs_ref[11:12, :]
    v = v_ref[...]
    bonus = _seg_sum(r_ref[...] * k_ref[...] * rk_ref[...]) * v
    z = (yn + bonus) * g_ref[...]
    o_ref[...] = x_ref[...] + gate_ref[...] * _dot(z, wo_ref[...])


def _rwkv_post(y, r, k, v, g, x, gate, vecs, rk, wo, *, seq_len, tm):
    n = x.shape[0]
    tm = min(tm, n)
    (gate,), tiles_per_mod = _mod_operands([gate], seq_len, tm)
    return pl.pallas_call(
        _rwkv_post_kernel,
        grid=(n // tm,),
        in_specs=[_row_spec(tm, D_MODEL)] * 6 + [_mod_spec(gate, tiles_per_mod), _full_spec(vecs.shape),
                                                 _full_spec(rk.shape), _full_spec(wo.shape)],
        out_specs=_row_spec(tm, D_MODEL),
        out_shape=jax.ShapeDtypeStruct((n, D_MODEL), F32),
        compiler_params=_cparams("arbitrary"),
        name="rwkv_post",
    )(y, r, k, v, g, x, gate, vecs, rk, wo)


def _ffn_kernel(x_ref, sh_ref, sc_ref, gate_ref, ng_ref, wg_ref, wu_ref, wd_ref, o_ref, h_scr, acc_scr):
    cidx = pl.program_id(1)

    @pl.when(cidx == 0)
    def _():
        h_scr[...] = _rms_mod(x_ref[...], ng_ref[...], sh_ref[...], sc_ref[...]).astype(BF16)
        acc_scr[...] = jnp.zeros_like(acc_scr)

    hb = h_scr[...]
    gt = jnp.dot(hb, wg_ref[...], preferred_element_type=F32)
    up = jnp.dot(hb, wu_ref[...], preferred_element_type=F32)
    acc_scr[...] += _dot(gt * _sigmoid(gt) * up, wd_ref[...])

    @pl.when(cidx == pl.num_programs(1) - 1)
    def _():
        o_ref[...] = x_ref[...] + gate_ref[...] * acc_scr[...]


def _ffn(x, sh, sc, gate, ng, wgu, wd, *, seq_len, tm, tf=1408):
    n = x.shape[0]
    tm = min(tm, n)
    (sh, sc, gate), tiles_per_mod = _mod_operands([sh, sc, gate], seq_len, tm)
    nc = D_FF // tf
    return pl.pallas_call(
        _ffn_kernel,
        grid=(n // tm, nc),
        in_specs=[_row_spec(tm, D_MODEL), _mod_spec(sh, tiles_per_mod), _mod_spec(sc, tiles_per_mod),
                  _mod_spec(gate, tiles_per_mod), _full_spec(ng.shape),
                  pl.BlockSpec((D_MODEL, tf), lambda i, j: (0, j)),
                  pl.BlockSpec((D_MODEL, tf), lambda i, j: (0, nc + j)),
                  pl.BlockSpec((tf, D_MODEL), lambda i, j: (j, 0))],
        out_specs=_row_spec(tm, D_MODEL),
        out_shape=jax.ShapeDtypeStruct((n, D_MODEL), F32),
        scratch_shapes=[pltpu.VMEM((tm, D_MODEL), BF16), pltpu.VMEM((tm, D_MODEL), F32)],
        compiler_params=_cparams("arbitrary", "arbitrary"),
        name="ffn",
    )(x, sh, sc, gate, ng, wgu, wgu, wd)


def _moe_kernel(x_ref, sh_ref, sc_ref, gate_ref, ng_ref, rw_ref, rb_ref, wgu_ref, wd_ref, o_ref,
                h_scr, comb_scr, acc_scr):
    e = pl.program_id(1)
    tm = x_ref.shape[0]
    lane = lax.broadcasted_iota(jnp.int32, (tm, LANES), 1)

    @pl.when(e == 0)
    def _():
        h = _rms_mod(x_ref[...], ng_ref[...], sh_ref[...], sc_ref[...]).astype(BF16)
        h_scr[...] = h
        logits = jnp.dot(h, rw_ref[...], preferred_element_type=F32) + rb_ref[...]
        m1 = jnp.max(logits, axis=1, keepdims=True)
        i1 = jnp.min(jnp.where(logits == m1, lane, LANES), axis=1, keepdims=True)
        rest = jnp.where(lane == i1, NEG * 2, logits)
        m2 = jnp.max(rest, axis=1, keepdims=True)
        i2 = jnp.min(jnp.where(rest == m2, lane, LANES), axis=1, keepdims=True)
        e2 = jnp.exp(m2 - m1)
        den = 1.0 + e2
        comb_scr[...] = jnp.where(lane == i1, 1.0 / den, 0.0) + jnp.where(lane == i2, e2 / den, 0.0)
        acc_scr[...] = jnp.zeros_like(acc_scr)

    gu = jnp.dot(h_scr[...], wgu_ref[...], preferred_element_type=F32)
    gt = gu[:, :EXPERT_FF]
    out_e = _dot(gt * _sigmoid(gt) * gu[:, EXPERT_FF:], wd_ref[...])
    ce = jnp.sum(jnp.where(lane == e, comb_scr[...], 0.0), axis=1, keepdims=True)
    acc_scr[...] += ce * out_e

    @pl.when(e == pl.num_programs(1) - 1)
    def _():
        o_ref[...] = x_ref[...] + gate_ref[...] * acc_scr[...]


def _moe(x, sh, sc, gate, ng, rw, rb, wgu, wd, *, seq_len, tm):
    n = x.shape[0]
    tm = min(tm, n)
    (sh, sc, gate), tiles_per_mod = _mod_operands([sh, sc, gate], seq_len, tm)
    return pl.pallas_call(
        _moe_kernel,
        grid=(n // tm, N_EXPERTS),
        in_specs=[_row_spec(tm, D_MODEL), _mod_spec(sh, tiles_per_mod), _mod_spec(sc, tiles_per_mod),
                  _mod_spec(gate, tiles_per_mod), _full_spec(ng.shape), _full_spec(rw.shape),
                  _full_spec(rb.shape),
                  pl.BlockSpec((None, D_MODEL, 2 * EXPERT_FF), lambda i, e: (e, 0, 0)),
                  pl.BlockSpec((None, EXPERT_FF, D_MODEL), lambda i, e: (e, 0, 0))],
        out_specs=_row_spec(tm, D_MODEL),
        out_shape=jax.ShapeDtypeStruct((n, D_MODEL), F32),
        scratch_shapes=[pltpu.VMEM((tm, D_MODEL), BF16), pltpu.VMEM((tm, LANES), F32),
                        pltpu.VMEM((tm, D_MODEL), F32)],
        compiler_params=_cparams("arbitrary", "arbitrary"),
        name="moe",
    )(x, sh, sc, gate, ng, rw, rb, wgu, wd)


def _head_norm_rope(t, g, cos, sin_up, sin_dn):
    ms = _seg_sum(t * t) * (1.0 / HEAD)
    return _rope_cols(t * lax.rsqrt(ms + RMS_EPS) * g, cos, sin_up, sin_dn)


def _kv_proj_kernel(x_ref, sh_ref, sc_ref, ng_ref, w_ref, kg_ref, cos_ref, su_ref, sd_ref, k_o, v_o):
    h = _rms_mod(x_ref[...], ng_ref[...], sh_ref[...], sc_ref[...])
    kv = _dot(h, w_ref[...])
    k_o[...] = _head_norm_rope(kv[:, :KV_DIM], kg_ref[...], cos_ref[...], su_ref[...], sd_ref[...])
    v_o[...] = kv[:, KV_DIM:]


def _kv_proj(x, sh, sc, ng, w, kg, rope, *, seq_len, tm):
    n = x.shape[0]
    tm = min(tm, n)
    (sh, sc), tiles_per_mod = _mod_operands([sh, sc], seq_len, tm)
    return pl.pallas_call(
        _kv_proj_kernel,
        grid=(n // tm,),
        in_specs=[_row_spec(tm, D_MODEL), _mod_spec(sh, tiles_per_mod), _mod_spec(sc, tiles_per_mod),
                  _full_spec(ng.shape), _full_spec(w.shape), _full_spec(kg.shape)]
                 + [_row_spec(tm, LANES)] * 3,
        out_specs=[_row_spec(tm, KV_DIM)] * 2,
        out_shape=[jax.ShapeDtypeStruct((n, KV_DIM), F32)] * 2,
        compiler_params=_cparams("arbitrary"),
        name="kv_proj",
    )(x, sh, sc, ng, w, kg, *rope)


def _q_proj_kernel(x_ref, sh_ref, sc_ref, ng_ref, w_ref, qg_ref, cos_ref, su_ref, sd_ref, q_o):
    h = _rms_mod(x_ref[...], ng_ref[...], sh_ref[...], sc_ref[...])
    q = _head_norm_rope(_dot(h, w_ref[...]), qg_ref[...], cos_ref[...], su_ref[...], sd_ref[...])
    q_o[...] = (q * HEAD ** -0.5).astype(BF16)


def _q_proj(x, sh, sc, ng, w, qg, rope, *, seq_len, tm):
    n = x.shape[0]
    tm = min(tm, n)
    (sh, sc), tiles_per_mod = _mod_operands([sh, sc], seq_len, tm)
    return pl.pallas_call(
        _q_proj_kernel,
        grid=(n // tm,),
        in_specs=[_row_spec(tm, D_MODEL), _mod_spec(sh, tiles_per_mod), _mod_spec(sc, tiles_per_mod),
                  _full_spec(ng.shape), _full_spec(w.shape), _full_spec(qg.shape)]
                 + [_row_spec(tm, LANES)] * 3,
        out_specs=_row_spec(tm, D_MODEL),
        out_shape=jax.ShapeDtypeStruct((n, D_MODEL), BF16),
        compiler_params=_cparams("arbitrary"),
        name="q_proj",
    )(x, sh, sc, ng, w, qg, *rope)


def _o_proj_kernel(a_ref, x_ref, gate_ref, w_ref, o_ref):
    o_ref[...] = x_ref[...] + gate_ref[...] * jnp.dot(a_ref[...], w_ref[...], preferred_element_type=F32)


def _o_proj(a, x, gate, w, *, seq_len, tm):
    n = x.shape[0]
    tm = min(tm, n)
    (gate,), tiles_per_mod = _mod_operands([gate], seq_len, tm)
    return pl.pallas_call(
        _o_proj_kernel,
        grid=(n // tm,),
        in_specs=[_row_spec(tm, D_MODEL), _row_spec(tm, D_MODEL), _mod_spec(gate, tiles_per_mod),
                  _full_spec(w.shape)],
        out_specs=_row_spec(tm, D_MODEL),
        out_shape=jax.ShapeDtypeStruct((n, D_MODEL), F32),
        compiler_params=_cparams("arbitrary"),
        name="o_proj",
    )(a, x, gate, w)


def _attend(q, k2, v2, valid, sink_ref):
    tq = q.shape[0]
    half = lax.broadcasted_iota(jnp.int32, (tq, LANES), 1) // HEAD
    kcols, vcols = [], []
    for kc in range(KV_DIM // LANES):
        kcol = k2[:, kc * LANES:(kc + 1) * LANES]
        vcol = v2[:, kc * LANES:(kc + 1) * LANES]
        kcols.append((kcol.astype(BF16), pltpu.roll(kcol, HEAD, 1).astype(BF16)))
        vcols.append((vcol.astype(BF16), pltpu.roll(vcol, HEAD, 1).astype(BF16)))
    out_cols = []
    for qc in range(PAIRS):
        qcol = q[:, qc * LANES:(qc + 1) * LANES].astype(F32)
        halves = []
        for he in range(2):
            hd = 2 * qc + he
            grp = hd // GROUP
            swap = (grp % 2) != he
            kx = kcols[grp // 2][1 if swap else 0]
            vx = vcols[grp // 2][1 if swap else 0]
            qm = jnp.where(half == he, qcol, 0.0).astype(BF16)
            s = lax.dot_general(qm, kx, (((1,), (1,)), ((), ())), preferred_element_type=F32)
            s = jnp.where(valid, s, NEG)
            sink = sink_ref[hd:hd + 1, 0:1]
            m = jnp.maximum(jnp.max(s, axis=1, keepdims=True), sink)
            p = jnp.exp(s - m)
            den = jnp.sum(p, axis=1, keepdims=True) + jnp.exp(sink - m)
            p = p / den
            halves.append(jnp.dot(p.astype(BF16), vx, preferred_element_type=F32))
        out_cols.append(jnp.where(half == 0, halves[0], halves[1]))
    return jnp.concatenate(out_cols, axis=1).astype(BF16)


def _attn_band_kernel(q_ref, k_ref, kp_ref, v_ref, vp_ref, sink_ref, o_ref):
    i = pl.program_id(1)
    nblk = q_ref.shape[0] // WINDOW
    r = lax.broadcasted_iota(jnp.int32, (WINDOW, 2 * WINDOW), 0)
    c = lax.broadcasted_iota(jnp.int32, (WINDOW, 2 * WINDOW), 1)
    band = (c > r) & (c <= r + WINDOW)
    for jb in range(nblk):
        rows = slice(jb * WINDOW, (jb + 1) * WINDOW)
        if jb == 0:
            kprev, vprev = kp_ref[...], vp_ref[...]
            valid = band & (c >= jnp.where(i > 0, 0, WINDOW))
        else:
            prows = slice((jb - 1) * WINDOW, jb * WINDOW)
            kprev, vprev = k_ref[prows, :], v_ref[prows, :]
            valid = band
        k2 = jnp.concatenate([kprev, k_ref[rows, :]], axis=0)
        v2 = jnp.concatenate([vprev, v_ref[rows, :]], axis=0)
        o_ref[rows, :] = _attend(q_ref[rows, :], k2, v2, valid, sink_ref)


def _attn_band(q, k, v, sinks, *, tq):
    bsz, t, _ = q.shape
    tq = min(tq, t)
    per = tq // WINDOW
    cur = lambda w: pl.BlockSpec((None, tq, w), lambda b, i: (b, i, 0))
    prv = pl.BlockSpec((None, WINDOW, KV_DIM), lambda b, i: (b, jnp.maximum(i * per - 1, 0), 0))
    return pl.pallas_call(
        _attn_band_kernel,
        grid=(bsz, t // tq),
        in_specs=[cur(D_MODEL), cur(KV_DIM), prv, cur(KV_DIM), prv, _full_spec(sinks.shape)],
        out_specs=cur(D_MODEL),
        out_shape=jax.ShapeDtypeStruct((bsz, t, D_MODEL), BF16),
        compiler_params=_cparams("arbitrary", "arbitrary"),
        name="attn_band",
    )(q, k, k, v, v, sinks)


def _attn_cache_kernel(q_ref, ck_ref, kn_ref, cv_ref, vn_ref, sink_ref, o_ref, k_scr, v_scr):
    tq = q_ref.shape[0]
    buf = ck_ref.shape[0]
    k_scr[...] = jnp.zeros_like(k_scr)
    v_scr[...] = jnp.zeros_like(v_scr)
    k_scr[0:buf, :] = ck_ref[...]
    v_scr[0:buf, :] = cv_ref[...]
    k_scr[buf:buf + tq, :] = kn_ref[...]
    v_scr[buf:buf + tq, :] = vn_ref[...]
    r = lax.broadcasted_iota(jnp.int32, (tq, k_scr.shape[0]), 0)
    c = lax.broadcasted_iota(jnp.int32, (tq, k_scr.shape[0]), 1)
    valid = ((c < buf) & (c > r)) | ((c >= buf) & (c <= buf + r))
    o_ref[...] = _attend(q_ref[...].astype(BF16), k_scr[...], v_scr[...], valid, sink_ref)


def _attn_cache(q, ck, kn, cv, vn, sinks):
    bsz, tq, _ = q.shape
    buf = ck.shape[1]
    spec = lambda rows, w: pl.BlockSpec((None, rows, w), lambda b: (b, 0, 0))
    return pl.pallas_call(
        _attn_cache_kernel,
        grid=(bsz,),
        in_specs=[spec(tq, D_MODEL), spec(buf, KV_DIM), spec(tq, KV_DIM), spec(buf, KV_DIM),
                  spec(tq, KV_DIM), _full_spec(sinks.shape)],
        out_specs=spec(tq, D_MODEL),
        out_shape=jax.ShapeDtypeStruct((bsz, tq, D_MODEL), BF16),
        scratch_shapes=[pltpu.VMEM((2 * buf, KV_DIM), F32)] * 2,
        compiler_params=_cparams("arbitrary"),
        name="attn_cache",
    )(q, ck, kn, cv, vn, sinks)


def _rope_tables(pos):
    half = ROPE_DIM // 2
    inv = ROPE_THETA ** (-jnp.arange(half, dtype=F32) * 2.0 / ROPE_DIM)
    ang = pos.astype(F32)[:, None] * inv
    cos, sin = jnp.cos(ang), jnp.sin(ang)
    t = pos.shape[0]
    pad = jnp.zeros((t, HEAD - ROPE_DIM), F32)
    z = jnp.zeros((t, half), F32)
    cos_h = jnp.concatenate([cos, cos, pad + 1.0], axis=1)
    up_h = jnp.concatenate([-sin, z, pad], axis=1)
    dn_h = jnp.concatenate([z, sin, pad], axis=1)
    tile = lambda a: jnp.concatenate([a, a], axis=1)
    return tile(cos_h), tile(up_h), tile(dn_h)


def _prep_weights(p):
    bf = lambda a: a.astype(BF16)
    pad_cols = lambda a: jnp.pad(a, ((0, 0), (0, LANES - a.shape[1])))
    pad_rows = lambda a: jnp.pad(a, ((0, LANES - a.shape[0]), (0, 0)))
    n_a = p['rwkv_w_rkv'].shape[0]
    w = dict(
        ada_w=bf(p['ada_w']), ada_b=p['ada_b'][:, None, :],
        kv_ada_w=bf(p['kv_ada_w'])[None], kv_ada_b=p['kv_ada_b'][None, None, :],
        rwkv=[(bf(p['rwkv_w_rkv'][l, 0]), bf(p['rwkv_w_rkv'][l, 1]), bf(p['rwkv_w_rkv'][l, 2]),
               bf(pad_cols(p['rwkv_w1'][l])), bf(pad_rows(p['rwkv_w2'][l])),
               bf(pad_cols(p['rwkv_a1'][l])), bf(pad_rows(p['rwkv_a2'][l])),
               bf(p['rwkv_g1'][l]), bf(p['rwkv_g2'][l])) for l in range(n_a)],
        rwkv_vecs=[jnp.pad(p['rwkv_vecs'][l], ((0, 4), (0, 0))) for l in range(n_a)],
        rwkv_vres=[None] + [(p['rwkv_v0'][l][None, :], bf(pad_cols(p['rwkv_v1'][l])),
                             bf(pad_rows(p['rwkv_v2'][l]))) for l in range(n_a - 1)],
        rwkv_wo=[bf(p['rwkv_w_o'][l]) for l in range(n_a)],
        rwkv_rk=[p['rwkv_r_k'][l].reshape(1, D_MODEL) for l in range(n_a)],
        w_kv=bf(p['attn_w_kv']),
        k_g=jnp.tile(p['k_norm_g'], N_KV)[None, :],
        w_q=[bf(p['attn_w_q'][j]) for j in range(p['attn_w_q'].shape[0])],
        q_g=[jnp.tile(p['q_norm_g'][j], N_HEADS)[None, :] for j in range(p['attn_w_q'].shape[0])],
        w_o=[bf(p['attn_w_o'][j]) for j in range(p['attn_w_q'].shape[0])],
        sinks=[jnp.broadcast_to(p['attn_sinks'][j][:, None], (N_HEADS, LANES)) for j in range(p['attn_w_q'].shape[0])],
        ffn_gu=[bf(p['ffn_w_gu'][i]) for i in range(p['ffn_w_gu'].shape[0])],
        ffn_d=[bf(p['ffn_w_down'][i]) for i in range(p['ffn_w_gu'].shape[0])],
        moe_rw=[bf(pad_cols(p['moe_router_w'][i])) for i in range(p['moe_router_w'].shape[0])],
        moe_rb=[jnp.pad(p['moe_router_b'][i], (0, LANES - N_EXPERTS), constant_values=NEG)[None, :]
                for i in range(p['moe_router_w'].shape[0])],
        moe_gu=[bf(p['moe_w_gu'][i]) for i in range(p['moe_router_w'].shape[0])],
        moe_d=[bf(p['moe_w_down'][i]) for i in range(p['moe_router_w'].shape[0])],
        norm_g=p['norm_g'], kv_norm_g=p['kv_norm_g'][None, :],
    )
    return w


def _trunk(x, ada, kv_ada, pos, states, cache, w, *, tm):
    bsz, t, _ = x.shape
    n = bsz * t
    depth = ada.shape[0]
    n_a = len(w['rwkv'])
    tm_small = min(tm, TM_RWKV)
    rope_tok = tuple(jnp.tile(a, (bsz, 1)) for a in _rope_tables(pos))
    tpad = (-t) % 8
    seq = lambda z: z.reshape(bsz, t, z.shape[-1])
    pad8 = lambda z: jnp.pad(seq(z), ((0, 0), (0, tpad), (0, 0))) if tpad else seq(z)
    xf = x.reshape(n, D_MODEL)
    new_states = []
    v_first = None
    kv_out = None
    for l in range(depth):
        sh1, sc1, g1, sh2, sc2, g2 = jnp.split(ada[l], 6, axis=-1)
        ng1 = w['norm_g'][l, 0][None, :]
        ng2 = w['norm_g'][l, 1][None, :]
        if l < n_a:
            s0, hl0 = states[l]
            vres = None if l == 0 else (*w['rwkv_vres'][l], v_first)
            r, wl, k, v, a, b, g, h_last = _rwkv_proj(xf, sh1, sc1, ng1, w['rwkv_vecs'][l], hl0, w['rwkv'][l],
                                                      vres, seq_len=t, tm=tm_small)
            if l == 0:
                v_first = v
            y, s_new = _wkv(pad8(r), pad8(wl), pad8(k), pad8(v), pad8(a), pad8(b), s0)
            y = y[:, :t].reshape(n, D_MODEL)
            xf = _rwkv_post(y, r, k, v, g, xf, g1, w['rwkv_vecs'][l], w['rwkv_rk'][l], w['rwkv_wo'][l],
                            seq_len=t, tm=tm_small)
            new_states.append((s_new, h_last))
        else:
            j = l - n_a
            if j == 0:
                ksh, ksc = jnp.split(kv_ada, 2, axis=-1)
                k_new, v_new = _kv_proj(xf, ksh, ksc, w['kv_norm_g'], w['w_kv'], w['k_g'], rope_tok,
                                        seq_len=t, tm=tm)
                k_new, v_new = seq(k_new), seq(v_new)
                if cache is None:
                    buf = min(WINDOW, t)
                    kv_out = (k_new[:, t - buf:], v_new[:, t - buf:])
                else:
                    ck, cv = cache
                    buf = ck.shape[1]
                    kv_out = (jnp.concatenate([ck, k_new], axis=1)[:, -buf:],
                              jnp.concatenate([cv, v_new], axis=1)[:, -buf:])
            q = _q_proj(xf, sh1, sc1, ng1, w['w_q'][j], w['q_g'][j], rope_tok, seq_len=t, tm=tm)
            if cache is None:
                o = _attn_band(seq(q), k_new, v_new, w['sinks'][j], tq=tm)
            else:
                o = _attn_cache(pad8(q.astype(F32)), ck, pad8(k_new), cv, pad8(v_new), w['sinks'][j])[:, :t]
            xf = _o_proj(o.reshape(n, D_MODEL), xf, g1, w['w_o'][j], seq_len=t, tm=tm)
        if l % 2 == 0:
            xf = _ffn(xf, sh2, sc2, g2, ng2, w['ffn_gu'][l // 2], w['ffn_d'][l // 2], seq_len=t, tm=tm)
        else:
            xf = _moe(xf, sh2, sc2, g2, ng2, w['moe_rw'][l // 2], w['moe_rb'][l // 2], w['moe_gu'][l // 2],
                      w['moe_d'][l // 2], seq_len=t, tm=tm)
    return xf.reshape(bsz, t, D_MODEL), new_states, kv_out


def kernel(x_prompt, x_sample, c_prompt, c_sample, state_wkv_0, state_shift_0, state_wkv_1, state_shift_1, cache_k, cache_v, ada_w, ada_b, norm_g, rwkv_vecs, rwkv_w_rkv, rwkv_w_o, rwkv_w1, rwkv_w2, rwkv_a1, rwkv_a2, rwkv_g1, rwkv_g2, rwkv_r_k, rwkv_v0, rwkv_v1, rwkv_v2, kv_ada_w, kv_ada_b, kv_norm_g, attn_w_kv, k_norm_g, attn_w_q, q_norm_g, attn_w_o, attn_sinks, ffn_w_gu, ffn_w_down, moe_router_w, moe_router_b, moe_w_gu, moe_w_down):
    p = dict(ada_w=ada_w, ada_b=ada_b, norm_g=norm_g, rwkv_vecs=rwkv_vecs, rwkv_w_rkv=rwkv_w_rkv,
             rwkv_w_o=rwkv_w_o, rwkv_w1=rwkv_w1, rwkv_w2=rwkv_w2, rwkv_a1=rwkv_a1, rwkv_a2=rwkv_a2,
             rwkv_g1=rwkv_g1, rwkv_g2=rwkv_g2, rwkv_r_k=rwkv_r_k, rwkv_v0=rwkv_v0, rwkv_v1=rwkv_v1,
             rwkv_v2=rwkv_v2, kv_ada_w=kv_ada_w, kv_ada_b=kv_ada_b, kv_norm_g=kv_norm_g,
             attn_w_kv=attn_w_kv, k_norm_g=k_norm_g, attn_w_q=attn_w_q, q_norm_g=q_norm_g,
             attn_w_o=attn_w_o, attn_sinks=attn_sinks, ffn_w_gu=ffn_w_gu, ffn_w_down=ffn_w_down,
             moe_router_w=moe_router_w, moe_router_b=moe_router_b, moe_w_gu=moe_w_gu,
             moe_w_down=moe_w_down)
    return _forward(x_prompt, x_sample, c_prompt, c_sample, state_wkv_0, state_shift_0, state_wkv_1,
                    state_shift_1, cache_k, cache_v, p)


def _forward(x_prompt, x_sample, c_prompt, c_sample, state_wkv_0, state_shift_0, state_wkv_1, state_shift_1,
             cache_k, cache_v, p):
    w = _prep_weights(p)
    bp, tp, _ = x_prompt.shape
    bs, ts, _ = x_sample.shape
    c_all = jnp.concatenate([c_prompt, c_sample], axis=0)
    c_all = jnp.pad(c_all, ((0, (-c_all.shape[0]) % 8), (0, 0)))
    ada = _ada(c_all, w['ada_w'], w['ada_b'])
    kv_ada = _ada(c_all, w['kv_ada_w'], w['kv_ada_b'])[0]
    n_a = len(w['rwkv'])
    zero_states = [(jnp.zeros((bp, N_HEADS, HEAD, HEAD), F32), jnp.zeros((bp, D_MODEL), F32)) for _ in range(n_a)]
    cache_k2 = cache_k.reshape(bs, cache_k.shape[1], KV_DIM)
    cache_v2 = cache_v.reshape(bs, cache_v.shape[1], KV_DIM)
    y_p, rw_p, (k_p, v_p) = _trunk(x_prompt, ada[:, :bp], kv_ada[:bp], jnp.arange(tp), zero_states, None, w,
                                   tm=TM)
    y_s, rw_s, (k_s, v_s) = _trunk(x_sample, ada[:, bp:bp + bs], kv_ada[bp:bp + bs], PAST_LEN + jnp.arange(ts),
                                   [(state_wkv_0, state_shift_0), (state_wkv_1, state_shift_1)],
                                   (cache_k2, cache_v2), w, tm=TM)
    heads = lambda z: z.reshape(z.shape[0], z.shape[1], N_KV, HEAD)
    return (y_p, y_s, rw_p[0][0], rw_s[0][0], rw_p[0][1], rw_s[0][1],
            rw_p[1][0], rw_s[1][0], rw_p[1][1], rw_s[1][1], heads(k_p), heads(k_s), heads(v_p), heads(v_s))
```

```python
import functools
import math

import jax
import jax.numpy as jnp
from jax import lax
from jax.experimental import pallas as pl
from jax.experimental.pallas import tpu as pltpu

F32 = jnp.float32
BF16 = jnp.bfloat16

D_MODEL = 1024
HEAD = 64
N_HEADS = D_MODEL // HEAD
N_KV = 4
GROUP = N_HEADS // N_KV
KV_DIM = N_KV * HEAD
WINDOW = 128
ROPE_DIM = HEAD // 4
ROPE_THETA = 500000.0
D_FF = 2816
N_EXPERTS = 8
EXPERT_FF = D_MODEL
RMS_EPS = 1e-6
GN_EPS = 64e-5
NEG = -1e30
LANES = 128
PAIRS = D_MODEL // LANES
WKV_ROWS = 64
WKV_GROUP = 8
VMEM_LIMIT = 56 * 1024 * 1024
TM = 512
TM_RWKV = 256
PAST_LEN = 8192


def _cparams(*sem):
    return pltpu.CompilerParams(dimension_semantics=sem, vmem_limit_bytes=VMEM_LIMIT)


def _dot(a, b):
    return jnp.dot(a.astype(BF16), b.astype(BF16), preferred_element_type=F32)


def _dot_nt(a, b):
    return lax.dot_general(a.astype(BF16), b.astype(BF16), (((1,), (1,)), ((), ())),
                           preferred_element_type=F32)


def _dot_tn(a, b):
    return lax.dot_general(a.astype(BF16), b.astype(BF16), (((0,), (0,)), ((), ())),
                           preferred_element_type=F32)


def _split_dot(m01, x, parts):
    acc = None
    rem = x
    for p in range(parts):
        piece = rem.astype(BF16)
        term = jnp.dot(m01, piece, preferred_element_type=F32)
        acc = term if acc is None else acc + term
        if p + 1 < parts:
            rem = rem - piece.astype(F32)
    return acc


def _head_ones():
    r = lax.broadcasted_iota(jnp.int32, (LANES, LANES), 0) // HEAD
    c = lax.broadcasted_iota(jnp.int32, (LANES, LANES), 1) // HEAD
    return jnp.where(r == c, 1.0, 0.0).astype(BF16)


def _seg_sum(x):
    rows, width = x.shape
    ncol = width // LANES
    ones = _head_ones()
    stacked = jnp.concatenate([x[:, c * LANES:(c + 1) * LANES] for c in range(ncol)], axis=0)
    s = _split_dot_right(stacked, ones)
    return jnp.concatenate([s[c * rows:(c + 1) * rows] for c in range(ncol)], axis=1)


def _split_dot_right(x, m01):
    hi = x.astype(BF16)
    lo = (x - hi.astype(F32)).astype(BF16)
    return (jnp.dot(hi, m01, preferred_element_type=F32)
            + jnp.dot(lo, m01, preferred_element_type=F32))


def _sigmoid(x):
    return 1.0 / (1.0 + jnp.exp(-x))


def _rms_mod(x, g, sh, sc):
    ms = jnp.mean(x * x, axis=-1, keepdims=True)
    y = x * lax.rsqrt(ms + RMS_EPS) * g
    return y * (1.0 + sc) + sh


def _rope_cols(x, cos, sin_up, sin_dn):
    cols = []
    for c in range(x.shape[1] // LANES):
        xc = x[:, c * LANES:(c + 1) * LANES]
        up = pltpu.roll(xc, LANES - ROPE_DIM // 2, 1)
        dn = pltpu.roll(xc, ROPE_DIM // 2, 1)
        cols.append(xc * cos + up * sin_up + dn * sin_dn)
    return jnp.concatenate(cols, axis=1)


def _full_spec(shape):
    nd = len(shape)
    return pl.BlockSpec(shape, lambda *_: (0,) * nd)


def _row_spec(tm, width):
    return pl.BlockSpec((tm, width), lambda i, *_: (i, 0))


def _mod_spec(mod, tiles_per_mod):
    return pl.BlockSpec((None,) + mod.shape[1:], lambda i, *_: (i // tiles_per_mod, 0, 0))


def _mod_operands(vecs, seq_len, tm):
    if seq_len >= tm:
        return [v[:, None, :] for v in vecs], seq_len // tm
    ops = []
    for v in vecs:
        rep = jnp.repeat(v, seq_len, axis=0)
        ops.append(rep.reshape(rep.shape[0] // tm, tm, v.shape[-1]))
    return ops, 1


def _ada_kernel(c_ref, w_ref, b_ref, o_ref):
    c = c_ref[...]
    o_ref[...] = _dot(c * _sigmoid(c), w_ref[...]) + b_ref[...]


def _ada(c, w, b, tn=1024):
    m = c.shape[0]
    nl, _, n = w.shape
    tn = min(tn, n)
    return pl.pallas_call(
        _ada_kernel,
        grid=(nl, n // tn),
        in_specs=[pl.BlockSpec((m, D_MODEL), lambda l, j: (0, 0)),
                  pl.BlockSpec((None, D_MODEL, tn), lambda l, j: (l, 0, j)),
                  pl.BlockSpec((None, 1, tn), lambda l, j: (l, 0, j))],
        out_specs=pl.BlockSpec((None, m, tn), lambda l, j: (l, 0, j)),
        out_shape=jax.ShapeDtypeStruct((nl, m, n), F32),
        compiler_params=_cparams("arbitrary", "arbitrary"),
        name="ada",
    )(c, w, b)


def _rwkv_proj_kernel(*refs, seq_len, tiles_per_seq, has_vres):
    it = iter(refs)
    x_ref, sh_ref, sc_ref, ng_ref, vecs_ref, hb_ref = (next(it) for _ in range(6))
    wr_ref, wk_ref, wv_ref, w1_ref, w2_ref, a1_ref, a2_ref, g1_ref, g2_ref = (next(it) for _ in range(9))
    if has_vres:
        v0_ref, v1_ref, v2_ref, vf_ref = (next(it) for _ in range(4))
    r_o, wl_o, k_o, v_o, a_o, b_o, g_o, hl_o = (next(it) for _ in range(8))
    carry = next(it)

    i = pl.program_id(0)
    tm = x_ref.shape[0]
    h = _rms_mod(x_ref[...], ng_ref[...], sh_ref[...], sc_ref[...])
    hl_o[...] = h[tm - hl_o.shape[0]:, :]
    row = lax.broadcasted_iota(jnp.int32, h.shape, 0)
    prev = pltpu.roll(h, 1, 0)
    if seq_len >= tm:
        @pl.when(i % tiles_per_seq == 0)
        def _():
            carry[0:1, :] = hb_ref[...]
        prev = jnp.where(row == 0, carry[0:1, :], prev)
        carry[0:1, :] = h[tm - 1:tm, :]
    else:
        prev = jnp.where(row % seq_len == 0, hb_ref[...], prev)
    xx = prev - h

    def mix(j):
        return (h + xx * vecs_ref[j:j + 1, :]).astype(BF16)

    r = _dot(mix(0), wr_ref[...])
    k = _dot(mix(2), wk_ref[...])
    xv = mix(3)
    v = _dot(xv, wv_ref[...])
    wz = vecs_ref[6:7, :] + _dot(jnp.tanh(_dot(mix(1), w1_ref[...])), w2_ref[...])
    wl_o[...] = -math.exp(-0.5) * _sigmoid(wz)
    a = _sigmoid(vecs_ref[7:8, :] + _dot(_dot(mix(4), a1_ref[...]), a2_ref[...]))
    g_o[...] = _dot(_sigmoid(_dot(mix(5), g1_ref[...])), g2_ref[...])
    if has_vres:
        mv = _sigmoid(v0_ref[...] + _dot(_dot(xv, v1_ref[...]), v2_ref[...]))
        v = v + (vf_ref[...] - v) * mv
    kk = k * vecs_ref[8:9, :]
    nrm = jnp.sqrt(_seg_sum(kk * kk))
    kk = kk / jnp.maximum(nrm, 1e-12)
    r_o[...] = r
    k_o[...] = k * (1.0 + (a - 1.0) * vecs_ref[9:10, :])
    v_o[...] = v
    a_o[...] = -kk
    b_o[...] = kk * a


def _rwkv_proj(x, sh, sc, ng, vecs, h_last, w, vres, *, seq_len, tm):
    n = x.shape[0]
    tm = min(tm, n)
    nt = n // tm
    (sh, sc, hb), mod_tiles = _mod_operands([sh, sc, h_last], seq_len, tm)
    if seq_len >= tm:
        hl_rows, hl_shape = 1, (n // seq_len, 1, D_MODEL)
    else:
        hl_rows, hl_shape = tm, (nt, tm, D_MODEL)
    in_specs = [_row_spec(tm, D_MODEL), _mod_spec(sh, mod_tiles), _mod_spec(sc, mod_tiles),
                _full_spec(ng.shape), _full_spec(vecs.shape), _mod_spec(hb, mod_tiles)]
    in_specs += [_full_spec(t.shape) for t in w]
    args = [x, sh, sc, ng, vecs, hb, *w]
    if vres is not None:
        v0, v1, v2, vf = vres
        in_specs += [_full_spec(v0.shape), _full_spec(v1.shape), _full_spec(v2.shape), _row_spec(tm, D_MODEL)]
        args += [v0, v1, v2, vf]
    tok = jax.ShapeDtypeStruct((n, D_MODEL), F32)
    *outs, hl = pl.pallas_call(
        functools.partial(_rwkv_proj_kernel, seq_len=seq_len, tiles_per_seq=mod_tiles,
                          has_vres=vres is not None),
        grid=(nt,),
        in_specs=in_specs,
        out_specs=[_row_spec(tm, D_MODEL)] * 7
                  + [pl.BlockSpec((None, hl_rows, D_MODEL), lambda i: (i // mod_tiles, 0, 0))],
        out_shape=[tok] * 7 + [jax.ShapeDtypeStruct(hl_shape, F32)],
        scratch_shapes=[pltpu.VMEM((8, D_MODEL), F32)],
        compiler_params=_cparams("arbitrary"),
        name="rwkv_proj",
    )(*args)
    if seq_len >= tm:
        hl = hl[:, 0, :]
    else:
        hl = hl.reshape(n // seq_len, seq_len, D_MODEL)[:, -1]
    return (*outs, hl)


def _wkv_kernel(r_ref, wl_ref, k_ref, v_ref, a_ref, b_ref, s0_ref, y_ref, so_ref, s_scr, *, nb, c):
    ci = pl.program_id(1)
    nrows = 2 * WKV_ROWS
    lane_head = lax.broadcasted_iota(jnp.int32, (c, LANES), 1) // HEAD
    ri = lax.broadcasted_iota(jnp.int32, (nrows, nrows), 0)
    cj = lax.broadcasted_iota(jnp.int32, (nrows, nrows), 1)
    same = (ri // c) == (cj // c)
    strict = jnp.where(same & (ri > cj), 1.0, 0.0)
    incl = jnp.where(same & (ri >= cj), 1.0, 0.0)
    eye = jnp.where(ri == cj, 1.0, 0.0)
    blk = jnp.where((ri // HEAD) == (cj // HEAD), 1.0, 0.0)
    ti = lax.broadcasted_iota(jnp.int32, (WKV_ROWS, WKV_ROWS), 0)
    tj = lax.broadcasted_iota(jnp.int32, (WKV_ROWS, WKV_ROWS), 1)
    cum01 = jnp.where(((ti // c) == (tj // c)) & (ti >= tj), 1.0, 0.0).astype(BF16)
    tot01 = jnp.where((ti // c) == (tj // c), 1.0, 0.0).astype(BF16)
    zero_h = jnp.zeros((HEAD, HEAD), F32)

    def stack(x):
        parts = []
        for g in range(nb):
            xg = x[g * c:(g + 1) * c]
            parts.append(jnp.where(lane_head == 0, xg, 0.0))
            parts.append(jnp.where(lane_head == 1, xg, 0.0))
        return jnp.concatenate(parts, axis=0)

    def dup(x):
        parts = []
        for g in range(nb):
            xg = x[g * c:(g + 1) * c]
            parts += [xg, xg]
        return jnp.concatenate(parts, axis=0)

    def fold(x2):
        parts = []
        for g in range(nb):
            parts.append(x2[2 * g * c:(2 * g + 1) * c] + x2[(2 * g + 1) * c:(2 * g + 2) * c])
        return jnp.concatenate(parts, axis=0)

    @pl.when(ci == 0)
    def _():
        for j in range(PAIRS):
            for g in range(nb):
                top = jnp.concatenate([s0_ref[g, 2 * j], zero_h], axis=1)
                bot = jnp.concatenate([zero_h, s0_ref[g, 2 * j + 1]], axis=1)
                s_scr[g * PAIRS + j] = jnp.concatenate([top, bot], axis=0)

    def load(ref, j):
        return ref[:, :, j * LANES:(j + 1) * LANES].reshape(WKV_ROWS, LANES)

    def state_dots(x2, j):
        parts = [_dot_nt(x2[2 * g * c:2 * (g + 1) * c], s_scr[g * PAIRS + j]) for g in range(nb)]
        return parts[0] if nb == 1 else jnp.concatenate(parts, axis=0)

    for j0 in range(0, PAIRS, WKV_GROUP):
        js = list(range(j0, j0 + WKV_GROUP))
        each = lambda f, *cols: [f(*args) for args in zip(*cols)]
        r, wl, k, v, a, b = ([load(ref, j) for j in js] for ref in (r_ref, wl_ref, k_ref, v_ref, a_ref, b_ref))
        gc = each(lambda x: _split_dot(cum01, x, 3), wl)
        gt = each(lambda x: _split_dot(tot01, x, 3), wl)
        e_out = each(lambda x: jnp.exp(-x), gc)
        e_end = each(lambda t, x: jnp.exp(t - x), gt, gc)
        at2 = each(lambda x, y, z: stack(x * jnp.exp(y - z)), a, gc, wl)
        rt2 = each(lambda x, y: stack(x * jnp.exp(y)), r, gc)
        bt = each(lambda x, e: x * e, b, e_out)
        kt = each(lambda x, e: x * e, k, e_out)
        q = each(lambda a2, r2, x, y: _dot_nt(jnp.concatenate([a2, r2], axis=0),
                                              jnp.concatenate([dup(x), dup(y)], axis=0)),
                 at2, rt2, bt, kt)
        m_ab = each(lambda x: x[:nrows, :nrows] * strict, q)
        m_ak = each(lambda x: x[:nrows, nrows:] * strict, q)
        a_rb = each(lambda x: x[nrows:, :nrows] * incl, q)
        a_rk = each(lambda x: x[nrows:, nrows:] * incl, q)

        p = m_ab
        tinv = each(lambda x: eye + x, m_ab)
        span = 2
        while span < c:
            p = each(lambda x: _dot(x, x), p)
            tinv = each(lambda x, y: x + _dot(x, y), tinv, p)
            span *= 2
        v2 = each(stack, v)
        w2 = each(_dot, tinv, at2)
        mv = each(_dot, m_ak, v2)
        uv = each(_dot, tinv, mv)
        yv = each(_dot, a_rk, v2)
        u2 = each(lambda x, j, y: state_dots(x, j) + y, w2, js, uv)
        rs = each(state_dots, rt2, js)
        y2 = each(lambda x, m, u_, y: x + _dot(m, u_) + y, rs, a_rb, u2, yv)
        for j, y2j in zip(js, y2):
            y_ref[:, :, j * LANES:(j + 1) * LANES] = fold(y2j).reshape(nb, c, LANES)
        u = each(fold, u2)
        bh = each(lambda x, e: x * e, b, e_end)
        kh = each(lambda x, e: x * e, k, e_end)
        for idx, j in enumerate(js):
            for g in range(nb):
                rows = slice(g * c, (g + 1) * c)
                upd = _dot_tn(jnp.concatenate([u[idx][rows], v[idx][rows]], axis=0),
                              jnp.concatenate([bh[idx][rows], kh[idx][rows]], axis=0))
                decay = jnp.exp(gt[idx][g * c:g * c + 1, :])
                s_scr[g * PAIRS + j] = s_scr[g * PAIRS + j] * decay + upd * blk

    @pl.when(ci == pl.num_programs(1) - 1)
    def _():
        for j in range(PAIRS):
            for g in range(nb):
                s_fin = s_scr[g * PAIRS + j]
                so_ref[g, 2 * j] = s_fin[:HEAD, :HEAD]
                so_ref[g, 2 * j + 1] = s_fin[HEAD:, HEAD:]


def _wkv(r, wl, k, v, a, b, s0):
    bsz, t, _ = r.shape
    c = min(t, WKV_ROWS)
    nb = WKV_ROWS // c
    seq_spec = pl.BlockSpec((nb, c, D_MODEL), lambda i, j: (i, j, 0))
    st_spec = pl.BlockSpec((nb, N_HEADS, HEAD, HEAD), lambda i, j: (i, 0, 0, 0))
    return pl.pallas_call(
        functools.partial(_wkv_kernel, nb=nb, c=c),
        grid=(bsz // nb, t // c),
        in_specs=[seq_spec] * 6 + [st_spec],
        out_specs=[seq_spec, st_spec],
        out_shape=[jax.ShapeDtypeStruct((bsz, t, D_MODEL), F32),
                   jax.ShapeDtypeStruct((bsz, N_HEADS, HEAD, HEAD), F32)],
        scratch_shapes=[pltpu.VMEM((nb * PAIRS, LANES, LANES), F32)],
        compiler_params=_cparams("arbitrary", "arbitrary"),
        name="wkv",
    )(r, wl, k, v, a, b, s0)


def _rwkv_post_kernel(y_ref, r_ref, k_ref, v_ref, g_ref, x_ref, gate_ref, vecs_ref, rk_ref, wo_ref, o_ref):
    y = y_ref[...]
    inv = 1.0 / HEAD
    mu = _seg_sum(y) * inv
    d = y - mu
    var = _seg_sum(d * d) * inv
    yn = d * lax.rsqrt(var + GN_EPS) * vecs_ref[10:11, :] + vecs_ref[11:12, :]
    v = v_ref[...]
    bonus = _seg_sum(r_ref[...] * k_ref[...] * rk_ref[...]) * v
    z = (yn + bonus) * g_ref[...]
    o_ref[...] = x_ref[...] + gate_ref[...] * _dot(z, wo_ref[...])


def _rwkv_post(y, r, k, v, g, x, gate, vecs, rk, wo, *, seq_len, tm):
    n = x.shape[0]
    tm = min(tm, n)
    (gate,), tiles_per_mod = _mod_operands([gate], seq_len, tm)
    return pl.pallas_call(
        _rwkv_post_kernel,
        grid=(n // tm,),
        in_specs=[_row_spec(tm, D_MODEL)] * 6 + [_mod_spec(gate, tiles_per_mod), _full_spec(vecs.shape),
                                                 _full_spec(rk.shape), _full_spec(wo.shape)],
        out_specs=_row_spec(tm, D_MODEL),
        out_shape=jax.ShapeDtypeStruct((n, D_MODEL), F32),
        compiler_params=_cparams("arbitrary"),
        name="rwkv_post",
    )(y, r, k, v, g, x, gate, vecs, rk, wo)


def _ffn_kernel(x_ref, sh_ref, sc_ref, gate_ref, ng_ref, wg_ref, wu_ref, wd_ref, o_ref, h_scr, acc_scr):
    cidx = pl.program_id(1)

    @pl.when(cidx == 0)
    def _():
        h_scr[...] = _rms_mod(x_ref[...], ng_ref[...], sh_ref[...], sc_ref[...]).astype(BF16)
        acc_scr[...] = jnp.zeros_like(acc_scr)

    hb = h_scr[...]
    gt = jnp.dot(hb, wg_ref[...], preferred_element_type=F32)
    up = jnp.dot(hb, wu_ref[...], preferred_element_type=F32)
    acc_scr[...] += _dot(gt * _sigmoid(gt) * up, wd_ref[...])

    @pl.when(cidx == pl.num_programs(1) - 1)
    def _():
        o_ref[...] = x_ref[...] + gate_ref[...] * acc_scr[...]


def _ffn(x, sh, sc, gate, ng, wgu, wd, *, seq_len, tm, tf=1408):
    n = x.shape[0]
    tm = min(tm, n)
    (sh, sc, gate), tiles_per_mod = _mod_operands([sh, sc, gate], seq_len, tm)
    nc = D_FF // tf
    return pl.pallas_call(
        _ffn_kernel,
        grid=(n // tm, nc),
        in_specs=[_row_spec(tm, D_MODEL), _mod_spec(sh, tiles_per_mod), _mod_spec(sc, tiles_per_mod),
                  _mod_spec(gate, tiles_per_mod), _full_spec(ng.shape),
                  pl.BlockSpec((D_MODEL, tf), lambda i, j: (0, j)),
                  pl.BlockSpec((D_MODEL, tf), lambda i, j: (0, nc + j)),
                  pl.BlockSpec((tf, D_MODEL), lambda i, j: (j, 0))],
        out_specs=_row_spec(tm, D_MODEL),
        out_shape=jax.ShapeDtypeStruct((n, D_MODEL), F32),
        scratch_shapes=[pltpu.VMEM((tm, D_MODEL), BF16), pltpu.VMEM((tm, D_MODEL), F32)],
        compiler_params=_cparams("arbitrary", "arbitrary"),
        name="ffn",
    )(x, sh, sc, gate, ng, wgu, wgu, wd)


def _moe_kernel(x_ref, sh_ref, sc_ref, gate_ref, ng_ref, rw_ref, rb_ref, wgu_ref, wd_ref, o_ref,
                h_scr, comb_scr, acc_scr):
    e = pl.program_id(1)
    tm = x_ref.shape[0]
    lane = lax.broadcasted_iota(jnp.int32, (tm, LANES), 1)

    @pl.when(e == 0)
    def _():
        h = _rms_mod(x_ref[...], ng_ref[...], sh_ref[...], sc_ref[...]).astype(BF16)
        h_scr[...] = h
        logits = jnp.dot(h, rw_ref[...], preferred_element_type=F32) + rb_ref[...]
        m1 = jnp.max(logits, axis=1, keepdims=True)
        i1 = jnp.min(jnp.where(logits == m1, lane, LANES), axis=1, keepdims=True)
        rest = jnp.where(lane == i1, NEG * 2, logits)
        m2 = jnp.max(rest, axis=1, keepdims=True)
        i2 = jnp.min(jnp.where(rest == m2, lane, LANES), axis=1, keepdims=True)
        e2 = jnp.exp(m2 - m1)
        den = 1.0 + e2
        comb_scr[...] = jnp.where(lane == i1, 1.0 / den, 0.0) + jnp.where(lane == i2, e2 / den, 0.0)
        acc_scr[...] = jnp.zeros_like(acc_scr)

    gu = jnp.dot(h_scr[...], wgu_ref[...], preferred_element_type=F32)
    gt = gu[:, :EXPERT_FF]
    out_e = _dot(gt * _sigmoid(gt) * gu[:, EXPERT_FF:], wd_ref[...])
    ce = jnp.sum(jnp.where(lane == e, comb_scr[...], 0.0), axis=1, keepdims=True)
    acc_scr[...] += ce * out_e

    @pl.when(e == pl.num_programs(1) - 1)
    def _():
        o_ref[...] = x_ref[...] + gate_ref[...] * acc_scr[...]


def _moe(x, sh, sc, gate, ng, rw, rb, wgu, wd, *, seq_len, tm):
    n = x.shape[0]
    tm = min(tm, n)
    (sh, sc, gate), tiles_per_mod = _mod_operands([sh, sc, gate], seq_len, tm)
    return pl.pallas_call(
        _moe_kernel,
        grid=(n // tm, N_EXPERTS),
        in_specs=[_row_spec(tm, D_MODEL), _mod_spec(sh, tiles_per_mod), _mod_spec(sc, tiles_per_mod),
                  _mod_spec(gate, tiles_per_mod), _full_spec(ng.shape), _full_spec(rw.shape),
                  _full_spec(rb.shape),
                  pl.BlockSpec((None, D_MODEL, 2 * EXPERT_FF), lambda i, e: (e, 0, 0)),
                  pl.BlockSpec((None, EXPERT_FF, D_MODEL), lambda i, e: (e, 0, 0))],
        out_specs=_row_spec(tm, D_MODEL),
        out_shape=jax.ShapeDtypeStruct((n, D_MODEL), F32),
        scratch_shapes=[pltpu.VMEM((tm, D_MODEL), BF16), pltpu.VMEM((tm, LANES), F32),
                        pltpu.VMEM((tm, D_MODEL), F32)],
        compiler_params=_cparams("arbitrary", "arbitrary"),
        name="moe",
    )(x, sh, sc, gate, ng, rw, rb, wgu, wd)


def _head_norm_rope(t, g, cos, sin_up, sin_dn):
    ms = _seg_sum(t * t) * (1.0 / HEAD)
    return _rope_cols(t * lax.rsqrt(ms + RMS_EPS) * g, cos, sin_up, sin_dn)


def _kv_proj_kernel(x_ref, sh_ref, sc_ref, ng_ref, w_ref, kg_ref, cos_ref, su_ref, sd_ref, k_o, v_o):
    h = _rms_mod(x_ref[...], ng_ref[...], sh_ref[...], sc_ref[...])
    kv = _dot(h, w_ref[...])
    k_o[...] = _head_norm_rope(kv[:, :KV_DIM], kg_ref[...], cos_ref[...], su_ref[...], sd_ref[...])
    v_o[...] = kv[:, KV_DIM:]


def _kv_proj(x, sh, sc, ng, w, kg, rope, *, seq_len, tm):
    n = x.shape[0]
    tm = min(tm, n)
    (sh, sc), tiles_per_mod = _mod_operands([sh, sc], seq_len, tm)
    return pl.pallas_call(
        _kv_proj_kernel,
        grid=(n // tm,),
        in_specs=[_row_spec(tm, D_MODEL), _mod_spec(sh, tiles_per_mod), _mod_spec(sc, tiles_per_mod),
                  _full_spec(ng.shape), _full_spec(w.shape), _full_spec(kg.shape)]
                 + [_row_spec(tm, LANES)] * 3,
        out_specs=[_row_spec(tm, KV_DIM)] * 2,
        out_shape=[jax.ShapeDtypeStruct((n, KV_DIM), F32)] * 2,
        compiler_params=_cparams("arbitrary"),
        name="kv_proj",
    )(x, sh, sc, ng, w, kg, *rope)


def _q_proj_kernel(x_ref, sh_ref, sc_ref, ng_ref, w_ref, qg_ref, cos_ref, su_ref, sd_ref, q_o):
    h = _rms_mod(x_ref[...], ng_ref[...], sh_ref[...], sc_ref[...])
    q = _head_norm_rope(_dot(h, w_ref[...]), qg_ref[...], cos_ref[...], su_ref[...], sd_ref[...])
    q_o[...] = (q * HEAD ** -0.5).astype(BF16)


def _q_proj(x, sh, sc, ng, w, qg, rope, *, seq_len, tm):
    n = x.shape[0]
    tm = min(tm, n)
    (sh, sc), tiles_per_mod = _mod_operands([sh, sc], seq_len, tm)
    return pl.pallas_call(
        _q_proj_kernel,
        grid=(n // tm,),
        in_specs=[_row_spec(tm, D_MODEL), _mod_spec(sh, tiles_per_mod), _mod_spec(sc, tiles_per_mod),
                  _full_spec(ng.shape), _full_spec(w.shape), _full_spec(qg.shape)]
                 + [_row_spec(tm, LANES)] * 3,
        out_specs=_row_spec(tm, D_MODEL),
        out_shape=jax.ShapeDtypeStruct((n, D_MODEL), BF16),
        compiler_params=_cparams("arbitrary"),
        name="q_proj",
    )(x, sh, sc, ng, w, qg, *rope)


def _o_proj_kernel(a_ref, x_ref, gate_ref, w_ref, o_ref):
    o_ref[...] = x_ref[...] + gate_ref[...] * jnp.dot(a_ref[...], w_ref[...], preferred_element_type=F32)


def _o_proj(a, x, gate, w, *, seq_len, tm):
    n = x.shape[0]
    tm = min(tm, n)
    (gate,), tiles_per_mod = _mod_operands([gate], seq_len, tm)
    return pl.pallas_call(
        _o_proj_kernel,
        grid=(n // tm,),
        in_specs=[_row_spec(tm, D_MODEL), _row_spec(tm, D_MODEL), _mod_spec(gate, tiles_per_mod),
                  _full_spec(w.shape)],
        out_specs=_row_spec(tm, D_MODEL),
        out_shape=jax.ShapeDtypeStruct((n, D_MODEL), F32),
        compiler_params=_cparams("arbitrary"),
        name="o_proj",
    )(a, x, gate, w)


def _attend(q, k2, v2, valid, sink_ref):
    tq = q.shape[0]
    half = lax.broadcasted_iota(jnp.int32, (tq, LANES), 1) // HEAD
    kcols, vcols = [], []
    for kc in range(KV_DIM // LANES):
        kcol = k2[:, kc * LANES:(kc + 1) * LANES]
        vcol = v2[:, kc * LANES:(kc + 1) * LANES]
        kcols.append((kcol.astype(BF16), pltpu.roll(kcol, HEAD, 1).astype(BF16)))
        vcols.append((vcol.astype(BF16), pltpu.roll(vcol, HEAD, 1).astype(BF16)))
    out_cols = []
    for qc in range(PAIRS):
        qcol = q[:, qc * LANES:(qc + 1) * LANES].astype(F32)
        halves = []
        for he in range(2):
            hd = 2 * qc + he
            grp = hd // GROUP
            swap = (grp % 2) != he
            kx = kcols[grp // 2][1 if swap else 0]
            vx = vcols[grp // 2][1 if swap else 0]
            qm = jnp.where(half == he, qcol, 0.0).astype(BF16)
            s = lax.dot_general(qm, kx, (((1,), (1,)), ((), ())), preferred_element_type=F32)
            s = jnp.where(valid, s, NEG)
            sink = sink_ref[hd:hd + 1, 0:1]
            m = jnp.maximum(jnp.max(s, axis=1, keepdims=True), sink)
            p = jnp.exp(s - m)
            den = jnp.sum(p, axis=1, keepdims=True) + jnp.exp(sink - m)
            p = p / den
            halves.append(jnp.dot(p.astype(BF16), vx, preferred_element_type=F32))
        out_cols.append(jnp.where(half == 0, halves[0], halves[1]))
    return jnp.concatenate(out_cols, axis=1).astype(BF16)


def _attn_band_kernel(q_ref, k_ref, kp_ref, v_ref, vp_ref, sink_ref, o_ref):
    i = pl.program_id(1)
    nblk = q_ref.shape[0] // WINDOW
    r = lax.broadcasted_iota(jnp.int32, (WINDOW, 2 * WINDOW), 0)
    c = lax.broadcasted_iota(jnp.int32, (WINDOW, 2 * WINDOW), 1)
    band = (c > r) & (c <= r + WINDOW)
    for jb in range(nblk):
        rows = slice(jb * WINDOW, (jb + 1) * WINDOW)
        if jb == 0:
            kprev, vprev = kp_ref[...], vp_ref[...]
            valid = band & (c >= jnp.where(i > 0, 0, WINDOW))
        else:
            prows = slice((jb - 1) * WINDOW, jb * WINDOW)
            kprev, vprev = k_ref[prows, :], v_ref[prows, :]
            valid = band
        k2 = jnp.concatenate([kprev, k_ref[rows, :]], axis=0)
        v2 = jnp.concatenate([vprev, v_ref[rows, :]], axis=0)
        o_ref[rows, :] = _attend(q_ref[rows, :], k2, v2, valid, sink_ref)


def _attn_band(q, k, v, sinks, *, tq):
    bsz, t, _ = q.shape
    tq = min(tq, t)
    per = tq // WINDOW
    cur = lambda w: pl.BlockSpec((None, tq, w), lambda b, i: (b, i, 0))
    prv = pl.BlockSpec((None, WINDOW, KV_DIM), lambda b, i: (b, jnp.maximum(i * per - 1, 0), 0))
    return pl.pallas_call(
        _attn_band_kernel,
        grid=(bsz, t // tq),
        in_specs=[cur(D_MODEL), cur(KV_DIM), prv, cur(KV_DIM), prv, _full_spec(sinks.shape)],
        out_specs=cur(D_MODEL),
        out_shape=jax.ShapeDtypeStruct((bsz, t, D_MODEL), BF16),
        compiler_params=_cparams("arbitrary", "arbitrary"),
        name="attn_band",
    )(q, k, k, v, v, sinks)


def _attn_cache_kernel(q_ref, ck_ref, kn_ref, cv_ref, vn_ref, sink_ref, o_ref, k_scr, v_scr):
    tq = q_ref.shape[0]
    buf = ck_ref.shape[0]
    k_scr[...] = jnp.zeros_like(k_scr)
    v_scr[...] = jnp.zeros_like(v_scr)
    k_scr[0:buf, :] = ck_ref[...]
    v_scr[0:buf, :] = cv_ref[...]
    k_scr[buf:buf + tq, :] = kn_ref[...]
    v_scr[buf:buf + tq, :] = vn_ref[...]
    r = lax.broadcasted_iota(jnp.int32, (tq, k_scr.shape[0]), 0)
    c = lax.broadcasted_iota(jnp.int32, (tq, k_scr.shape[0]), 1)
    valid = ((c < buf) & (c > r)) | ((c >= buf) & (c <= buf + r))
    o_ref[...] = _attend(q_ref[...].astype(BF16), k_scr[...], v_scr[...], valid, sink_ref)


def _attn_cache(q, ck, kn, cv, vn, sinks):
    bsz, tq, _ = q.shape
    buf = ck.shape[1]
    spec = lambda rows, w: pl.BlockSpec((None, rows, w), lambda b: (b, 0, 0))
    return pl.pallas_call(
        _attn_cache_kernel,
        grid=(bsz,),
        in_specs=[spec(tq, D_MODEL), spec(buf, KV_DIM), spec(tq, KV_DIM), spec(buf, KV_DIM),
                  spec(tq, KV_DIM), _full_spec(sinks.shape)],
        out_specs=spec(tq, D_MODEL),
        out_shape=jax.ShapeDtypeStruct((bsz, tq, D_MODEL), BF16),
        scratch_shapes=[pltpu.VMEM((2 * buf, KV_DIM), F32)] * 2,
        compiler_params=_cparams("arbitrary"),
        name="attn_cache",
    )(q, ck, kn, cv, vn, sinks)


def _rope_tables(pos):
    half = ROPE_DIM // 2
    inv = ROPE_THETA ** (-jnp.arange(half, dtype=F32) * 2.0 / ROPE_DIM)
    ang = pos.astype(F32)[:, None] * inv
    cos, sin = jnp.cos(ang), jnp.sin(ang)
    t = pos.shape[0]
    pad = jnp.zeros((t, HEAD - ROPE_DIM), F32)
    z = jnp.zeros((t, half), F32)
    cos_h = jnp.concatenate([cos, cos, pad + 1.0], axis=1)
    up_h = jnp.concatenate([-sin, z, pad], axis=1)
    dn_h = jnp.concatenate([z, sin, pad], axis=1)
    tile = lambda a: jnp.concatenate([a, a], axis=1)
    return tile(cos_h), tile(up_h), tile(dn_h)


def _prep_weights(p):
    bf = lambda a: a.astype(BF16)
    pad_cols = lambda a: jnp.pad(a, ((0, 0), (0, LANES - a.shape[1])))
    pad_rows = lambda a: jnp.pad(a, ((0, LANES - a.shape[0]), (0, 0)))
    n_a = p['rwkv_w_rkv'].shape[0]
    w = dict(
        ada_w=bf(p['ada_w']), ada_b=p['ada_b'][:, None, :],
        kv_ada_w=bf(p['kv_ada_w'])[None], kv_ada_b=p['kv_ada_b'][None, None, :],
        rwkv=[(bf(p['rwkv_w_rkv'][l, 0]), bf(p['rwkv_w_rkv'][l, 1]), bf(p['rwkv_w_rkv'][l, 2]),
               bf(pad_cols(p['rwkv_w1'][l])), bf(pad_rows(p['rwkv_w2'][l])),
               bf(pad_cols(p['rwkv_a1'][l])), bf(pad_rows(p['rwkv_a2'][l])),
               bf(p['rwkv_g1'][l]), bf(p['rwkv_g2'][l])) for l in range(n_a)],
        rwkv_vecs=[jnp.pad(p['rwkv_vecs'][l], ((0, 4), (0, 0))) for l in range(n_a)],
        rwkv_vres=[None] + [(p['rwkv_v0'][l][None, :], bf(pad_cols(p['rwkv_v1'][l])),
                             bf(pad_rows(p['rwkv_v2'][l]))) for l in range(n_a - 1)],
        rwkv_wo=[bf(p['rwkv_w_o'][l]) for l in range(n_a)],
        rwkv_rk=[p['rwkv_r_k'][l].reshape(1, D_MODEL) for l in range(n_a)],
        w_kv=bf(p['attn_w_kv']),
        k_g=jnp.tile(p['k_norm_g'], N_KV)[None, :],
        w_q=[bf(p['attn_w_q'][j]) for j in range(p['attn_w_q'].shape[0])],
        q_g=[jnp.tile(p['q_norm_g'][j], N_HEADS)[None, :] for j in range(p['attn_w_q'].shape[0])],
        w_o=[bf(p['attn_w_o'][j]) for j in range(p['attn_w_q'].shape[0])],
        sinks=[jnp.broadcast_to(p['attn_sinks'][j][:, None], (N_HEADS, LANES)) for j in range(p['attn_w_q'].shape[0])],
        ffn_gu=[bf(p['ffn_w_gu'][i]) for i in range(p['ffn_w_gu'].shape[0])],
        ffn_d=[bf(p['ffn_w_down'][i]) for i in range(p['ffn_w_gu'].shape[0])],
        moe_rw=[bf(pad_cols(p['moe_router_w'][i])) for i in range(p['moe_router_w'].shape[0])],
        moe_rb=[jnp.pad(p['moe_router_b'][i], (0, LANES - N_EXPERTS), constant_values=NEG)[None, :]
                for i in range(p['moe_router_w'].shape[0])],
        moe_gu=[bf(p['moe_w_gu'][i]) for i in range(p['moe_router_w'].shape[0])],
        moe_d=[bf(p['moe_w_down'][i]) for i in range(p['moe_router_w'].shape[0])],
        norm_g=p['norm_g'], kv_norm_g=p['kv_norm_g'][None, :],
    )
    return w


def _trunk(x, ada, kv_ada, pos, states, cache, w, *, tm):
    bsz, t, _ = x.shape
    n = bsz * t
    depth = ada.shape[0]
    n_a = len(w['rwkv'])
    tm_small = min(tm, TM_RWKV)
    rope_tok = tuple(jnp.tile(a, (bsz, 1)) for a in _rope_tables(pos))
    tpad = (-t) % 8
    seq = lambda z: z.reshape(bsz, t, z.shape[-1])
    pad8 = lambda z: jnp.pad(seq(z), ((0, 0), (0, tpad), (0, 0))) if tpad else seq(z)
    xf = x.reshape(n, D_MODEL)
    new_states = []
    v_first = None
    kv_out = None
    for l in range(depth):
        sh1, sc1, g1, sh2, sc2, g2 = jnp.split(ada[l], 6, axis=-1)
        ng1 = w['norm_g'][l, 0][None, :]
        ng2 = w['norm_g'][l, 1][None, :]
        if l < n_a:
            s0, hl0 = states[l]
            vres = None if l == 0 else (*w['rwkv_vres'][l], v_first)
            r, wl, k, v, a, b, g, h_last = _rwkv_proj(xf, sh1, sc1, ng1, w['rwkv_vecs'][l], hl0, w['rwkv'][l],
                                                      vres, seq_len=t, tm=tm_small)
            if l == 0:
                v_first = v
            y, s_new = _wkv(pad8(r), pad8(wl), pad8(k), pad8(v), pad8(a), pad8(b), s0)
            y = y[:, :t].reshape(n, D_MODEL)
            xf = _rwkv_post(y, r, k, v, g, xf, g1, w['rwkv_vecs'][l], w['rwkv_rk'][l], w['rwkv_wo'][l],
                            seq_len=t, tm=tm_small)
            new_states.append((s_new, h_last))
        else:
            j = l - n_a
            if j == 0:
                ksh, ksc = jnp.split(kv_ada, 2, axis=-1)
                k_new, v_new = _kv_proj(xf, ksh, ksc, w['kv_norm_g'], w['w_kv'], w['k_g'], rope_tok,
                                        seq_len=t, tm=tm)
                k_new, v_new = seq(k_new), seq(v_new)
                if cache is None:
                    buf = min(WINDOW, t)
                    kv_out = (k_new[:, t - buf:], v_new[:, t - buf:])
                else:
                    ck, cv = cache
                    buf = ck.shape[1]
                    kv_out = (jnp.concatenate([ck, k_new], axis=1)[:, -buf:],
                              jnp.concatenate([cv, v_new], axis=1)[:, -buf:])
            q = _q_proj(xf, sh1, sc1, ng1, w['w_q'][j], w['q_g'][j], rope_tok, seq_len=t, tm=tm)
            if cache is None:
                o = _attn_band(seq(q), k_new, v_new, w['sinks'][j], tq=tm)
            else:
                o = _attn_cache(pad8(q.astype(F32)), ck, pad8(k_new), cv, pad8(v_new), w['sinks'][j])[:, :t]
            xf = _o_proj(o.reshape(n, D_MODEL), xf, g1, w['w_o'][j], seq_len=t, tm=tm)
        if l % 2 == 0:
            xf = _ffn(xf, sh2, sc2, g2, ng2, w['ffn_gu'][l // 2], w['ffn_d'][l // 2], seq_len=t, tm=tm)
        else:
            xf = _moe(xf, sh2, sc2, g2, ng2, w['moe_rw'][l // 2], w['moe_rb'][l // 2], w['moe_gu'][l // 2],
                      w['moe_d'][l // 2], seq_len=t, tm=tm)
    return xf.reshape(bsz, t, D_MODEL), new_states, kv_out


def kernel(x_prompt, x_sample, c_prompt, c_sample, state_wkv_0, state_shift_0, state_wkv_1, state_shift_1, cache_k, cache_v, ada_w, ada_b, norm_g, rwkv_vecs, rwkv_w_rkv, rwkv_w_o, rwkv_w1, rwkv_w2, rwkv_a1, rwkv_a2, rwkv_g1, rwkv_g2, rwkv_r_k, rwkv_v0, rwkv_v1, rwkv_v2, kv_ada_w, kv_ada_b, kv_norm_g, attn_w_kv, k_norm_g, attn_w_q, q_norm_g, attn_w_o, attn_sinks, ffn_w_gu, ffn_w_down, moe_router_w, moe_router_b, moe_w_gu, moe_w_down):
    p = dict(ada_w=ada_w, ada_b=ada_b, norm_g=norm_g, rwkv_vecs=rwkv_vecs, rwkv_w_rkv=rwkv_w_rkv,
             rwkv_w_o=rwkv_w_o, rwkv_w1=rwkv_w1, rwkv_w2=rwkv_w2, rwkv_a1=rwkv_a1, rwkv_a2=rwkv_a2,
             rwkv_g1=rwkv_g1, rwkv_g2=rwkv_g2, rwkv_r_k=rwkv_r_k, rwkv_v0=rwkv_v0, rwkv_v1=rwkv_v1,
             rwkv_v2=rwkv_v2, kv_ada_w=kv_ada_w, kv_ada_b=kv_ada_b, kv_norm_g=kv_norm_g,
             attn_w_kv=attn_w_kv, k_norm_g=k_norm_g, attn_w_q=attn_w_q, q_norm_g=q_norm_g,
             attn_w_o=attn_w_o, attn_sinks=attn_sinks, ffn_w_gu=ffn_w_gu, ffn_w_down=ffn_w_down,
             moe_router_w=moe_router_w, moe_router_b=moe_router_b, moe_w_gu=moe_w_gu,
             moe_w_down=moe_w_down)
    return _forward(x_prompt, x_sample, c_prompt, c_sample, state_wkv_0, state_shift_0, state_wkv_1,
                    state_shift_1, cache_k, cache_v, p)


def _forward(x_prompt, x_sample, c_prompt, c_sample, state_wkv_0, state_shift_0, state_wkv_1, state_shift_1,
             cache_k, cache_v, p):
    w = _prep_weights(p)
    bp, tp, _ = x_prompt.shape
    bs, ts, _ = x_sample.shape
    c_all = jnp.concatenate([c_prompt, c_sample], axis=0)
    c_all = jnp.pad(c_all, ((0, (-c_all.shape[0]) % 8), (0, 0)))
    ada = _ada(c_all, w['ada_w'], w['ada_b'])
    kv_ada = _ada(c_all, w['kv_ada_w'], w['kv_ada_b'])[0]
    n_a = len(w['rwkv'])
    zero_states = [(jnp.zeros((bp, N_HEADS, HEAD, HEAD), F32), jnp.zeros((bp, D_MODEL), F32)) for _ in range(n_a)]
    cache_k2 = cache_k.reshape(bs, cache_k.shape[1], KV_DIM)
    cache_v2 = cache_v.reshape(bs, cache_v.shape[1], KV_DIM)
    y_p, rw_p, (k_p, v_p) = _trunk(x_prompt, ada[:, :bp], kv_ada[:bp], jnp.arange(tp), zero_states, None, w,
                                   tm=TM)
    y_s, rw_s, (k_s, v_s) = _trunk(x_sample, ada[:, bp:bp + bs], kv_ada[bp:bp + bs], PAST_LEN + jnp.arange(ts),
                                   [(state_wkv_0, state_shift_0), (state_wkv_1, state_shift_1)],
                                   (cache_k2, cache_v2), w, tm=TM)
    heads = lambda z: z.reshape(z.shape[0], z.shape[1], N_KV, HEAD)
    return (y_p, y_s, rw_p[0][0], rw_s[0][0], rw_p[0][1], rw_s[0][1],
            rw_p[1][0], rw_s[1][0], rw_p[1][1], rw_s[1][1], heads(k_p), heads(k_s), heads(v_p), heads(v_s))
```

```python
import functools
import math

import jax
import jax.numpy as jnp
from jax import lax
from jax.experimental import pallas as pl
from jax.experimental.pallas import tpu as pltpu

F32 = jnp.float32
BF16 = jnp.bfloat16

D_MODEL = 1024
HEAD = 64
N_HEADS = D_MODEL // HEAD
N_KV = 4
GROUP = N_HEADS // N_KV
KV_DIM = N_KV * HEAD
WINDOW = 128
ROPE_DIM = HEAD // 4
ROPE_THETA = 500000.0
D_FF = 2816
N_EXPERTS = 8
EXPERT_FF = D_MODEL
RMS_EPS = 1e-6
GN_EPS = 64e-5
NEG = -1e30
LANES = 128
PAIRS = D_MODEL // LANES
WKV_ROWS = 64
WKV_GROUP = 8
VMEM_LIMIT = 56 * 1024 * 1024
TM = 512
TM_RWKV = 256
PAST_LEN = 8192


def _cparams(*sem):
    return pltpu.CompilerParams(dimension_semantics=sem, vmem_limit_bytes=VMEM_LIMIT)


def _dot(a, b):
    return jnp.dot(a.astype(BF16), b.astype(BF16), preferred_element_type=F32)


def _dot_nt(a, b):
    return lax.dot_general(a.astype(BF16), b.astype(BF16), (((1,), (1,)), ((), ())),
                           preferred_element_type=F32)


def _dot_tn(a, b):
    return lax.dot_general(a.astype(BF16), b.astype(BF16), (((0,), (0,)), ((), ())),
                           preferred_element_type=F32)


def _split_dot(m01_wide, x):
    parts = m01_wide.shape[1] // x.shape[0]
    pieces = []
    rem = x
    for p in range(parts):
        piece = rem.astype(BF16)
        pieces.append(piece)
        if p + 1 < parts:
            rem = rem - piece.astype(F32)
    return jnp.dot(m01_wide, jnp.concatenate(pieces, axis=0), preferred_element_type=F32)


def _head_ones():
    r = lax.broadcasted_iota(jnp.int32, (LANES, LANES), 0) // HEAD
    c = lax.broadcasted_iota(jnp.int32, (LANES, LANES), 1) // HEAD
    return jnp.where(r == c, 1.0, 0.0).astype(BF16)


def _seg_sum(x):
    rows, width = x.shape
    ncol = width // LANES
    ones = _head_ones()
    stacked = jnp.concatenate([x[:, c * LANES:(c + 1) * LANES] for c in range(ncol)], axis=0)
    s = _split_dot_right(stacked, ones)
    return jnp.concatenate([s[c * rows:(c + 1) * rows] for c in range(ncol)], axis=1)


def _split_dot_right(x, m01):
    hi = x.astype(BF16)
    lo = (x - hi.astype(F32)).astype(BF16)
    return jnp.dot(jnp.concatenate([hi, lo], axis=1), jnp.concatenate([m01, m01], axis=0),
                   preferred_element_type=F32)


def _sigmoid(x):
    return 1.0 / (1.0 + jnp.exp(-x))


def _rms_mod(x, g, sh, sc):
    ms = jnp.mean(x * x, axis=-1, keepdims=True)
    y = x * lax.rsqrt(ms + RMS_EPS) * g
    return y * (1.0 + sc) + sh


def _rope_cols(x, cos, sin_up, sin_dn):
    cols = []
    for c in range(x.shape[1] // LANES):
        xc = x[:, c * LANES:(c + 1) * LANES]
        up = pltpu.roll(xc, LANES - ROPE_DIM // 2, 1)
        dn = pltpu.roll(xc, ROPE_DIM // 2, 1)
        cols.append(xc * cos + up * sin_up + dn * sin_dn)
    return jnp.concatenate(cols, axis=1)


def _full_spec(shape):
    nd = len(shape)
    return pl.BlockSpec(shape, lambda *_: (0,) * nd)


def _row_spec(tm, width):
    return pl.BlockSpec((tm, width), lambda i, *_: (i, 0))


def _mod_spec(mod, tiles_per_mod):
    return pl.BlockSpec((None,) + mod.shape[1:], lambda i, *_: (i // tiles_per_mod, 0, 0))


def _mod_operands(vecs, seq_len, tm):
    if seq_len >= tm:
        return [v[:, None, :] for v in vecs], seq_len // tm
    ops = []
    for v in vecs:
        rep = jnp.repeat(v, seq_len, axis=0)
        ops.append(rep.reshape(rep.shape[0] // tm, tm, v.shape[-1]))
    return ops, 1


def _ada_kernel(c_ref, w_ref, b_ref, o_ref):
    c = c_ref[...]
    o_ref[...] = _dot(c * _sigmoid(c), w_ref[...]) + b_ref[...]


def _ada(c, w, b, tn=1024):
    m = c.shape[0]
    nl, _, n = w.shape
    tn = min(tn, n)
    return pl.pallas_call(
        _ada_kernel,
        grid=(nl, n // tn),
        in_specs=[pl.BlockSpec((m, D_MODEL), lambda l, j: (0, 0)),
                  pl.BlockSpec((None, D_MODEL, tn), lambda l, j: (l, 0, j)),
                  pl.BlockSpec((None, 1, tn), lambda l, j: (l, 0, j))],
        out_specs=pl.BlockSpec((None, m, tn), lambda l, j: (l, 0, j)),
        out_shape=jax.ShapeDtypeStruct((nl, m, n), F32),
        compiler_params=_cparams("arbitrary", "arbitrary"),
        name="ada",
    )(c, w, b)


def _rwkv_proj_kernel(*refs, seq_len, tiles_per_seq, has_vres):
    it = iter(refs)
    x_ref, sh_ref, sc_ref, ng_ref, vecs_ref, hb_ref = (next(it) for _ in range(6))
    wr_ref, wk_ref, wv_ref, w1_ref, w2_ref, a1_ref, a2_ref, g1_ref, g2_ref = (next(it) for _ in range(9))
    if has_vres:
        v0_ref, v1_ref, v2_ref, vf_ref = (next(it) for _ in range(4))
    r_o, wl_o, k_o, v_o, a_o, b_o, g_o, hl_o = (next(it) for _ in range(8))
    carry = next(it)

    i = pl.program_id(0)
    tm = x_ref.shape[0]
    h = _rms_mod(x_ref[...], ng_ref[...], sh_ref[...], sc_ref[...])
    hl_o[...] = h[tm - hl_o.shape[0]:, :]
    row = lax.broadcasted_iota(jnp.int32, h.shape, 0)
    prev = pltpu.roll(h, 1, 0)
    if seq_len >= tm:
        @pl.when(i % tiles_per_seq == 0)
        def _():
            carry[0:1, :] = hb_ref[...]
        prev = jnp.where(row == 0, carry[0:1, :], prev)
        carry[0:1, :] = h[tm - 1:tm, :]
    else:
        prev = jnp.where(row % seq_len == 0, hb_ref[...], prev)
    xx = prev - h

    def mix(j):
        return (h + xx * vecs_ref[j:j + 1, :]).astype(BF16)

    r = _dot(mix(0), wr_ref[...])
    k = _dot(mix(2), wk_ref[...])
    xv = mix(3)
    v = _dot(xv, wv_ref[...])
    wz = vecs_ref[6:7, :] + _dot(jnp.tanh(_dot(mix(1), w1_ref[...])), w2_ref[...])
    wl_o[...] = -math.exp(-0.5) * _sigmoid(wz)
    a = _sigmoid(vecs_ref[7:8, :] + _dot(_dot(mix(4), a1_ref[...]), a2_ref[...]))
    g_o[...] = _dot(_sigmoid(_dot(mix(5), g1_ref[...])), g2_ref[...])
    if has_vres:
        mv = _sigmoid(v0_ref[...] + _dot(_dot(xv, v1_ref[...]), v2_ref[...]))
        v = v + (vf_ref[...] - v) * mv
    kk = k * vecs_ref[8:9, :]
    nrm = jnp.sqrt(_seg_sum(kk * kk))
    kk = kk / jnp.maximum(nrm, 1e-12)
    r_o[...] = r
    k_o[...] = k * (1.0 + (a - 1.0) * vecs_ref[9:10, :])
    v_o[...] = v
    a_o[...] = -kk
    b_o[...] = kk * a


def _rwkv_proj(x, sh, sc, ng, vecs, h_last, w, vres, *, seq_len, tm):
    n = x.shape[0]
    tm = min(tm, n)
    nt = n // tm
    (sh, sc, hb), mod_tiles = _mod_operands([sh, sc, h_last], seq_len, tm)
    if seq_len >= tm:
        hl_rows, hl_shape = 1, (n // seq_len, 1, D_MODEL)
    else:
        hl_rows, hl_shape = tm, (nt, tm, D_MODEL)
    in_specs = [_row_spec(tm, D_MODEL), _mod_spec(sh, mod_tiles), _mod_spec(sc, mod_tiles),
                _full_spec(ng.shape), _full_spec(vecs.shape), _mod_spec(hb, mod_tiles)]
    in_specs += [_full_spec(t.shape) for t in w]
    args = [x, sh, sc, ng, vecs, hb, *w]
    if vres is not None:
        v0, v1, v2, vf = vres
        in_specs += [_full_spec(v0.shape), _full_spec(v1.shape), _full_spec(v2.shape), _row_spec(tm, D_MODEL)]
        args += [v0, v1, v2, vf]
    tok = jax.ShapeDtypeStruct((n, D_MODEL), F32)
    *outs, hl = pl.pallas_call(
        functools.partial(_rwkv_proj_kernel, seq_len=seq_len, tiles_per_seq=mod_tiles,
                          has_vres=vres is not None),
        grid=(nt,),
        in_specs=in_specs,
        out_specs=[_row_spec(tm, D_MODEL)] * 7
                  + [pl.BlockSpec((None, hl_rows, D_MODEL), lambda i: (i // mod_tiles, 0, 0))],
        out_shape=[tok] * 7 + [jax.ShapeDtypeStruct(hl_shape, F32)],
        scratch_shapes=[pltpu.VMEM((8, D_MODEL), F32)],
        compiler_params=_cparams("arbitrary"),
        name="rwkv_proj",
    )(*args)
    if seq_len >= tm:
        hl = hl[:, 0, :]
    else:
        hl = hl.reshape(n // seq_len, seq_len, D_MODEL)[:, -1]
    return (*outs, hl)


def _wkv_kernel(r_ref, wl_ref, k_ref, v_ref, a_ref, b_ref, s0_ref, y_ref, so_ref, s_scr, *, nb, c):
    ci = pl.program_id(1)
    nrows = 2 * WKV_ROWS
    lane_head = lax.broadcasted_iota(jnp.int32, (c, LANES), 1) // HEAD
    ri = lax.broadcasted_iota(jnp.int32, (nrows, nrows), 0)
    cj = lax.broadcasted_iota(jnp.int32, (nrows, nrows), 1)
    same = (ri // c) == (cj // c)
    strict = jnp.where(same & (ri > cj), 1.0, 0.0)
    incl = jnp.where(same & (ri >= cj), 1.0, 0.0)
    eye = jnp.where(ri == cj, 1.0, 0.0)
    blk = jnp.where((ri // HEAD) == (cj // HEAD), 1.0, 0.0)
    ti = lax.broadcasted_iota(jnp.int32, (WKV_ROWS, 3 * WKV_ROWS), 0)
    tj = lax.broadcasted_iota(jnp.int32, (WKV_ROWS, 3 * WKV_ROWS), 1) & (WKV_ROWS - 1)
    cum01 = jnp.where(((ti // c) == (tj // c)) & (ti >= tj), 1.0, 0.0).astype(BF16)
    zero_h = jnp.zeros((HEAD, HEAD), F32)

    def seq_total(x):
        if nb == 1:
            return x[c - 1:c, :]
        return jnp.concatenate([jnp.broadcast_to(x[(g + 1) * c - 1:(g + 1) * c, :], (c, LANES))
                                for g in range(nb)], axis=0)

    def stack(x):
        parts = []
        for g in range(nb):
            xg = x[g * c:(g + 1) * c]
            parts.append(jnp.where(lane_head == 0, xg, 0.0))
            parts.append(jnp.where(lane_head == 1, xg, 0.0))
        return jnp.concatenate(parts, axis=0)

    def dup(x):
        parts = []
        for g in range(nb):
            xg = x[g * c:(g + 1) * c]
            parts += [xg, xg]
        return jnp.concatenate(parts, axis=0)

    def fold(x2):
        parts = []
        for g in range(nb):
            parts.append(x2[2 * g * c:(2 * g + 1) * c] + x2[(2 * g + 1) * c:(2 * g + 2) * c])
        return jnp.concatenate(parts, axis=0)

    @pl.when(ci == 0)
    def _():
        for j in range(PAIRS):
            for g in range(nb):
                top = jnp.concatenate([s0_ref[g, 2 * j], zero_h], axis=1)
                bot = jnp.concatenate([zero_h, s0_ref[g, 2 * j + 1]], axis=1)
                s_scr[g * PAIRS + j] = jnp.concatenate([top, bot], axis=0)

    def load(ref, j):
        return ref[:, :, j * LANES:(j + 1) * LANES].reshape(WKV_ROWS, LANES)

    def state_dots(x2, j):
        parts = [_dot_nt(x2[2 * g * c:2 * (g + 1) * c], s_scr[g * PAIRS + j]) for g in range(nb)]
        return parts[0] if nb == 1 else jnp.concatenate(parts, axis=0)

    for j0 in range(0, PAIRS, WKV_GROUP):
        js = list(range(j0, j0 + WKV_GROUP))
        each = lambda f, *cols: [f(*args) for args in zip(*cols)]
        r, wl, k, v, a, b = ([load(ref, j) for j in js] for ref in (r_ref, wl_ref, k_ref, v_ref, a_ref, b_ref))
        gc = each(lambda x: _split_dot(cum01, x), wl)
        gt = each(seq_total, gc)
        e_out = each(lambda x: jnp.exp(-x), gc)
        e_end = each(lambda t, x: jnp.exp(t - x), gt, gc)
        at2 = each(lambda x, y, z: stack(x * jnp.exp(y - z)), a, gc, wl)
        rt2 = each(lambda x, y: stack(x * jnp.exp(y)), r, gc)
        bt = each(lambda x, e: x * e, b, e_out)
        kt = each(lambda x, e: x * e, k, e_out)
        q = each(lambda a2, r2, x, y: _dot_nt(jnp.concatenate([a2, r2], axis=0),
                                              jnp.concatenate([dup(x), dup(y)], axis=0)),
                 at2, rt2, bt, kt)
        m_ab = each(lambda x: x[:nrows, :nrows] * strict, q)
        m_ak = each(lambda x: x[:nrows, nrows:] * strict, q)
        a_rb = each(lambda x: x[nrows:, :nrows] * incl, q)
        a_rk = each(lambda x: x[nrows:, nrows:] * incl, q)

        p = m_ab
        tinv = each(lambda x: eye + x, m_ab)
        span = 2
        while span < c:
            p = each(lambda x: _dot(x, x), p)
            tinv = each(lambda x, y: x + _dot(x, y), tinv, p)
            span *= 2
        v2 = each(stack, v)
        w2 = each(_dot, tinv, at2)
        mv = each(_dot, m_ak, v2)
        uv = each(_dot, tinv, mv)
        u2 = each(lambda x, j, y: state_dots(x, j) + y, w2, js, uv)
        rs = each(state_dots, rt2, js)
        y2 = each(lambda x, mb, mk, u_, v_: x + _dot(jnp.concatenate([mb, mk], axis=1),
                                                      jnp.concatenate([u_, v_], axis=0)),
                  rs, a_rb, a_rk, u2, v2)
        for j, y2j in zip(js, y2):
            y_ref[:, :, j * LANES:(j + 1) * LANES] = fold(y2j).reshape(nb, c, LANES)
        u = each(fold, u2)
        bh = each(lambda x, e: x * e, b, e_end)
        kh = each(lambda x, e: x * e, k, e_end)
        for idx, j in enumerate(js):
            for g in range(nb):
                rows = slice(g * c, (g + 1) * c)
                upd = _dot_tn(jnp.concatenate([u[idx][rows], v[idx][rows]], axis=0),
                              jnp.concatenate([bh[idx][rows], kh[idx][rows]], axis=0))
                decay = jnp.exp(gt[idx][g * c:g * c + 1, :])
                s_scr[g * PAIRS + j] = s_scr[g * PAIRS + j] * decay + upd * blk

    @pl.when(ci == pl.num_programs(1) - 1)
    def _():
        for j in range(PAIRS):
            for g in range(nb):
                s_fin = s_scr[g * PAIRS + j]
                so_ref[g, 2 * j] = s_fin[:HEAD, :HEAD]
                so_ref[g, 2 * j + 1] = s_fin[HEAD:, HEAD:]


def _wkv(r, wl, k, v, a, b, s0):
    bsz, t, _ = r.shape
    c = min(t, WKV_ROWS)
    nb = WKV_ROWS // c
    seq_spec = pl.BlockSpec((nb, c, D_MODEL), lambda i, j: (i, j, 0))
    st_spec = pl.BlockSpec((nb, N_HEADS, HEAD, HEAD), lambda i, j: (i, 0, 0, 0))
    return pl.pallas_call(
        functools.partial(_wkv_kernel, nb=nb, c=c),
        grid=(bsz // nb, t // c),
        in_specs=[seq_spec] * 6 + [st_spec],
        out_specs=[seq_spec, st_spec],
        out_shape=[jax.ShapeDtypeStruct((bsz, t, D_MODEL), F32),
                   jax.ShapeDtypeStruct((bsz, N_HEADS, HEAD, HEAD), F32)],
        scratch_shapes=[pltpu.VMEM((nb * PAIRS, LANES, LANES), F32)],
        compiler_params=_cparams("arbitrary", "arbitrary"),
        name="wkv",
    )(r, wl, k, v, a, b, s0)


def _rwkv_post_kernel(y_ref, r_ref, k_ref, v_ref, g_ref, x_ref, gate_ref, vecs_ref, rk_ref, wo_ref, o_ref):
    y = y_ref[...]
    inv = 1.0 / HEAD
    mu = _seg_sum(y) * inv
    d = y - mu
    var = _seg_sum(d * d) * inv
    yn = d * lax.rsqrt(var + GN_EPS) * vecs_ref[10:11, :] + vecs_ref[11:12, :]
    v = v_ref[...]
    bonus = _seg_sum(r_ref[...] * k_ref[...] * rk_ref[...]) * v
    z = (yn + bonus) * g_ref[...]
    o_ref[...] = x_ref[...] + gate_ref[...] * _dot(z, wo_ref[...])


def _rwkv_post(y, r, k, v, g, x, gate, vecs, rk, wo, *, seq_len, tm):
    n = x.shape[0]
    tm = min(tm, n)
    (gate,), tiles_per_mod = _mod_operands([gate], seq_len, tm)
    return pl.pallas_call(
        _rwkv_post_kernel,
        grid=(n // tm,),
        in_specs=[_row_spec(tm, D_MODEL)] * 6 + [_mod_spec(gate, tiles_per_mod), _full_spec(vecs.shape),
                                                 _full_spec(rk.shape), _full_spec(wo.shape)],
        out_specs=_row_spec(tm, D_MODEL),
        out_shape=jax.ShapeDtypeStruct((n, D_MODEL), F32),
        compiler_params=_cparams("arbitrary"),
        name="rwkv_post",
    )(y, r, k, v, g, x, gate, vecs, rk, wo)


def _ffn_kernel(x_ref, sh_ref, sc_ref, gate_ref, ng_ref, wg_ref, wu_ref, wd_ref, o_ref, h_scr, acc_scr):
    cidx = pl.program_id(1)

    @pl.when(cidx == 0)
    def _():
        h_scr[...] = _rms_mod(x_ref[...], ng_ref[...], sh_ref[...], sc_ref[...]).astype(BF16)
        acc_scr[...] = jnp.zeros_like(acc_scr)

    hb = h_scr[...]
    gt = jnp.dot(hb, wg_ref[...], preferred_element_type=F32)
    up = jnp.dot(hb, wu_ref[...], preferred_element_type=F32)
    acc_scr[...] += _dot(gt * _sigmoid(gt) * up, wd_ref[...])

    @pl.when(cidx == pl.num_programs(1) - 1)
    def _():
        o_ref[...] = x_ref[...] + gate_ref[...] * acc_scr[...]


def _ffn(x, sh, sc, gate, ng, wgu, wd, *, seq_len, tm, tf=1408):
    n = x.shape[0]
    tm = min(tm, n)
    (sh, sc, gate), tiles_per_mod = _mod_operands([sh, sc, gate], seq_len, tm)
    nc = D_FF // tf
    return pl.pallas_call(
        _ffn_kernel,
        grid=(n // tm, nc),
        in_specs=[_row_spec(tm, D_MODEL), _mod_spec(sh, tiles_per_mod), _mod_spec(sc, tiles_per_mod),
                  _mod_spec(gate, tiles_per_mod), _full_spec(ng.shape),
                  pl.BlockSpec((D_MODEL, tf), lambda i, j: (0, j)),
                  pl.BlockSpec((D_MODEL, tf), lambda i, j: (0, nc + j)),
                  pl.BlockSpec((tf, D_MODEL), lambda i, j: (j, 0))],
        out_specs=_row_spec(tm, D_MODEL),
        out_shape=jax.ShapeDtypeStruct((n, D_MODEL), F32),
        scratch_shapes=[pltpu.VMEM((tm, D_MODEL), BF16), pltpu.VMEM((tm, D_MODEL), F32)],
        compiler_params=_cparams("arbitrary", "arbitrary"),
        name="ffn",
    )(x, sh, sc, gate, ng, wgu, wgu, wd)


def _moe_kernel(x_ref, sh_ref, sc_ref, gate_ref, ng_ref, rw_ref, rb_ref, wgu_ref, wd_ref, o_ref,
                h_scr, comb_scr, acc_scr):
    e = pl.program_id(1)
    tm = x_ref.shape[0]
    lane = lax.broadcasted_iota(jnp.int32, (tm, LANES), 1)

    @pl.when(e == 0)
    def _():
        h = _rms_mod(x_ref[...], ng_ref[...], sh_ref[...], sc_ref[...]).astype(BF16)
        h_scr[...] = h
        logits = jnp.dot(h, rw_ref[...], preferred_element_type=F32) + rb_ref[...]
        m1 = jnp.max(logits, axis=1, keepdims=True)
        i1 = jnp.min(jnp.where(logits == m1, lane, LANES), axis=1, keepdims=True)
        rest = jnp.where(lane == i1, NEG * 2, logits)
        m2 = jnp.max(rest, axis=1, keepdims=True)
        i2 = jnp.min(jnp.where(rest == m2, lane, LANES), axis=1, keepdims=True)
        e2 = jnp.exp(m2 - m1)
        den = 1.0 + e2
        comb_scr[...] = jnp.where(lane == i1, 1.0 / den, 0.0) + jnp.where(lane == i2, e2 / den, 0.0)
        acc_scr[...] = jnp.zeros_like(acc_scr)

    gu = jnp.dot(h_scr[...], wgu_ref[...], preferred_element_type=F32)
    gt = gu[:, :EXPERT_FF]
    out_e = _dot(gt * _sigmoid(gt) * gu[:, EXPERT_FF:], wd_ref[...])
    ce = jnp.sum(jnp.where(lane == e, comb_scr[...], 0.0), axis=1, keepdims=True)
    acc_scr[...] += ce * out_e

    @pl.when(e == pl.num_programs(1) - 1)
    def _():
        o_ref[...] = x_ref[...] + gate_ref[...] * acc_scr[...]


def _moe(x, sh, sc, gate, ng, rw, rb, wgu, wd, *, seq_len, tm):
    n = x.shape[0]
    tm = min(tm, n)
    (sh, sc, gate), tiles_per_mod = _mod_operands([sh, sc, gate], seq_len, tm)
    return pl.pallas_call(
        _moe_kernel,
        grid=(n // tm, N_EXPERTS),
        in_specs=[_row_spec(tm, D_MODEL), _mod_spec(sh, tiles_per_mod), _mod_spec(sc, tiles_per_mod),
                  _mod_spec(gate, tiles_per_mod), _full_spec(ng.shape), _full_spec(rw.shape),
                  _full_spec(rb.shape),
                  pl.BlockSpec((None, D_MODEL, 2 * EXPERT_FF), lambda i, e: (e, 0, 0)),
                  pl.BlockSpec((None, EXPERT_FF, D_MODEL), lambda i, e: (e, 0, 0))],
        out_specs=_row_spec(tm, D_MODEL),
        out_shape=jax.ShapeDtypeStruct((n, D_MODEL), F32),
        scratch_shapes=[pltpu.VMEM((tm, D_MODEL), BF16), pltpu.VMEM((tm, LANES), F32),
                        pltpu.VMEM((tm, D_MODEL), F32)],
        compiler_params=_cparams("arbitrary", "arbitrary"),
        name="moe",
    )(x, sh, sc, gate, ng, rw, rb, wgu, wd)


def _head_norm_rope(t, g, cos, sin_up, sin_dn):
    ms = _seg_sum(t * t) * (1.0 / HEAD)
    return _rope_cols(t * lax.rsqrt(ms + RMS_EPS) * g, cos, sin_up, sin_dn)


def _kv_variants(t):
    rows = t.shape[0]
    half = lax.broadcasted_iota(jnp.int32, (rows, LANES), 1) // HEAD
    out = []
    for g in range(N_KV):
        col = t[:, (g // 2) * LANES:(g // 2 + 1) * LANES]
        swapped = pltpu.roll(col, HEAD, 1)
        for he in range(2):
            src = col if he == g % 2 else swapped
            out.append(jnp.where(half == he, src, 0.0).astype(BF16))
    return jnp.concatenate(out, axis=1)


def _kv_proj_kernel(x_ref, sh_ref, sc_ref, ng_ref, w_ref, kg_ref, cos_ref, su_ref, sd_ref, k_o, v_o, kz_o, vz_o):
    h = _rms_mod(x_ref[...], ng_ref[...], sh_ref[...], sc_ref[...])
    kv = _dot(h, w_ref[...])
    k = _head_norm_rope(kv[:, :KV_DIM], kg_ref[...], cos_ref[...], su_ref[...], sd_ref[...])
    v = kv[:, KV_DIM:]
    k_o[...] = k
    v_o[...] = v
    kz_o[...] = _kv_variants(k)
    vz_o[...] = _kv_variants(v)


def _kv_proj(x, sh, sc, ng, w, kg, rope, *, seq_len, tm):
    n = x.shape[0]
    tm = min(tm, n)
    (sh, sc), tiles_per_mod = _mod_operands([sh, sc], seq_len, tm)
    return pl.pallas_call(
        _kv_proj_kernel,
        grid=(n // tm,),
        in_specs=[_row_spec(tm, D_MODEL), _mod_spec(sh, tiles_per_mod), _mod_spec(sc, tiles_per_mod),
                  _full_spec(ng.shape), _full_spec(w.shape), _full_spec(kg.shape)]
                 + [_row_spec(tm, LANES)] * 3,
        out_specs=[_row_spec(tm, KV_DIM)] * 2 + [_row_spec(tm, D_MODEL)] * 2,
        out_shape=[jax.ShapeDtypeStruct((n, KV_DIM), F32)] * 2 + [jax.ShapeDtypeStruct((n, D_MODEL), BF16)] * 2,
        compiler_params=_cparams("arbitrary"),
        name="kv_proj",
    )(x, sh, sc, ng, w, kg, *rope)


def _q_proj_kernel(x_ref, sh_ref, sc_ref, ng_ref, w_ref, qg_ref, cos_ref, su_ref, sd_ref, q_o):
    h = _rms_mod(x_ref[...], ng_ref[...], sh_ref[...], sc_ref[...])
    q = _head_norm_rope(_dot(h, w_ref[...]), qg_ref[...], cos_ref[...], su_ref[...], sd_ref[...])
    q_o[...] = (q * HEAD ** -0.5).astype(BF16)


def _q_proj(x, sh, sc, ng, w, qg, rope, *, seq_len, tm):
    n = x.shape[0]
    tm = min(tm, n)
    (sh, sc), tiles_per_mod = _mod_operands([sh, sc], seq_len, tm)
    return pl.pallas_call(
        _q_proj_kernel,
        grid=(n // tm,),
        in_specs=[_row_spec(tm, D_MODEL), _mod_spec(sh, tiles_per_mod), _mod_spec(sc, tiles_per_mod),
                  _full_spec(ng.shape), _full_spec(w.shape), _full_spec(qg.shape)]
                 + [_row_spec(tm, LANES)] * 3,
        out_specs=_row_spec(tm, D_MODEL),
        out_shape=jax.ShapeDtypeStruct((n, D_MODEL), BF16),
        compiler_params=_cparams("arbitrary"),
        name="q_proj",
    )(x, sh, sc, ng, w, qg, *rope)


def _o_proj_kernel(a_ref, x_ref, gate_ref, w_ref, o_ref):
    o_ref[...] = x_ref[...] + gate_ref[...] * jnp.dot(a_ref[...], w_ref[...], preferred_element_type=F32)


def _o_proj(a, x, gate, w, *, seq_len, tm):
    n = x.shape[0]
    tm = min(tm, n)
    (gate,), tiles_per_mod = _mod_operands([gate], seq_len, tm)
    return pl.pallas_call(
        _o_proj_kernel,
        grid=(n // tm,),
        in_specs=[_row_spec(tm, D_MODEL), _row_spec(tm, D_MODEL), _mod_spec(gate, tiles_per_mod),
                  _full_spec(w.shape)],
        out_specs=_row_spec(tm, D_MODEL),
        out_shape=jax.ShapeDtypeStruct((n, D_MODEL), F32),
        compiler_params=_cparams("arbitrary"),
        name="o_proj",
    )(a, x, gate, w)


def _scores(q, kz):
    out = []
    for g in range(N_KV):
        qs = jnp.concatenate([q[:, (2 * g) * LANES:(2 * g + 1) * LANES],
                              q[:, (2 * g + 1) * LANES:(2 * g + 2) * LANES]], axis=0).astype(BF16)
        for he in range(2):
            col = 2 * g + he
            out.append(lax.dot_general(qs, kz[:, col * LANES:(col + 1) * LANES], (((1,), (1,)), ((), ())),
                                       preferred_element_type=F32))
    return out


def _softmax_pv(problems, valid, sink_ref):
    tq, tk = valid.shape[0] // 2, valid.shape[1]
    ones = jnp.ones((tk, LANES), BF16)
    blocks = [(pi, col) for pi in range(len(problems)) for col in range(2 * N_KV)]

    def sink_rows(col):
        ha = GROUP * (col // 2) + col % 2
        return jnp.concatenate([jnp.broadcast_to(sink_ref[ha:ha + 1, :], (tq, LANES)),
                                jnp.broadcast_to(sink_ref[ha + 2:ha + 3, :], (tq, LANES))], axis=0)

    sinks = [sink_rows(col) for col in range(2 * N_KV)]
    masked = [jnp.where(valid, problems[pi][0][col], NEG) for pi, col in blocks]
    mx = [jnp.maximum(jnp.broadcast_to(jnp.max(s, axis=1, keepdims=True), (2 * tq, LANES)), sinks[col])
          for s, (pi, col) in zip(masked, blocks)]
    probs = [jnp.concatenate([jnp.exp(s[:, c * LANES:(c + 1) * LANES] - m) for c in range(tk // LANES)],
                             axis=1).astype(BF16) for s, m in zip(masked, mx)]
    pv = [jnp.dot(p, jnp.concatenate([problems[pi][1][:, col * LANES:(col + 1) * LANES], ones], axis=1),
                  preferred_element_type=F32) for p, (pi, col) in zip(probs, blocks)]
    outs = []
    for pi in range(len(problems)):
        cols = [None] * PAIRS
        for col in range(2 * N_KV):
            idx = pi * 2 * N_KV + col
            den = pv[idx][:, LANES:] + jnp.exp(sinks[col] - mx[idx])
            o = pv[idx][:, :LANES] * (1.0 / den)
            for part, qc in ((o[:tq], 2 * (col // 2)), (o[tq:], 2 * (col // 2) + 1)):
                cols[qc] = part if cols[qc] is None else cols[qc] + part
        outs.append(jnp.concatenate(cols, axis=1).astype(BF16))
    return outs


def _attn_band_kernel(q_ref, kz_ref, kzp_ref, vz_ref, vzp_ref, sink_ref, o_ref):
    i = pl.program_id(1)
    nblk = q_ref.shape[0] // WINDOW
    r = lax.broadcasted_iota(jnp.int32, (2 * WINDOW, 2 * WINDOW), 0) & (WINDOW - 1)
    c = lax.broadcasted_iota(jnp.int32, (2 * WINDOW, 2 * WINDOW), 1)
    band = (c > r) & (c <= r + WINDOW)
    band0 = band & (c >= jnp.where(i > 0, 0, WINDOW))
    rows = lambda jb: slice(jb * WINDOW, (jb + 1) * WINDOW)

    def keys(ref, prev_ref, jb):
        prev = prev_ref[...] if jb == 0 else ref[rows(jb - 1), :]
        return jnp.concatenate([prev, ref[rows(jb), :]], axis=0)

    nxt = _scores(q_ref[rows(0), :], keys(kz_ref, kzp_ref, 0))
    for jb in range(nblk):
        cur = nxt
        if jb + 1 < nblk:
            nxt = _scores(q_ref[rows(jb + 1), :], keys(kz_ref, kzp_ref, jb + 1))
        o_ref[rows(jb), :] = _softmax_pv([(cur, keys(vz_ref, vzp_ref, jb))], band0 if jb == 0 else band,
                                         sink_ref)[0]


def _attn_band(q, kz, vz, sinks, *, tq):
    bsz, t, _ = q.shape
    tq = min(tq, t)
    per = tq // WINDOW
    cur = pl.BlockSpec((None, tq, D_MODEL), lambda b, i: (b, i, 0))
    prv = pl.BlockSpec((None, WINDOW, D_MODEL), lambda b, i: (b, jnp.maximum(i * per - 1, 0), 0))
    return pl.pallas_call(
        _attn_band_kernel,
        grid=(bsz, t // tq),
        in_specs=[cur, cur, prv, cur, prv, _full_spec(sinks.shape)],
        out_specs=cur,
        out_shape=jax.ShapeDtypeStruct((bsz, t, D_MODEL), BF16),
        compiler_params=_cparams("arbitrary", "arbitrary"),
        name="attn_band",
    )(q, kz, kz, vz, vz, sinks)


def _attn_cache_kernel(q_ref, ck_ref, kn_ref, cv_ref, vn_ref, sink_ref, o_ref):
    nseq, tq, _ = q_ref.shape
    buf = ck_ref.shape[1]
    tk = 2 * buf
    zpad = jnp.zeros((tk - buf - tq, KV_DIM), F32)
    r = lax.broadcasted_iota(jnp.int32, (2 * tq, tk), 0) & (tq - 1)
    c = lax.broadcasted_iota(jnp.int32, (2 * tq, tk), 1)
    valid = ((c < buf) & (c > r)) | ((c >= buf) & (c <= buf + r))

    def operand(cache_ref, new_ref, b):
        return _kv_variants(jnp.concatenate([cache_ref[b], new_ref[b], zpad], axis=0))

    problems = [(_scores(q_ref[b], operand(ck_ref, kn_ref, b)), operand(cv_ref, vn_ref, b)) for b in range(nseq)]
    for b, o in enumerate(_softmax_pv(problems, valid, sink_ref)):
        o_ref[b] = o


def _attn_cache(q, ck, kn, cv, vn, sinks, *, group=8):
    bsz, tq, _ = q.shape
    buf = ck.shape[1]
    group = min(group, bsz)
    spec = lambda rows, w: pl.BlockSpec((group, rows, w), lambda b: (b, 0, 0))
    return pl.pallas_call(
        _attn_cache_kernel,
        grid=(bsz // group,),
        in_specs=[spec(tq, D_MODEL), spec(buf, KV_DIM), spec(tq, KV_DIM), spec(buf, KV_DIM),
                  spec(tq, KV_DIM), _full_spec(sinks.shape)],
        out_specs=spec(tq, D_MODEL),
        out_shape=jax.ShapeDtypeStruct((bsz, tq, D_MODEL), BF16),
        compiler_params=_cparams("arbitrary"),
        name="attn_cache",
    )(q, ck, kn, cv, vn, sinks)


def _rope_tables(pos):
    half = ROPE_DIM // 2
    inv = ROPE_THETA ** (-jnp.arange(half, dtype=F32) * 2.0 / ROPE_DIM)
    ang = pos.astype(F32)[:, None] * inv
    cos, sin = jnp.cos(ang), jnp.sin(ang)
    t = pos.shape[0]
    pad = jnp.zeros((t, HEAD - ROPE_DIM), F32)
    z = jnp.zeros((t, half), F32)
    cos_h = jnp.concatenate([cos, cos, pad + 1.0], axis=1)
    up_h = jnp.concatenate([-sin, z, pad], axis=1)
    dn_h = jnp.concatenate([z, sin, pad], axis=1)
    tile = lambda a: jnp.concatenate([a, a], axis=1)
    return tile(cos_h), tile(up_h), tile(dn_h)


def _prep_weights(p):
    bf = lambda a: a.astype(BF16)
    pad_cols = lambda a: jnp.pad(a, ((0, 0), (0, LANES - a.shape[1])))
    pad_rows = lambda a: jnp.pad(a, ((0, LANES - a.shape[0]), (0, 0)))
    n_a = p['rwkv_w_rkv'].shape[0]
    w = dict(
        ada_w=bf(p['ada_w']), ada_b=p['ada_b'][:, None, :],
        kv_ada_w=bf(p['kv_ada_w'])[None], kv_ada_b=p['kv_ada_b'][None, None, :],
        rwkv=[(bf(p['rwkv_w_rkv'][l, 0]), bf(p['rwkv_w_rkv'][l, 1]), bf(p['rwkv_w_rkv'][l, 2]),
               bf(pad_cols(p['rwkv_w1'][l])), bf(pad_rows(p['rwkv_w2'][l])),
               bf(pad_cols(p['rwkv_a1'][l])), bf(pad_rows(p['rwkv_a2'][l])),
               bf(p['rwkv_g1'][l]), bf(p['rwkv_g2'][l])) for l in range(n_a)],
        rwkv_vecs=[jnp.pad(p['rwkv_vecs'][l], ((0, 4), (0, 0))) for l in range(n_a)],
        rwkv_vres=[None] + [(p['rwkv_v0'][l][None, :], bf(pad_cols(p['rwkv_v1'][l])),
                             bf(pad_rows(p['rwkv_v2'][l]))) for l in range(n_a - 1)],
        rwkv_wo=[bf(p['rwkv_w_o'][l]) for l in range(n_a)],
        rwkv_rk=[p['rwkv_r_k'][l].reshape(1, D_MODEL) for l in range(n_a)],
        w_kv=bf(p['attn_w_kv']),
        k_g=jnp.tile(p['k_norm_g'], N_KV)[None, :],
        w_q=[bf(p['attn_w_q'][j]) for j in range(p['attn_w_q'].shape[0])],
        q_g=[jnp.tile(p['q_norm_g'][j], N_HEADS)[None, :] for j in range(p['attn_w_q'].shape[0])],
        w_o=[bf(p['attn_w_o'][j]) for j in range(p['attn_w_q'].shape[0])],
        sinks=[jnp.broadcast_to(p['attn_sinks'][j][:, None], (N_HEADS, LANES)) for j in range(p['attn_w_q'].shape[0])],
        ffn_gu=[bf(p['ffn_w_gu'][i]) for i in range(p['ffn_w_gu'].shape[0])],
        ffn_d=[bf(p['ffn_w_down'][i]) for i in range(p['ffn_w_gu'].shape[0])],
        moe_rw=[bf(pad_cols(p['moe_router_w'][i])) for i in range(p['moe_router_w'].shape[0])],
        moe_rb=[jnp.pad(p['moe_router_b'][i], (0, LANES - N_EXPERTS), constant_values=NEG)[None, :]
                for i in range(p['moe_router_w'].shape[0])],
        moe_gu=[bf(p['moe_w_gu'][i]) for i in range(p['moe_router_w'].shape[0])],
        moe_d=[bf(p['moe_w_down'][i]) for i in range(p['moe_router_w'].shape[0])],
        norm_g=p['norm_g'], kv_norm_g=p['kv_norm_g'][None, :],
    )
    return w


def _trunk(x, ada, kv_ada, pos, states, cache, w, *, tm):
    bsz, t, _ = x.shape
    n = bsz * t
    depth = ada.shape[0]
    n_a = len(w['rwkv'])
    tm_small = min(tm, TM_RWKV)
    rope_tok = tuple(jnp.tile(a, (bsz, 1)) for a in _rope_tables(pos))
    tpad = (-t) % 8
    seq = lambda z: z.reshape(bsz, t, z.shape[-1])
    pad8 = lambda z: jnp.pad(seq(z), ((0, 0), (0, tpad), (0, 0))) if tpad else seq(z)
    xf = x.reshape(n, D_MODEL)
    new_states = []
    v_first = None
    kv_out = None
    for l in range(depth):
        sh1, sc1, g1, sh2, sc2, g2 = jnp.split(ada[l], 6, axis=-1)
        ng1 = w['norm_g'][l, 0][None, :]
        ng2 = w['norm_g'][l, 1][None, :]
        if l < n_a:
            s0, hl0 = states[l]
            vres = None if l == 0 else (*w['rwkv_vres'][l], v_first)
            r, wl, k, v, a, b, g, h_last = _rwkv_proj(xf, sh1, sc1, ng1, w['rwkv_vecs'][l], hl0, w['rwkv'][l],
                                                      vres, seq_len=t, tm=tm_small)
            if l == 0:
                v_first = v
            y, s_new = _wkv(pad8(r), pad8(wl), pad8(k), pad8(v), pad8(a), pad8(b), s0)
            y = y[:, :t].reshape(n, D_MODEL)
            xf = _rwkv_post(y, r, k, v, g, xf, g1, w['rwkv_vecs'][l], w['rwkv_rk'][l], w['rwkv_wo'][l],
                            seq_len=t, tm=tm_small)
            new_states.append((s_new, h_last))
        else:
            j = l - n_a
            if j == 0:
                ksh, ksc = jnp.split(kv_ada, 2, axis=-1)
                k_new, v_new, kz, vz = _kv_proj(xf, ksh, ksc, w['kv_norm_g'], w['w_kv'], w['k_g'], rope_tok,
                                                seq_len=t, tm=tm)
                k_new, v_new = seq(k_new), seq(v_new)
                if cache is None:
                    buf = min(WINDOW, t)
                    kv_out = (k_new[:, t - buf:], v_new[:, t - buf:])
                else:
                    ck, cv = cache
                    buf = ck.shape[1]
                    kv_out = (jnp.concatenate([ck, k_new], axis=1)[:, -buf:],
                              jnp.concatenate([cv, v_new], axis=1)[:, -buf:])
            q = _q_proj(xf, sh1, sc1, ng1, w['w_q'][j], w['q_g'][j], rope_tok, seq_len=t, tm=tm)
            if cache is None:
                o = _attn_band(seq(q), seq(kz), seq(vz), w['sinks'][j], tq=tm)
            else:
                o = _attn_cache(pad8(q.astype(F32)), ck, pad8(k_new), cv, pad8(v_new), w['sinks'][j])[:, :t]
            xf = _o_proj(o.reshape(n, D_MODEL), xf, g1, w['w_o'][j], seq_len=t, tm=tm)
        if l % 2 == 0:
            xf = _ffn(xf, sh2, sc2, g2, ng2, w['ffn_gu'][l // 2], w['ffn_d'][l // 2], seq_len=t, tm=tm)
        else:
            xf = _moe(xf, sh2, sc2, g2, ng2, w['moe_rw'][l // 2], w['moe_rb'][l // 2], w['moe_gu'][l // 2],
                      w['moe_d'][l // 2], seq_len=t, tm=tm)
    return xf.reshape(bsz, t, D_MODEL), new_states, kv_out


def kernel(x_prompt, x_sample, c_prompt, c_sample, state_wkv_0, state_shift_0, state_wkv_1, state_shift_1, cache_k, cache_v, ada_w, ada_b, norm_g, rwkv_vecs, rwkv_w_rkv, rwkv_w_o, rwkv_w1, rwkv_w2, rwkv_a1, rwkv_a2, rwkv_g1, rwkv_g2, rwkv_r_k, rwkv_v0, rwkv_v1, rwkv_v2, kv_ada_w, kv_ada_b, kv_norm_g, attn_w_kv, k_norm_g, attn_w_q, q_norm_g, attn_w_o, attn_sinks, ffn_w_gu, ffn_w_down, moe_router_w, moe_router_b, moe_w_gu, moe_w_down):
    p = dict(ada_w=ada_w, ada_b=ada_b, norm_g=norm_g, rwkv_vecs=rwkv_vecs, rwkv_w_rkv=rwkv_w_rkv,
             rwkv_w_o=rwkv_w_o, rwkv_w1=rwkv_w1, rwkv_w2=rwkv_w2, rwkv_a1=rwkv_a1, rwkv_a2=rwkv_a2,
             rwkv_g1=rwkv_g1, rwkv_g2=rwkv_g2, rwkv_r_k=rwkv_r_k, rwkv_v0=rwkv_v0, rwkv_v1=rwkv_v1,
             rwkv_v2=rwkv_v2, kv_ada_w=kv_ada_w, kv_ada_b=kv_ada_b, kv_norm_g=kv_norm_g,
             attn_w_kv=attn_w_kv, k_norm_g=k_norm_g, attn_w_q=attn_w_q, q_norm_g=q_norm_g,
             attn_w_o=attn_w_o, attn_sinks=attn_sinks, ffn_w_gu=ffn_w_gu, ffn_w_down=ffn_w_down,
             moe_router_w=moe_router_w, moe_router_b=moe_router_b, moe_w_gu=moe_w_gu,
             moe_w_down=moe_w_down)
    return _forward(x_prompt, x_sample, c_prompt, c_sample, state_wkv_0, state_shift_0, state_wkv_1,
                    state_shift_1, cache_k, cache_v, p)


def _forward(x_prompt, x_sample, c_prompt, c_sample, state_wkv_0, state_shift_0, state_wkv_1, state_shift_1,
             cache_k, cache_v, p):
    w = _prep_weights(p)
    bp, tp, _ = x_prompt.shape
    bs, ts, _ = x_sample.shape
    c_all = jnp.concatenate([c_prompt, c_sample], axis=0)
    c_all = jnp.pad(c_all, ((0, (-c_all.shape[0]) % 8), (0, 0)))
    ada = _ada(c_all, w['ada_w'], w['ada_b'])
    kv_ada = _ada(c_all, w['kv_ada_w'], w['kv_ada_b'])[0]
    n_a = len(w['rwkv'])
    zero_states = [(jnp.zeros((bp, N_HEADS, HEAD, HEAD), F32), jnp.zeros((bp, D_MODEL), F32)) for _ in range(n_a)]
    cache_k2 = cache_k.reshape(bs, cache_k.shape[1], KV_DIM)
    cache_v2 = cache_v.reshape(bs, cache_v.shape[1], KV_DIM)
    y_p, rw_p, (k_p, v_p) = _trunk(x_prompt, ada[:, :bp], kv_ada[:bp], jnp.arange(tp), zero_states, None, w,
                                   tm=TM)
    y_s, rw_s, (k_s, v_s) = _trunk(x_sample, ada[:, bp:bp + bs], kv_ada[bp:bp + bs], PAST_LEN + jnp.arange(ts),
                                   [(state_wkv_0, state_shift_0), (state_wkv_1, state_shift_1)],
                                   (cache_k2, cache_v2), w, tm=TM)
    heads = lambda z: z.reshape(z.shape[0], z.shape[1], N_KV, HEAD)
    return (y_p, y_s, rw_p[0][0], rw_s[0][0], rw_p[0][1], rw_s[0][1],
            rw_p[1][0], rw_s[1][0], rw_p[1][1], rw_s[1][1], heads(k_p), heads(k_s), heads(v_p), heads(v_s))
```

```python
import functools
import math

import jax
import jax.numpy as jnp
from jax import lax
from jax.experimental import pallas as pl
from jax.experimental.pallas import tpu as pltpu

F32 = jnp.float32
BF16 = jnp.bfloat16

D_MODEL = 1024
HEAD = 64
N_HEADS = D_MODEL // HEAD
N_KV = 4
GROUP = N_HEADS // N_KV
KV_DIM = N_KV * HEAD
WINDOW = 128
ROPE_DIM = HEAD // 4
ROPE_THETA = 500000.0
D_FF = 2816
N_EXPERTS = 8
EXPERT_FF = D_MODEL
RMS_EPS = 1e-6
GN_EPS = 64e-5
NEG = -1e30
LANES = 128
PAIRS = D_MODEL // LANES
WKV_ROWS = 64
WKV_SETS = 2
WKV_GROUP = 16
VMEM_LIMIT = 56 * 1024 * 1024
TM = 512
TM_RWKV = 256
PAST_LEN = 8192


def _cparams(*sem):
    return pltpu.CompilerParams(dimension_semantics=sem, vmem_limit_bytes=VMEM_LIMIT)


def _dot(a, b):
    return jnp.dot(a.astype(BF16), b.astype(BF16), preferred_element_type=F32)


def _dot_nt(a, b):
    return lax.dot_general(a.astype(BF16), b.astype(BF16), (((1,), (1,)), ((), ())),
                           preferred_element_type=F32)


def _dot_tn(a, b):
    return lax.dot_general(a.astype(BF16), b.astype(BF16), (((0,), (0,)), ((), ())),
                           preferred_element_type=F32)


def _split_dot(m01_wide, x):
    parts = m01_wide.shape[1] // x.shape[0]
    pieces = []
    rem = x
    for p in range(parts):
        piece = rem.astype(BF16)
        pieces.append(piece)
        if p + 1 < parts:
            rem = rem - piece.astype(F32)
    return jnp.dot(m01_wide, jnp.concatenate(pieces, axis=0), preferred_element_type=F32)


def _head_ones():
    r = lax.broadcasted_iota(jnp.int32, (LANES, LANES), 0) // HEAD
    c = lax.broadcasted_iota(jnp.int32, (LANES, LANES), 1) // HEAD
    return jnp.where(r == c, 1.0, 0.0).astype(BF16)


def _seg_sum(x):
    rows, width = x.shape
    ncol = width // LANES
    ones = _head_ones()
    stacked = jnp.concatenate([x[:, c * LANES:(c + 1) * LANES] for c in range(ncol)], axis=0)
    s = _split_dot_right(stacked, ones)
    return jnp.concatenate([s[c * rows:(c + 1) * rows] for c in range(ncol)], axis=1)


def _split_dot_right(x, m01):
    hi = x.astype(BF16)
    lo = (x - hi.astype(F32)).astype(BF16)
    return jnp.dot(jnp.concatenate([hi, lo], axis=1), jnp.concatenate([m01, m01], axis=0),
                   preferred_element_type=F32)


def _sigmoid(x):
    return 1.0 / (1.0 + jnp.exp(-x))


def _rms_mod(x, g, sh, sc):
    ms = jnp.mean(x * x, axis=-1, keepdims=True)
    y = x * lax.rsqrt(ms + RMS_EPS) * g
    return y * (1.0 + sc) + sh


def _rope_cols(x, cos, sin_up, sin_dn):
    cols = []
    for c in range(x.shape[1] // LANES):
        xc = x[:, c * LANES:(c + 1) * LANES]
        up = pltpu.roll(xc, LANES - ROPE_DIM // 2, 1)
        dn = pltpu.roll(xc, ROPE_DIM // 2, 1)
        cols.append(xc * cos + up * sin_up + dn * sin_dn)
    return jnp.concatenate(cols, axis=1)


def _full_spec(shape):
    nd = len(shape)
    return pl.BlockSpec(shape, lambda *_: (0,) * nd)


def _row_spec(tm, width):
    return pl.BlockSpec((tm, width), lambda i, *_: (i, 0))


def _mod_tables(vecs, seq_len, tiles):
    nvec, bsz, width = vecs.shape
    out, rep = {}, None
    for tile in set(tiles):
        if seq_len >= tile:
            out[tile] = (vecs[:, :, None, :], seq_len // tile)
        else:
            if rep is None:
                rep = jnp.repeat(vecs, seq_len, axis=1)
            out[tile] = (rep.reshape(nvec, bsz * seq_len // tile, tile, width), 1)
    return out


def _mod_specs(mods, tm):
    specs, args = [], []
    for tables, idx in mods:
        table, per_block = tables[tm]
        specs.append(pl.BlockSpec((None, None) + table.shape[2:],
                                  lambda i, *_, idx=idx, per_block=per_block: (idx, i // per_block, 0, 0)))
        args.append(table)
    return specs, args


def _rope_specs(rope, tm):
    cyc = rope[0].shape[0] // tm
    return [pl.BlockSpec((tm, LANES), lambda i, *_: (i % cyc, 0))] * 3


def _ada_kernel(c_ref, w_ref, b_ref, o_ref):
    c = c_ref[...]
    o_ref[...] = _dot(c * _sigmoid(c), w_ref[...]) + b_ref[...]


def _ada(c, w, b):
    m = c.shape[0]
    nl, _, n = w.shape
    nch = n // D_MODEL
    return pl.pallas_call(
        _ada_kernel,
        grid=(nl, nch),
        in_specs=[pl.BlockSpec((m, D_MODEL), lambda l, j: (0, 0)),
                  pl.BlockSpec((None, D_MODEL, D_MODEL), lambda l, j: (l, 0, j)),
                  pl.BlockSpec((None, 1, D_MODEL), lambda l, j: (l, 0, j))],
        out_specs=pl.BlockSpec((None, m, D_MODEL), lambda l, j: (l * nch + j, 0, 0)),
        out_shape=jax.ShapeDtypeStruct((nl * nch, m, D_MODEL), F32),
        compiler_params=_cparams("arbitrary", "arbitrary"),
        name="ada",
    )(c, w, b)


def _rwkv_proj_kernel(*refs, seq_len, tiles_per_seq, has_vres):
    it = iter(refs)
    x_ref, sh_ref, sc_ref, ng_ref, vecs_ref, hb_ref = (next(it) for _ in range(6))
    wr_ref, wk_ref, wv_ref, w1_ref, w2_ref, a1_ref, a2_ref, g1_ref, g2_ref = (next(it) for _ in range(9))
    if has_vres:
        v0_ref, v1_ref, v2_ref, vf_ref = (next(it) for _ in range(4))
    r_o, wl_o, k_o, v_o, a_o, b_o, g_o, hl_o = (next(it) for _ in range(8))
    carry = next(it)

    i = pl.program_id(0)
    tm = x_ref.shape[0]
    h = _rms_mod(x_ref[...], ng_ref[...], sh_ref[...], sc_ref[...])
    hl_o[...] = h[tm - hl_o.shape[0]:, :]
    row = lax.broadcasted_iota(jnp.int32, h.shape, 0)
    prev = pltpu.roll(h, 1, 0)
    if seq_len >= tm:
        @pl.when(i % tiles_per_seq == 0)
        def _():
            carry[0:1, :] = hb_ref[...]
        prev = jnp.where(row == 0, carry[0:1, :], prev)
        carry[0:1, :] = h[tm - 1:tm, :]
    else:
        prev = jnp.where(row % seq_len == 0, hb_ref[...], prev)
    xx = prev - h

    def mix(j):
        return (h + xx * vecs_ref[j:j + 1, :]).astype(BF16)

    r = _dot(mix(0), wr_ref[...])
    k = _dot(mix(2), wk_ref[...])
    xv = mix(3)
    v = _dot(xv, wv_ref[...])
    wz = vecs_ref[6:7, :] + _dot(jnp.tanh(_dot(mix(1), w1_ref[...])), w2_ref[...])
    wl_o[...] = -math.exp(-0.5) * _sigmoid(wz)
    a = _sigmoid(vecs_ref[7:8, :] + _dot(_dot(mix(4), a1_ref[...]), a2_ref[...]))
    g_o[...] = _dot(_sigmoid(_dot(mix(5), g1_ref[...])), g2_ref[...])
    if has_vres:
        mv = _sigmoid(v0_ref[...] + _dot(_dot(xv, v1_ref[...]), v2_ref[...]))
        v = v + (vf_ref[...] - v) * mv
    kk = k * vecs_ref[8:9, :]
    nrm = jnp.sqrt(_seg_sum(kk * kk))
    kk = kk / jnp.maximum(nrm, 1e-12)
    r_o[...] = r
    k_o[...] = k * (1.0 + (a - 1.0) * vecs_ref[9:10, :])
    v_o[...] = v
    a_o[...] = -kk
    b_o[...] = kk * a


def _rwkv_proj(x, sh, sc, ng, vecs, h_last, w, vres, *, seq_len, tm):
    n = x.shape[0]
    nt = n // tm
    mspecs, margs = _mod_specs([sh, sc, h_last], tm)
    if seq_len >= tm:
        mod_tiles, hl_rows, hl_shape = seq_len // tm, 1, (n // seq_len, 1, D_MODEL)
    else:
        mod_tiles, hl_rows, hl_shape = 1, tm, (nt, tm, D_MODEL)
    w_rkv, li, loras = w
    in_specs = [_row_spec(tm, D_MODEL), mspecs[0], mspecs[1], _full_spec(ng.shape), _full_spec(vecs.shape),
                mspecs[2]]
    in_specs += [pl.BlockSpec((None, None, D_MODEL, D_MODEL), lambda i, which=which: (li, which, 0, 0))
                 for which in range(3)]
    in_specs += [_full_spec(t.shape) for t in loras]
    args = [x, margs[0], margs[1], ng, vecs, margs[2], w_rkv, w_rkv, w_rkv, *loras]
    if vres is not None:
        v0, v1, v2, vf = vres
        in_specs += [_full_spec(v0.shape), _full_spec(v1.shape), _full_spec(v2.shape), _row_spec(tm, D_MODEL)]
        args += [v0, v1, v2, vf]
    tok = jax.ShapeDtypeStruct((n, D_MODEL), F32)
    *outs, hl = pl.pallas_call(
        functools.partial(_rwkv_proj_kernel, seq_len=seq_len, tiles_per_seq=mod_tiles,
                          has_vres=vres is not None),
        grid=(nt,),
        in_specs=in_specs,
        out_specs=[_row_spec(tm, D_MODEL)] * 7
                  + [pl.BlockSpec((None, hl_rows, D_MODEL), lambda i: (i // mod_tiles, 0, 0))],
        out_shape=[tok] * 7 + [jax.ShapeDtypeStruct(hl_shape, F32)],
        scratch_shapes=[pltpu.VMEM((8, D_MODEL), F32)],
        compiler_params=_cparams("arbitrary"),
        name="rwkv_proj",
    )(*args)
    if seq_len >= tm:
        hl = hl[:, 0, :]
    else:
        hl = hl.reshape(n // seq_len, seq_len, D_MODEL)[:, -1]
    return (*outs, hl)


def _wkv_kernel(r_ref, wl_ref, k_ref, v_ref, a_ref, b_ref, s0_ref, y_ref, so_ref, s_scr, *, nb, c):
    ci = pl.program_id(1)
    nrows = 2 * WKV_ROWS
    lane_head = lax.broadcasted_iota(jnp.int32, (c, LANES), 1) // HEAD
    ri = lax.broadcasted_iota(jnp.int32, (nrows, nrows), 0)
    cj = lax.broadcasted_iota(jnp.int32, (nrows, nrows), 1)
    same = (ri // c) == (cj // c)
    strict = jnp.where(same & (ri > cj), 1.0, 0.0)
    incl = jnp.where(same & (ri >= cj), 1.0, 0.0)
    eye = jnp.where(ri == cj, 1.0, 0.0)
    blk = jnp.where((ri // HEAD) == (cj // HEAD), 1.0, 0.0)
    ti = lax.broadcasted_iota(jnp.int32, (WKV_ROWS, 3 * WKV_ROWS), 0)
    tj = lax.broadcasted_iota(jnp.int32, (WKV_ROWS, 3 * WKV_ROWS), 1) & (WKV_ROWS - 1)
    cum01 = jnp.where(((ti // c) == (tj // c)) & (ti >= tj), 1.0, 0.0).astype(BF16)
    zero_h = jnp.zeros((HEAD, HEAD), F32)

    def seq_total(x):
        if nb == 1:
            return x[c - 1:c, :]
        return jnp.concatenate([jnp.broadcast_to(x[(g + 1) * c - 1:(g + 1) * c, :], (c, LANES))
                                for g in range(nb)], axis=0)

    def stack(x):
        parts = []
        for g in range(nb):
            xg = x[g * c:(g + 1) * c]
            parts.append(jnp.where(lane_head == 0, xg, 0.0))
            parts.append(jnp.where(lane_head == 1, xg, 0.0))
        return jnp.concatenate(parts, axis=0)

    def dup(x):
        parts = []
        for g in range(nb):
            xg = x[g * c:(g + 1) * c]
            parts += [xg, xg]
        return jnp.concatenate(parts, axis=0)

    def fold(x2):
        parts = []
        for g in range(nb):
            parts.append(x2[2 * g * c:(2 * g + 1) * c] + x2[(2 * g + 1) * c:(2 * g + 2) * c])
        return jnp.concatenate(parts, axis=0)

    nseq = s0_ref.shape[0]
    slot = lambda q, j: q * PAIRS + j

    @pl.when(ci == 0)
    def _():
        for q in range(nseq):
            for j in range(PAIRS):
                top = jnp.concatenate([s0_ref[q, 2 * j], zero_h], axis=1)
                bot = jnp.concatenate([zero_h, s0_ref[q, 2 * j + 1]], axis=1)
                s_scr[slot(q, j)] = jnp.concatenate([top, bot], axis=0)

    def load(ref, chain):
        s, j = chain
        return ref[s * nb:(s + 1) * nb, :, j * LANES:(j + 1) * LANES].reshape(WKV_ROWS, LANES)

    def state_dots(x2, chain):
        s, j = chain
        parts = [_dot_nt(x2[2 * g * c:2 * (g + 1) * c], s_scr[slot(s * nb + g, j)]) for g in range(nb)]
        return parts[0] if nb == 1 else jnp.concatenate(parts, axis=0)

    chains = [(s, j) for s in range(nseq // nb) for j in range(PAIRS)]
    for j0 in range(0, len(chains), WKV_GROUP):
        js = chains[j0:j0 + WKV_GROUP]
        each = lambda f, *cols: [f(*args) for args in zip(*cols)]
        r, wl, k, v, a, b = ([load(ref, ch) for ch in js] for ref in (r_ref, wl_ref, k_ref, v_ref, a_ref, b_ref))
        gc = each(lambda x: _split_dot(cum01, x), wl)
        gt = each(seq_total, gc)
        e_out = each(lambda x: jnp.exp(-x), gc)
        e_end = each(lambda t, x: jnp.exp(t - x), gt, gc)
        at2 = each(lambda x, y, z: stack(x * jnp.exp(y - z)), a, gc, wl)
        rt2 = each(lambda x, y: stack(x * jnp.exp(y)), r, gc)
        bt = each(lambda x, e: x * e, b, e_out)
        kt = each(lambda x, e: x * e, k, e_out)
        q = each(lambda a2, r2, x, y: _dot_nt(jnp.concatenate([a2, r2], axis=0),
                                              jnp.concatenate([dup(x), dup(y)], axis=0)),
                 at2, rt2, bt, kt)
        m_ab = each(lambda x: x[:nrows, :nrows] * strict, q)
        m_ak = each(lambda x: x[:nrows, nrows:] * strict, q)
        a_rb = each(lambda x: x[nrows:, :nrows] * incl, q)
        a_rk = each(lambda x: x[nrows:, nrows:] * incl, q)

        p = m_ab
        tinv = each(lambda x: eye + x, m_ab)
        span = 2
        while span < c:
            p = each(lambda x: _dot(x, x), p)
            tinv = each(lambda x, y: x + _dot(x, y), tinv, p)
            span *= 2
        v2 = each(stack, v)
        w2 = each(_dot, tinv, at2)
        mv = each(_dot, m_ak, v2)
        uv = each(_dot, tinv, mv)
        u2 = each(lambda x, j, y: state_dots(x, j) + y, w2, js, uv)
        rs = each(state_dots, rt2, js)
        y2 = each(lambda x, mb, mk, u_, v_: x + _dot(jnp.concatenate([mb, mk], axis=1),
                                                      jnp.concatenate([u_, v_], axis=0)),
                  rs, a_rb, a_rk, u2, v2)
        for (s, j), y2j in zip(js, y2):
            y_ref[s * nb:(s + 1) * nb, :, j * LANES:(j + 1) * LANES] = fold(y2j).reshape(nb, c, LANES)
        u = each(fold, u2)
        bh = each(lambda x, e: x * e, b, e_end)
        kh = each(lambda x, e: x * e, k, e_end)
        for idx, (s, j) in enumerate(js):
            for g in range(nb):
                rows = slice(g * c, (g + 1) * c)
                upd = _dot_tn(jnp.concatenate([u[idx][rows], v[idx][rows]], axis=0),
                              jnp.concatenate([bh[idx][rows], kh[idx][rows]], axis=0))
                decay = jnp.exp(gt[idx][g * c:g * c + 1, :])
                sl = slot(s * nb + g, j)
                s_scr[sl] = s_scr[sl] * decay + upd * blk

    @pl.when(ci == pl.num_programs(1) - 1)
    def _():
        for q in range(nseq):
            for j in range(PAIRS):
                s_fin = s_scr[slot(q, j)]
                so_ref[q, 2 * j] = s_fin[:HEAD, :HEAD]
                so_ref[q, 2 * j + 1] = s_fin[HEAD:, HEAD:]


def _wkv(r, wl, k, v, a, b, s0):
    bsz, t, _ = r.shape
    c = min(t, WKV_ROWS)
    nb = WKV_ROWS // c
    nseq = nb * math.gcd(WKV_SETS, bsz // nb)
    seq_spec = pl.BlockSpec((nseq, c, D_MODEL), lambda i, j: (i, j, 0))
    st_spec = pl.BlockSpec((nseq, N_HEADS, HEAD, HEAD), lambda i, j: (i, 0, 0, 0))
    return pl.pallas_call(
        functools.partial(_wkv_kernel, nb=nb, c=c),
        grid=(bsz // nseq, t // c),
        in_specs=[seq_spec] * 6 + [st_spec],
        out_specs=[seq_spec, st_spec],
        out_shape=[jax.ShapeDtypeStruct((bsz, t, D_MODEL), F32),
                   jax.ShapeDtypeStruct((bsz, N_HEADS, HEAD, HEAD), F32)],
        scratch_shapes=[pltpu.VMEM((nseq * PAIRS, LANES, LANES), F32)],
        compiler_params=_cparams("arbitrary", "arbitrary"),
        name="wkv",
    )(r, wl, k, v, a, b, s0)


def _rwkv_post_kernel(y_ref, r_ref, k_ref, v_ref, g_ref, x_ref, gate_ref, vecs_ref, rk_ref, wo_ref, o_ref):
    y = y_ref[...]
    inv = 1.0 / HEAD
    mu = _seg_sum(y) * inv
    d = y - mu
    var = _seg_sum(d * d) * inv
    yn = d * lax.rsqrt(var + GN_EPS) * vecs_ref[10:11, :] + vecs_ref[11:12, :]
    v = v_ref[...]
    bonus = _seg_sum(r_ref[...] * k_ref[...] * rk_ref[...]) * v
    z = (yn + bonus) * g_ref[...]
    o_ref[...] = x_ref[...] + gate_ref[...] * _dot(z, wo_ref[...])


def _rwkv_post(y, r, k, v, g, x, gate, vecs, rk, wo, li, *, tm):
    n = x.shape[0]
    mspecs, margs = _mod_specs([gate], tm)
    return pl.pallas_call(
        _rwkv_post_kernel,
        grid=(n // tm,),
        in_specs=[_row_spec(tm, D_MODEL)] * 6 + mspecs + [
                  _full_spec(vecs.shape), _full_spec(rk.shape),
                  pl.BlockSpec((None, D_MODEL, D_MODEL), lambda i: (li, 0, 0))],
        out_specs=_row_spec(tm, D_MODEL),
        out_shape=jax.ShapeDtypeStruct((n, D_MODEL), F32),
        compiler_params=_cparams("arbitrary"),
        name="rwkv_post",
    )(y, r, k, v, g, x, *margs, vecs, rk, wo)


def _ffn_kernel(x_ref, sh_ref, sc_ref, gate_ref, ng_ref, wg_ref, wu_ref, wd_ref, o_ref, h_scr, acc_scr):
    cidx = pl.program_id(1)

    @pl.when(cidx == 0)
    def _():
        h_scr[...] = _rms_mod(x_ref[...], ng_ref[...], sh_ref[...], sc_ref[...]).astype(BF16)
        acc_scr[...] = jnp.zeros_like(acc_scr)

    hb = h_scr[...]
    gt = jnp.dot(hb, wg_ref[...], preferred_element_type=F32)
    up = jnp.dot(hb, wu_ref[...], preferred_element_type=F32)
    acc_scr[...] += _dot(gt * _sigmoid(gt) * up, wd_ref[...])

    @pl.when(cidx == pl.num_programs(1) - 1)
    def _():
        o_ref[...] = x_ref[...] + gate_ref[...] * acc_scr[...]


def _ffn(x, sh, sc, gate, ng, wgu, wd, li, *, tm, tf=1408):
    n = x.shape[0]
    mspecs, margs = _mod_specs([sh, sc, gate], tm)
    nc = D_FF // tf
    return pl.pallas_call(
        _ffn_kernel,
        grid=(n // tm, nc),
        in_specs=[_row_spec(tm, D_MODEL)] + mspecs + [
                  _full_spec(ng.shape),
                  pl.BlockSpec((None, D_MODEL, tf), lambda i, j: (li, 0, j)),
                  pl.BlockSpec((None, D_MODEL, tf), lambda i, j: (li, 0, nc + j)),
                  pl.BlockSpec((None, tf, D_MODEL), lambda i, j: (li, j, 0))],
        out_specs=_row_spec(tm, D_MODEL),
        out_shape=jax.ShapeDtypeStruct((n, D_MODEL), F32),
        scratch_shapes=[pltpu.VMEM((tm, D_MODEL), BF16), pltpu.VMEM((tm, D_MODEL), F32)],
        compiler_params=_cparams("arbitrary", "arbitrary"),
        name="ffn",
    )(x, *margs, ng, wgu, wgu, wd)


def _moe_kernel(x_ref, sh_ref, sc_ref, gate_ref, ng_ref, rw_ref, rb_ref, wgu_ref, wd_ref, o_ref,
                h_scr, comb_scr, acc_scr):
    e = pl.program_id(1)
    tm = x_ref.shape[0]
    lane = lax.broadcasted_iota(jnp.int32, (tm, LANES), 1)

    @pl.when(e == 0)
    def _():
        h = _rms_mod(x_ref[...], ng_ref[...], sh_ref[...], sc_ref[...]).astype(BF16)
        h_scr[...] = h
        logits = jnp.dot(h, rw_ref[...], preferred_element_type=F32) + rb_ref[...]
        m1 = jnp.max(logits, axis=1, keepdims=True)
        i1 = jnp.min(jnp.where(logits == m1, lane, LANES), axis=1, keepdims=True)
        rest = jnp.where(lane == i1, NEG * 2, logits)
        m2 = jnp.max(rest, axis=1, keepdims=True)
        i2 = jnp.min(jnp.where(rest == m2, lane, LANES), axis=1, keepdims=True)
        e2 = jnp.exp(m2 - m1)
        den = 1.0 + e2
        comb_scr[...] = jnp.where(lane == i1, 1.0 / den, 0.0) + jnp.where(lane == i2, e2 / den, 0.0)
        acc_scr[...] = jnp.zeros_like(acc_scr)

    gu = jnp.dot(h_scr[...], wgu_ref[...], preferred_element_type=F32)
    gt = gu[:, :EXPERT_FF]
    out_e = _dot(gt * _sigmoid(gt) * gu[:, EXPERT_FF:], wd_ref[...])
    ce = jnp.sum(jnp.where(lane == e, comb_scr[...], 0.0), axis=1, keepdims=True)
    acc_scr[...] += ce * out_e

    @pl.when(e == pl.num_programs(1) - 1)
    def _():
        o_ref[...] = x_ref[...] + gate_ref[...] * acc_scr[...]


def _moe(x, sh, sc, gate, ng, rw, rb, wgu, wd, li, *, tm):
    n = x.shape[0]
    mspecs, margs = _mod_specs([sh, sc, gate], tm)
    return pl.pallas_call(
        _moe_kernel,
        grid=(n // tm, N_EXPERTS),
        in_specs=[_row_spec(tm, D_MODEL)] + mspecs + [
                  _full_spec(ng.shape), _full_spec(rw.shape), _full_spec(rb.shape),
                  pl.BlockSpec((None, None, D_MODEL, 2 * EXPERT_FF), lambda i, e: (li, e, 0, 0)),
                  pl.BlockSpec((None, None, EXPERT_FF, D_MODEL), lambda i, e: (li, e, 0, 0))],
        out_specs=_row_spec(tm, D_MODEL),
        out_shape=jax.ShapeDtypeStruct((n, D_MODEL), F32),
        scratch_shapes=[pltpu.VMEM((tm, D_MODEL), BF16), pltpu.VMEM((tm, LANES), F32),
                        pltpu.VMEM((tm, D_MODEL), F32)],
        compiler_params=_cparams("arbitrary", "arbitrary"),
        name="moe",
    )(x, *margs, ng, rw, rb, wgu, wd)


def _head_norm_rope(t, g, cos, sin_up, sin_dn):
    ms = _seg_sum(t * t) * (1.0 / HEAD)
    return _rope_cols(t * lax.rsqrt(ms + RMS_EPS) * g, cos, sin_up, sin_dn)


def _kv_variants(t):
    rows = t.shape[0]
    half = lax.broadcasted_iota(jnp.int32, (rows, LANES), 1) // HEAD
    out = []
    for g in range(N_KV):
        col = t[:, (g // 2) * LANES:(g // 2 + 1) * LANES]
        swapped = pltpu.roll(col, HEAD, 1)
        for he in range(2):
            src = col if he == g % 2 else swapped
            out.append(jnp.where(half == he, src, 0.0).astype(BF16))
    return jnp.concatenate(out, axis=1)


def _kv_proj_kernel(x_ref, sh_ref, sc_ref, ng_ref, w_ref, kg_ref, cos_ref, su_ref, sd_ref, k_o, v_o, kz_o, vz_o):
    h = _rms_mod(x_ref[...], ng_ref[...], sh_ref[...], sc_ref[...])
    kv = _dot(h, w_ref[...])
    k = _head_norm_rope(kv[:, :KV_DIM], kg_ref[...], cos_ref[...], su_ref[...], sd_ref[...])
    v = kv[:, KV_DIM:]
    k_o[...] = k
    v_o[...] = v
    kz_o[...] = _kv_variants(k)
    vz_o[...] = _kv_variants(v)


def _kv_proj(x, sh, sc, ng, w, kg, rope, *, tm):
    n = x.shape[0]
    mspecs, margs = _mod_specs([sh, sc], tm)
    return pl.pallas_call(
        _kv_proj_kernel,
        grid=(n // tm,),
        in_specs=[_row_spec(tm, D_MODEL)] + mspecs + [_full_spec(ng.shape), _full_spec(w.shape),
                                                      _full_spec(kg.shape)] + _rope_specs(rope, tm),
        out_specs=[_row_spec(tm, KV_DIM)] * 2 + [_row_spec(tm, D_MODEL)] * 2,
        out_shape=[jax.ShapeDtypeStruct((n, KV_DIM), F32)] * 2 + [jax.ShapeDtypeStruct((n, D_MODEL), BF16)] * 2,
        compiler_params=_cparams("arbitrary"),
        name="kv_proj",
    )(x, *margs, ng, w, kg, *rope)


def _q_proj_kernel(x_ref, sh_ref, sc_ref, ng_ref, w_ref, qg_ref, cos_ref, su_ref, sd_ref, q_o):
    h = _rms_mod(x_ref[...], ng_ref[...], sh_ref[...], sc_ref[...])
    q = _head_norm_rope(_dot(h, w_ref[...]), qg_ref[...], cos_ref[...], su_ref[...], sd_ref[...])
    q_o[...] = (q * HEAD ** -0.5).astype(BF16)


def _q_proj(x, sh, sc, ng, w, li, qg, rope, *, tm):
    n = x.shape[0]
    mspecs, margs = _mod_specs([sh, sc], tm)
    return pl.pallas_call(
        _q_proj_kernel,
        grid=(n // tm,),
        in_specs=[_row_spec(tm, D_MODEL)] + mspecs + [
                  _full_spec(ng.shape), pl.BlockSpec((None, D_MODEL, D_MODEL), lambda i: (li, 0, 0)),
                  _full_spec(qg.shape)] + _rope_specs(rope, tm),
        out_specs=_row_spec(tm, D_MODEL),
        out_shape=jax.ShapeDtypeStruct((n, D_MODEL), BF16),
        compiler_params=_cparams("arbitrary"),
        name="q_proj",
    )(x, *margs, ng, w, qg, *rope)


def _o_proj_kernel(a_ref, x_ref, gate_ref, w_ref, o_ref):
    o_ref[...] = x_ref[...] + gate_ref[...] * jnp.dot(a_ref[...], w_ref[...], preferred_element_type=F32)


def _o_proj(a, x, gate, w, li, *, tm):
    n = x.shape[0]
    mspecs, margs = _mod_specs([gate], tm)
    return pl.pallas_call(
        _o_proj_kernel,
        grid=(n // tm,),
        in_specs=[_row_spec(tm, D_MODEL), _row_spec(tm, D_MODEL)] + mspecs + [
                  pl.BlockSpec((None, D_MODEL, D_MODEL), lambda i: (li, 0, 0))],
        out_specs=_row_spec(tm, D_MODEL),
        out_shape=jax.ShapeDtypeStruct((n, D_MODEL), F32),
        compiler_params=_cparams("arbitrary"),
        name="o_proj",
    )(a, x, *margs, w)


def _scores(q, kz):
    out = []
    for g in range(N_KV):
        qs = jnp.concatenate([q[:, (2 * g) * LANES:(2 * g + 1) * LANES],
                              q[:, (2 * g + 1) * LANES:(2 * g + 2) * LANES]], axis=0).astype(BF16)
        for he in range(2):
            col = 2 * g + he
            out.append(lax.dot_general(qs, kz[:, col * LANES:(col + 1) * LANES], (((1,), (1,)), ((), ())),
                                       preferred_element_type=F32))
    return out


def _softmax_pv(problems, valid, sink_ref):
    tq, tk = valid.shape[0] // 2, valid.shape[1]
    ones = jnp.ones((tk, LANES), BF16)
    blocks = [(pi, col) for pi in range(len(problems)) for col in range(2 * N_KV)]

    def sink_rows(col):
        ha = GROUP * (col // 2) + col % 2
        return jnp.concatenate([jnp.broadcast_to(sink_ref[ha:ha + 1, :], (tq, LANES)),
                                jnp.broadcast_to(sink_ref[ha + 2:ha + 3, :], (tq, LANES))], axis=0)

    sinks = [sink_rows(col) for col in range(2 * N_KV)]
    masked = [jnp.where(valid, problems[pi][0][col], NEG) for pi, col in blocks]
    mx = [jnp.maximum(jnp.broadcast_to(jnp.max(s, axis=1, keepdims=True), (2 * tq, LANES)), sinks[col])
          for s, (pi, col) in zip(masked, blocks)]
    probs = [jnp.concatenate([jnp.exp(s[:, c * LANES:(c + 1) * LANES] - m) for c in range(tk // LANES)],
                             axis=1).astype(BF16) for s, m in zip(masked, mx)]
    pv = [jnp.dot(p, jnp.concatenate([problems[pi][1][:, col * LANES:(col + 1) * LANES], ones], axis=1),
                  preferred_element_type=F32) for p, (pi, col) in zip(probs, blocks)]
    outs = []
    for pi in range(len(problems)):
        cols = [None] * PAIRS
        for col in range(2 * N_KV):
            idx = pi * 2 * N_KV + col
            den = pv[idx][:, LANES:] + jnp.exp(sinks[col] - mx[idx])
            o = pv[idx][:, :LANES] * (1.0 / den)
            for part, qc in ((o[:tq], 2 * (col // 2)), (o[tq:], 2 * (col // 2) + 1)):
                cols[qc] = part if cols[qc] is None else cols[qc] + part
        outs.append(jnp.concatenate(cols, axis=1).astype(BF16))
    return outs


def _attn_band_kernel(q_ref, kz_ref, kzp_ref, vz_ref, vzp_ref, sink_ref, o_ref):
    i = pl.program_id(1)
    nblk = q_ref.shape[0] // WINDOW
    r = lax.broadcasted_iota(jnp.int32, (2 * WINDOW, 2 * WINDOW), 0) & (WINDOW - 1)
    c = lax.broadcasted_iota(jnp.int32, (2 * WINDOW, 2 * WINDOW), 1)
    band = (c > r) & (c <= r + WINDOW)
    band0 = band & (c >= jnp.where(i > 0, 0, WINDOW))
    rows = lambda jb: slice(jb * WINDOW, (jb + 1) * WINDOW)

    def keys(ref, prev_ref, jb):
        prev = prev_ref[...] if jb == 0 else ref[rows(jb - 1), :]
        return jnp.concatenate([prev, ref[rows(jb), :]], axis=0)

    nxt = _scores(q_ref[rows(0), :], keys(kz_ref, kzp_ref, 0))
    for jb in range(nblk):
        cur = nxt
        if jb + 1 < nblk:
            nxt = _scores(q_ref[rows(jb + 1), :], keys(kz_ref, kzp_ref, jb + 1))
        o_ref[rows(jb), :] = _softmax_pv([(cur, keys(vz_ref, vzp_ref, jb))], band0 if jb == 0 else band,
                                         sink_ref)[0]


def _attn_band(q, kz, vz, sinks, *, tq):
    bsz, t, _ = q.shape
    tq = min(tq, t)
    per = tq // WINDOW
    cur = pl.BlockSpec((None, tq, D_MODEL), lambda b, i: (b, i, 0))
    prv = pl.BlockSpec((None, WINDOW, D_MODEL), lambda b, i: (b, jnp.maximum(i * per - 1, 0), 0))
    return pl.pallas_call(
        _attn_band_kernel,
        grid=(bsz, t // tq),
        in_specs=[cur, cur, prv, cur, prv, _full_spec(sinks.shape)],
        out_specs=cur,
        out_shape=jax.ShapeDtypeStruct((bsz, t, D_MODEL), BF16),
        compiler_params=_cparams("arbitrary", "arbitrary"),
        name="attn_band",
    )(q, kz, kz, vz, vz, sinks)


def _attn_cache_kernel(q_ref, ck_ref, kn_ref, cv_ref, vn_ref, sink_ref, o_ref):
    nseq, tq, _ = q_ref.shape
    buf = ck_ref.shape[1]
    tk = 2 * buf
    zpad = jnp.zeros((tk - buf - tq, KV_DIM), F32)
    r = lax.broadcasted_iota(jnp.int32, (2 * tq, tk), 0) & (tq - 1)
    c = lax.broadcasted_iota(jnp.int32, (2 * tq, tk), 1)
    valid = ((c < buf) & (c > r)) | ((c >= buf) & (c <= buf + r))

    def operand(cache_ref, new_ref, b):
        return _kv_variants(jnp.concatenate([cache_ref[b], new_ref[b], zpad], axis=0))

    problems = [(_scores(q_ref[b], operand(ck_ref, kn_ref, b)), operand(cv_ref, vn_ref, b)) for b in range(nseq)]
    for b, o in enumerate(_softmax_pv(problems, valid, sink_ref)):
        o_ref[b] = o


def _attn_cache(q, ck, kn, cv, vn, sinks, *, group=8):
    bsz, tq, _ = q.shape
    buf = ck.shape[1]
    group = min(group, bsz)
    spec = lambda rows, w: pl.BlockSpec((group, rows, w), lambda b: (b, 0, 0))
    return pl.pallas_call(
        _attn_cache_kernel,
        grid=(bsz // group,),
        in_specs=[spec(tq, D_MODEL), spec(buf, KV_DIM), spec(tq, KV_DIM), spec(buf, KV_DIM),
                  spec(tq, KV_DIM), _full_spec(sinks.shape)],
        out_specs=spec(tq, D_MODEL),
        out_shape=jax.ShapeDtypeStruct((bsz, tq, D_MODEL), BF16),
        compiler_params=_cparams("arbitrary"),
        name="attn_cache",
    )(q, ck, kn, cv, vn, sinks)


def _rope_tables(pos):
    half = ROPE_DIM // 2
    inv = ROPE_THETA ** (-jnp.arange(half, dtype=F32) * 2.0 / ROPE_DIM)
    ang = pos.astype(F32)[:, None] * inv
    cos, sin = jnp.cos(ang), jnp.sin(ang)
    t = pos.shape[0]
    pad = jnp.zeros((t, HEAD - ROPE_DIM), F32)
    z = jnp.zeros((t, half), F32)
    cos_h = jnp.concatenate([cos, cos, pad + 1.0], axis=1)
    up_h = jnp.concatenate([-sin, z, pad], axis=1)
    dn_h = jnp.concatenate([z, sin, pad], axis=1)
    tile = lambda a: jnp.concatenate([a, a], axis=1)
    return tile(cos_h), tile(up_h), tile(dn_h)


def _prep_weights(p):
    bf = lambda a: a.astype(BF16)
    pad_cols = lambda a: jnp.pad(a, ((0, 0), (0, LANES - a.shape[1])))
    pad_rows = lambda a: jnp.pad(a, ((0, LANES - a.shape[0]), (0, 0)))
    n_a = p['rwkv_w_rkv'].shape[0]
    w = dict(
        ada_w=bf(p['ada_w']), ada_b=p['ada_b'][:, None, :],
        kv_ada_w=bf(p['kv_ada_w'])[None], kv_ada_b=p['kv_ada_b'][None, None, :],
        rwkv_w_rkv=bf(p['rwkv_w_rkv']),
        rwkv_lora=[(bf(pad_cols(p['rwkv_w1'][l])), bf(pad_rows(p['rwkv_w2'][l])),
                    bf(pad_cols(p['rwkv_a1'][l])), bf(pad_rows(p['rwkv_a2'][l])),
                    bf(p['rwkv_g1'][l]), bf(p['rwkv_g2'][l])) for l in range(n_a)],
        rwkv_vecs=[jnp.pad(p['rwkv_vecs'][l], ((0, 4), (0, 0))) for l in range(n_a)],
        rwkv_vres=[None] + [(p['rwkv_v0'][l][None, :], bf(pad_cols(p['rwkv_v1'][l])),
                             bf(pad_rows(p['rwkv_v2'][l]))) for l in range(n_a - 1)],
        rwkv_wo=bf(p['rwkv_w_o']),
        rwkv_rk=[p['rwkv_r_k'][l].reshape(1, D_MODEL) for l in range(n_a)],
        w_kv=bf(p['attn_w_kv']),
        k_g=jnp.tile(p['k_norm_g'], N_KV)[None, :],
        w_q=bf(p['attn_w_q']),
        q_g=[jnp.tile(p['q_norm_g'][j], N_HEADS)[None, :] for j in range(p['attn_w_q'].shape[0])],
        w_o=bf(p['attn_w_o']),
        sinks=[jnp.broadcast_to(p['attn_sinks'][j][:, None], (N_HEADS, LANES)) for j in range(p['attn_w_q'].shape[0])],
        ffn_gu=bf(p['ffn_w_gu']), ffn_d=bf(p['ffn_w_down']),
        moe_rw=[bf(pad_cols(p['moe_router_w'][i])) for i in range(p['moe_router_w'].shape[0])],
        moe_rb=[jnp.pad(p['moe_router_b'][i], (0, LANES - N_EXPERTS), constant_values=NEG)[None, :]
                for i in range(p['moe_router_w'].shape[0])],
        moe_gu=bf(p['moe_w_gu']), moe_d=bf(p['moe_w_down']),
        norm_g=p['norm_g'], kv_norm_g=p['kv_norm_g'][None, :],
    )
    return w


def _trunk(x, ada, kv_ada, pos, states, cache, w, *, tm):
    bsz, t, _ = x.shape
    n = bsz * t
    depth = ada.shape[0] // 6
    n_a = len(w['rwkv_lora'])
    tm = min(tm, n)
    tm_small = min(tm, TM_RWKV)
    tiles = (tm, tm_small)
    ada_t = _mod_tables(ada, t, tiles)
    kv_ada_t = _mod_tables(kv_ada, t, tiles)
    rope = _rope_tables(pos)
    if t < tm:
        rope = tuple(jnp.tile(a, (bsz, 1)) for a in rope)
    tpad = (-t) % 8
    seq = lambda z: z.reshape(bsz, t, z.shape[-1])
    pad8 = lambda z: jnp.pad(seq(z), ((0, 0), (0, tpad), (0, 0))) if tpad else seq(z)
    xf = x.reshape(n, D_MODEL)
    new_states = []
    v_first = None
    kv_out = None
    for l in range(depth):
        sh1, sc1, g1, sh2, sc2, g2 = [(ada_t, 6 * l + i) for i in range(6)]
        ng1 = w['norm_g'][l, 0][None, :]
        ng2 = w['norm_g'][l, 1][None, :]
        if l < n_a:
            s0, hl0 = states[l]
            vres = None if l == 0 else (*w['rwkv_vres'][l], v_first)
            r, wl, k, v, a, b, g, h_last = _rwkv_proj(
                xf, sh1, sc1, ng1, w['rwkv_vecs'][l], (_mod_tables(hl0[None], t, tiles), 0),
                (w['rwkv_w_rkv'], l, w['rwkv_lora'][l]), vres, seq_len=t, tm=tm_small)
            if l == 0:
                v_first = v
            y, s_new = _wkv(pad8(r), pad8(wl), pad8(k), pad8(v), pad8(a), pad8(b), s0)
            y = y[:, :t].reshape(n, D_MODEL)
            xf = _rwkv_post(y, r, k, v, g, xf, g1, w['rwkv_vecs'][l], w['rwkv_rk'][l], w['rwkv_wo'], l,
                            tm=tm_small)
            new_states.append((s_new, h_last))
        else:
            j = l - n_a
            if j == 0:
                k_new, v_new, kz, vz = _kv_proj(xf, (kv_ada_t, 0), (kv_ada_t, 1), w['kv_norm_g'], w['w_kv'],
                                                w['k_g'], rope, tm=tm)
                k_new, v_new = seq(k_new), seq(v_new)
                if cache is None:
                    buf = min(WINDOW, t)
                    kv_out = (k_new[:, t - buf:], v_new[:, t - buf:])
                else:
                    ck, cv = cache
                    buf = ck.shape[1]
                    kv_out = (jnp.concatenate([ck, k_new], axis=1)[:, -buf:],
                              jnp.concatenate([cv, v_new], axis=1)[:, -buf:])
            q = _q_proj(xf, sh1, sc1, ng1, w['w_q'], j, w['q_g'][j], rope, tm=tm)
            if cache is None:
                o = _attn_band(seq(q), seq(kz), seq(vz), w['sinks'][j], tq=tm)
            else:
                o = _attn_cache(pad8(q.astype(F32)), ck, pad8(k_new), cv, pad8(v_new), w['sinks'][j])[:, :t]
            xf = _o_proj(o.reshape(n, D_MODEL), xf, g1, w['w_o'], j, tm=tm)
        if l % 2 == 0:
            xf = _ffn(xf, sh2, sc2, g2, ng2, w['ffn_gu'], w['ffn_d'], l // 2, tm=tm)
        else:
            xf = _moe(xf, sh2, sc2, g2, ng2, w['moe_rw'][l // 2], w['moe_rb'][l // 2], w['moe_gu'], w['moe_d'],
                      l // 2, tm=tm)
    return xf.reshape(bsz, t, D_MODEL), new_states, kv_out


def kernel(x_prompt, x_sample, c_prompt, c_sample, state_wkv_0, state_shift_0, state_wkv_1, state_shift_1, cache_k, cache_v, ada_w, ada_b, norm_g, rwkv_vecs, rwkv_w_rkv, rwkv_w_o, rwkv_w1, rwkv_w2, rwkv_a1, rwkv_a2, rwkv_g1, rwkv_g2, rwkv_r_k, rwkv_v0, rwkv_v1, rwkv_v2, kv_ada_w, kv_ada_b, kv_norm_g, attn_w_kv, k_norm_g, attn_w_q, q_norm_g, attn_w_o, attn_sinks, ffn_w_gu, ffn_w_down, moe_router_w, moe_router_b, moe_w_gu, moe_w_down):
    p = dict(ada_w=ada_w, ada_b=ada_b, norm_g=norm_g, rwkv_vecs=rwkv_vecs, rwkv_w_rkv=rwkv_w_rkv,
             rwkv_w_o=rwkv_w_o, rwkv_w1=rwkv_w1, rwkv_w2=rwkv_w2, rwkv_a1=rwkv_a1, rwkv_a2=rwkv_a2,
             rwkv_g1=rwkv_g1, rwkv_g2=rwkv_g2, rwkv_r_k=rwkv_r_k, rwkv_v0=rwkv_v0, rwkv_v1=rwkv_v1,
             rwkv_v2=rwkv_v2, kv_ada_w=kv_ada_w, kv_ada_b=kv_ada_b, kv_norm_g=kv_norm_g,
             attn_w_kv=attn_w_kv, k_norm_g=k_norm_g, attn_w_q=attn_w_q, q_norm_g=q_norm_g,
             attn_w_o=attn_w_o, attn_sinks=attn_sinks, ffn_w_gu=ffn_w_gu, ffn_w_down=ffn_w_down,
             moe_router_w=moe_router_w, moe_router_b=moe_router_b, moe_w_gu=moe_w_gu,
             moe_w_down=moe_w_down)
    return _forward(x_prompt, x_sample, c_prompt, c_sample, state_wkv_0, state_shift_0, state_wkv_1,
                    state_shift_1, cache_k, cache_v, p)


def _forward(x_prompt, x_sample, c_prompt, c_sample, state_wkv_0, state_shift_0, state_wkv_1, state_shift_1,
             cache_k, cache_v, p):
    w = _prep_weights(p)
    bp, tp, _ = x_prompt.shape
    bs, ts, _ = x_sample.shape
    c_all = jnp.concatenate([c_prompt, c_sample], axis=0)
    c_all = jnp.pad(c_all, ((0, (-c_all.shape[0]) % 8), (0, 0)))
    ada = _ada(c_all, w['ada_w'], w['ada_b'])
    kv_ada = _ada(c_all, w['kv_ada_w'], w['kv_ada_b'])
    n_a = len(w['rwkv_lora'])
    zero_states = [(jnp.zeros((bp, N_HEADS, HEAD, HEAD), F32), jnp.zeros((bp, D_MODEL), F32)) for _ in range(n_a)]
    cache_k2 = cache_k.reshape(bs, cache_k.shape[1], KV_DIM)
    cache_v2 = cache_v.reshape(bs, cache_v.shape[1], KV_DIM)
    y_p, rw_p, (k_p, v_p) = _trunk(x_prompt, ada[:, :bp], kv_ada[:, :bp], jnp.arange(tp), zero_states, None, w,
                                   tm=TM)
    y_s, rw_s, (k_s, v_s) = _trunk(x_sample, ada[:, bp:bp + bs], kv_ada[:, bp:bp + bs], PAST_LEN + jnp.arange(ts),
                                   [(state_wkv_0, state_shift_0), (state_wkv_1, state_shift_1)],
                                   (cache_k2, cache_v2), w, tm=TM)
    heads = lambda z: z.reshape(z.shape[0], z.shape[1], N_KV, HEAD)
    return (y_p, y_s, rw_p[0][0], rw_s[0][0], rw_p[0][1], rw_s[0][1],
            rw_p[1][0], rw_s[1][0], rw_p[1][1], rw_s[1][1], heads(k_p), heads(k_s), heads(v_p), heads(v_s))
```

```python
import functools
import math

import jax
import jax.numpy as jnp
from jax import lax
from jax.experimental import pallas as pl
from jax.experimental.pallas import tpu as pltpu

F32 = jnp.float32
BF16 = jnp.bfloat16

D_MODEL = 1024
HEAD = 64
N_HEADS = D_MODEL // HEAD
N_KV = 4
GROUP = N_HEADS // N_KV
KV_DIM = N_KV * HEAD
WINDOW = 128
ROPE_DIM = HEAD // 4
ROPE_THETA = 500000.0
D_FF = 2816
N_EXPERTS = 8
EXPERT_FF = D_MODEL
RMS_EPS = 1e-6
GN_EPS = 64e-5
NEG = -1e30
LANES = 128
PAIRS = D_MODEL // LANES
WKV_ROWS = 64
WKV_SETS = 2
WKV_GROUP = 16
VMEM_LIMIT = 56 * 1024 * 1024
TM = 512
TM_RWKV = 256
MOE_ROWS = 128
PAST_LEN = 8192


def _cparams(*sem):
    return pltpu.CompilerParams(dimension_semantics=sem, vmem_limit_bytes=VMEM_LIMIT)


def _dot(a, b):
    return jnp.dot(a.astype(BF16), b.astype(BF16), preferred_element_type=F32)


def _dot_nt(a, b):
    return lax.dot_general(a.astype(BF16), b.astype(BF16), (((1,), (1,)), ((), ())),
                           preferred_element_type=F32)


def _dot_tn(a, b):
    return lax.dot_general(a.astype(BF16), b.astype(BF16), (((0,), (0,)), ((), ())),
                           preferred_element_type=F32)


def _split_dot(m01_wide, x):
    parts = m01_wide.shape[1] // x.shape[0]
    pieces = []
    rem = x
    for p in range(parts):
        piece = rem.astype(BF16)
        pieces.append(piece)
        if p + 1 < parts:
            rem = rem - piece.astype(F32)
    return jnp.dot(m01_wide, jnp.concatenate(pieces, axis=0), preferred_element_type=F32)


def _head_ones():
    r = lax.broadcasted_iota(jnp.int32, (LANES, LANES), 0) // HEAD
    c = lax.broadcasted_iota(jnp.int32, (LANES, LANES), 1) // HEAD
    return jnp.where(r == c, 1.0, 0.0).astype(BF16)


def _seg_sum(x):
    rows, width = x.shape
    ncol = width // LANES
    ones = _head_ones()
    stacked = jnp.concatenate([x[:, c * LANES:(c + 1) * LANES] for c in range(ncol)], axis=0)
    s = _split_dot_right(stacked, ones)
    return jnp.concatenate([s[c * rows:(c + 1) * rows] for c in range(ncol)], axis=1)


def _split_dot_right(x, m01):
    hi = x.astype(BF16)
    lo = (x - hi.astype(F32)).astype(BF16)
    return jnp.dot(jnp.concatenate([hi, lo], axis=1), jnp.concatenate([m01, m01], axis=0),
                   preferred_element_type=F32)


def _sigmoid(x):
    return 1.0 / (1.0 + jnp.exp(-x))


def _rms_mod(x, g, sh, sc):
    ms = jnp.mean(x * x, axis=-1, keepdims=True)
    y = x * lax.rsqrt(ms + RMS_EPS) * g
    return y * (1.0 + sc) + sh


def _rope_cols(x, cos, sin_up, sin_dn):
    cols = []
    for c in range(x.shape[1] // LANES):
        xc = x[:, c * LANES:(c + 1) * LANES]
        up = pltpu.roll(xc, LANES - ROPE_DIM // 2, 1)
        dn = pltpu.roll(xc, ROPE_DIM // 2, 1)
        cols.append(xc * cos + up * sin_up + dn * sin_dn)
    return jnp.concatenate(cols, axis=1)


def _full_spec(shape):
    nd = len(shape)
    return pl.BlockSpec(shape, lambda *_: (0,) * nd)


def _row_spec(tm, width):
    return pl.BlockSpec((tm, width), lambda i, *_: (i, 0))


def _mod_tables(vecs, seq_len, tiles):
    nvec, bsz, width = vecs.shape
    out, rep = {}, None
    for tile in set(tiles):
        if seq_len >= tile:
            out[tile] = (vecs[:, :, None, :], seq_len // tile)
        else:
            if rep is None:
                rep = jnp.repeat(vecs, seq_len, axis=1)
            out[tile] = (rep.reshape(nvec, bsz * seq_len // tile, tile, width), 1)
    return out


def _mod_specs(mods, tm):
    specs, args = [], []
    for tables, idx in mods:
        table, per_block = tables[tm]
        specs.append(pl.BlockSpec((None, None) + table.shape[2:],
                                  lambda i, *_, idx=idx, per_block=per_block: (idx, i // per_block, 0, 0)))
        args.append(table)
    return specs, args


def _rope_specs(rope, tm):
    cyc = rope[0].shape[0] // tm
    return [pl.BlockSpec((tm, LANES), lambda i, *_: (i % cyc, 0))] * 3


def _ada_kernel(c_ref, w_ref, b_ref, o_ref):
    c = c_ref[...]
    o_ref[...] = _dot(c * _sigmoid(c), w_ref[...]) + b_ref[...]


def _ada(c, w, b):
    m = c.shape[0]
    nl, _, n = w.shape
    nch = n // D_MODEL
    return pl.pallas_call(
        _ada_kernel,
        grid=(nl, nch),
        in_specs=[pl.BlockSpec((m, D_MODEL), lambda l, j: (0, 0)),
                  pl.BlockSpec((None, D_MODEL, D_MODEL), lambda l, j: (l, 0, j)),
                  pl.BlockSpec((None, 1, D_MODEL), lambda l, j: (l, 0, j))],
        out_specs=pl.BlockSpec((None, m, D_MODEL), lambda l, j: (l * nch + j, 0, 0)),
        out_shape=jax.ShapeDtypeStruct((nl * nch, m, D_MODEL), F32),
        compiler_params=_cparams("arbitrary", "arbitrary"),
        name="ada",
    )(c, w, b)


def _rwkv_proj_kernel(*refs, seq_len, tiles_per_seq, has_vres):
    it = iter(refs)
    x_ref, sh_ref, sc_ref, ng_ref, vecs_ref, hb_ref = (next(it) for _ in range(6))
    wr_ref, wk_ref, wv_ref, w1_ref, w2_ref, a1_ref, a2_ref, g1_ref, g2_ref = (next(it) for _ in range(9))
    if has_vres:
        v0_ref, v1_ref, v2_ref, vf_ref = (next(it) for _ in range(4))
    r_o, wl_o, k_o, v_o, a_o, b_o, g_o, hl_o = (next(it) for _ in range(8))
    carry = next(it)

    i = pl.program_id(0)
    tm = x_ref.shape[0]
    h = _rms_mod(x_ref[...], ng_ref[...], sh_ref[...], sc_ref[...])
    hl_o[...] = h[tm - hl_o.shape[0]:, :]
    row = lax.broadcasted_iota(jnp.int32, h.shape, 0)
    prev = pltpu.roll(h, 1, 0)
    if seq_len >= tm:
        @pl.when(i % tiles_per_seq == 0)
        def _():
            carry[0:1, :] = hb_ref[...]
        prev = jnp.where(row == 0, carry[0:1, :], prev)
        carry[0:1, :] = h[tm - 1:tm, :]
    else:
        prev = jnp.where(row % seq_len == 0, hb_ref[...], prev)
    xx = prev - h

    def mix(j):
        return (h + xx * vecs_ref[j:j + 1, :]).astype(BF16)

    r = _dot(mix(0), wr_ref[...])
    k = _dot(mix(2), wk_ref[...])
    xv = mix(3)
    v = _dot(xv, wv_ref[...])
    wz = vecs_ref[6:7, :] + _dot(jnp.tanh(_dot(mix(1), w1_ref[...])), w2_ref[...])
    wl_o[...] = -math.exp(-0.5) * _sigmoid(wz)
    a = _sigmoid(vecs_ref[7:8, :] + _dot(_dot(mix(4), a1_ref[...]), a2_ref[...]))
    g_o[...] = _dot(_sigmoid(_dot(mix(5), g1_ref[...])), g2_ref[...])
    if has_vres:
        mv = _sigmoid(v0_ref[...] + _dot(_dot(xv, v1_ref[...]), v2_ref[...]))
        v = v + (vf_ref[...] - v) * mv
    kk = k * vecs_ref[8:9, :]
    nrm = jnp.sqrt(_seg_sum(kk * kk))
    kk = kk / jnp.maximum(nrm, 1e-12)
    r_o[...] = r
    k_o[...] = k * (1.0 + (a - 1.0) * vecs_ref[9:10, :])
    v_o[...] = v
    a_o[...] = -kk
    b_o[...] = kk * a


def _rwkv_proj(x, sh, sc, ng, vecs, h_last, w, vres, *, seq_len, tm):
    n = x.shape[0]
    nt = n // tm
    mspecs, margs = _mod_specs([sh, sc, h_last], tm)
    if seq_len >= tm:
        mod_tiles, hl_rows, hl_shape = seq_len // tm, 1, (n // seq_len, 1, D_MODEL)
    else:
        mod_tiles, hl_rows, hl_shape = 1, tm, (nt, tm, D_MODEL)
    w_rkv, li, loras = w
    in_specs = [_row_spec(tm, D_MODEL), mspecs[0], mspecs[1], _full_spec(ng.shape), _full_spec(vecs.shape),
                mspecs[2]]
    in_specs += [pl.BlockSpec((None, None, D_MODEL, D_MODEL), lambda i, which=which: (li, which, 0, 0))
                 for which in range(3)]
    in_specs += [_full_spec(t.shape) for t in loras]
    args = [x, margs[0], margs[1], ng, vecs, margs[2], w_rkv, w_rkv, w_rkv, *loras]
    if vres is not None:
        v0, v1, v2, vf = vres
        in_specs += [_full_spec(v0.shape), _full_spec(v1.shape), _full_spec(v2.shape), _row_spec(tm, D_MODEL)]
        args += [v0, v1, v2, vf]
    tok = jax.ShapeDtypeStruct((n, D_MODEL), F32)
    *outs, hl = pl.pallas_call(
        functools.partial(_rwkv_proj_kernel, seq_len=seq_len, tiles_per_seq=mod_tiles,
                          has_vres=vres is not None),
        grid=(nt,),
        in_specs=in_specs,
        out_specs=[_row_spec(tm, D_MODEL)] * 7
                  + [pl.BlockSpec((None, hl_rows, D_MODEL), lambda i: (i // mod_tiles, 0, 0))],
        out_shape=[tok] * 7 + [jax.ShapeDtypeStruct(hl_shape, F32)],
        scratch_shapes=[pltpu.VMEM((8, D_MODEL), F32)],
        compiler_params=_cparams("arbitrary"),
        name="rwkv_proj",
    )(*args)
    if seq_len >= tm:
        hl = hl[:, 0, :]
    else:
        hl = hl.reshape(n // seq_len, seq_len, D_MODEL)[:, -1]
    return (*outs, hl)


def _wkv_kernel(r_ref, wl_ref, k_ref, v_ref, a_ref, b_ref, s0_ref, y_ref, so_ref, s_scr, *, nb, c):
    ci = pl.program_id(1)
    nrows = 2 * WKV_ROWS
    lane_head = lax.broadcasted_iota(jnp.int32, (c, LANES), 1) // HEAD
    ri = lax.broadcasted_iota(jnp.int32, (nrows, nrows), 0)
    cj = lax.broadcasted_iota(jnp.int32, (nrows, nrows), 1)
    same = (ri // c) == (cj // c)
    strict = jnp.where(same & (ri > cj), 1.0, 0.0)
    incl = jnp.where(same & (ri >= cj), 1.0, 0.0)
    eye = jnp.where(ri == cj, 1.0, 0.0)
    blk = jnp.where((ri // HEAD) == (cj // HEAD), 1.0, 0.0)
    ti = lax.broadcasted_iota(jnp.int32, (WKV_ROWS, 3 * WKV_ROWS), 0)
    tj = lax.broadcasted_iota(jnp.int32, (WKV_ROWS, 3 * WKV_ROWS), 1) & (WKV_ROWS - 1)
    cum01 = jnp.where(((ti // c) == (tj // c)) & (ti >= tj), 1.0, 0.0).astype(BF16)
    zero_h = jnp.zeros((HEAD, HEAD), F32)

    def seq_total(x):
        if nb == 1:
            return x[c - 1:c, :]
        return jnp.concatenate([jnp.broadcast_to(x[(g + 1) * c - 1:(g + 1) * c, :], (c, LANES))
                                for g in range(nb)], axis=0)

    def stack(x):
        parts = []
        for g in range(nb):
            xg = x[g * c:(g + 1) * c]
            parts.append(jnp.where(lane_head == 0, xg, 0.0))
            parts.append(jnp.where(lane_head == 1, xg, 0.0))
        return jnp.concatenate(parts, axis=0)

    def dup(x):
        parts = []
        for g in range(nb):
            xg = x[g * c:(g + 1) * c]
            parts += [xg, xg]
        return jnp.concatenate(parts, axis=0)

    def fold(x2):
        parts = []
        for g in range(nb):
            parts.append(x2[2 * g * c:(2 * g + 1) * c] + x2[(2 * g + 1) * c:(2 * g + 2) * c])
        return jnp.concatenate(parts, axis=0)

    nseq = s0_ref.shape[0]
    slot = lambda q, j: q * PAIRS + j

    @pl.when(ci == 0)
    def _():
        for q in range(nseq):
            for j in range(PAIRS):
                top = jnp.concatenate([s0_ref[q, 2 * j], zero_h], axis=1)
                bot = jnp.concatenate([zero_h, s0_ref[q, 2 * j + 1]], axis=1)
                s_scr[slot(q, j)] = jnp.concatenate([top, bot], axis=0)

    def load(ref, chain):
        s, j = chain
        return ref[s * nb:(s + 1) * nb, :, j * LANES:(j + 1) * LANES].reshape(WKV_ROWS, LANES)

    def state_dots(x2, chain):
        s, j = chain
        parts = [_dot_nt(x2[2 * g * c:2 * (g + 1) * c], s_scr[slot(s * nb + g, j)]) for g in range(nb)]
        return parts[0] if nb == 1 else jnp.concatenate(parts, axis=0)

    chains = [(s, j) for s in range(nseq // nb) for j in range(PAIRS)]
    for j0 in range(0, len(chains), WKV_GROUP):
        js = chains[j0:j0 + WKV_GROUP]
        each = lambda f, *cols: [f(*args) for args in zip(*cols)]
        r, wl, k, v, a, b = ([load(ref, ch) for ch in js] for ref in (r_ref, wl_ref, k_ref, v_ref, a_ref, b_ref))
        gc = each(lambda x: _split_dot(cum01, x), wl)
        gt = each(seq_total, gc)
        e_out = each(lambda x: jnp.exp(-x), gc)
        e_end = each(lambda t, x: jnp.exp(t - x), gt, gc)
        at2 = each(lambda x, y, z: stack(x * jnp.exp(y - z)), a, gc, wl)
        rt2 = each(lambda x, y: stack(x * jnp.exp(y)), r, gc)
        bt = each(lambda x, e: x * e, b, e_out)
        kt = each(lambda x, e: x * e, k, e_out)
        q = each(lambda a2, r2, x, y: _dot_nt(jnp.concatenate([a2, r2], axis=0),
                                              jnp.concatenate([dup(x), dup(y)], axis=0)),
                 at2, rt2, bt, kt)
        m_ab = each(lambda x: x[:nrows, :nrows] * strict, q)
        m_ak = each(lambda x: x[:nrows, nrows:] * strict, q)
        a_rb = each(lambda x: x[nrows:, :nrows] * incl, q)
        a_rk = each(lambda x: x[nrows:, nrows:] * incl, q)

        p = m_ab
        tinv = each(lambda x: eye + x, m_ab)
        span = 2
        while span < c:
            p = each(lambda x: _dot(x, x), p)
            tinv = each(lambda x, y: x + _dot(x, y), tinv, p)
            span *= 2
        v2 = each(stack, v)
        w2 = each(_dot, tinv, at2)
        mv = each(_dot, m_ak, v2)
        uv = each(_dot, tinv, mv)
        u2 = each(lambda x, j, y: state_dots(x, j) + y, w2, js, uv)
        rs = each(state_dots, rt2, js)
        y2 = each(lambda x, mb, mk, u_, v_: x + _dot(jnp.concatenate([mb, mk], axis=1),
                                                      jnp.concatenate([u_, v_], axis=0)),
                  rs, a_rb, a_rk, u2, v2)
        for (s, j), y2j in zip(js, y2):
            y_ref[s * nb:(s + 1) * nb, :, j * LANES:(j + 1) * LANES] = fold(y2j).reshape(nb, c, LANES)
        u = each(fold, u2)
        bh = each(lambda x, e: x * e, b, e_end)
        kh = each(lambda x, e: x * e, k, e_end)
        for idx, (s, j) in enumerate(js):
            for g in range(nb):
                rows = slice(g * c, (g + 1) * c)
                upd = _dot_tn(jnp.concatenate([u[idx][rows], v[idx][rows]], axis=0),
                              jnp.concatenate([bh[idx][rows], kh[idx][rows]], axis=0))
                decay = jnp.exp(gt[idx][g * c:g * c + 1, :])
                sl = slot(s * nb + g, j)
                s_scr[sl] = s_scr[sl] * decay + upd * blk

    @pl.when(ci == pl.num_programs(1) - 1)
    def _():
        for q in range(nseq):
            for j in range(PAIRS):
                s_fin = s_scr[slot(q, j)]
                so_ref[q, 2 * j] = s_fin[:HEAD, :HEAD]
                so_ref[q, 2 * j + 1] = s_fin[HEAD:, HEAD:]


def _wkv(r, wl, k, v, a, b, s0):
    bsz, t, _ = r.shape
    c = min(t, WKV_ROWS)
    nb = WKV_ROWS // c
    nseq = nb * math.gcd(WKV_SETS, bsz // nb)
    seq_spec = pl.BlockSpec((nseq, c, D_MODEL), lambda i, j: (i, j, 0))
    st_spec = pl.BlockSpec((nseq, N_HEADS, HEAD, HEAD), lambda i, j: (i, 0, 0, 0))
    return pl.pallas_call(
        functools.partial(_wkv_kernel, nb=nb, c=c),
        grid=(bsz // nseq, t // c),
        in_specs=[seq_spec] * 6 + [st_spec],
        out_specs=[seq_spec, st_spec],
        out_shape=[jax.ShapeDtypeStruct((bsz, t, D_MODEL), F32),
                   jax.ShapeDtypeStruct((bsz, N_HEADS, HEAD, HEAD), F32)],
        scratch_shapes=[pltpu.VMEM((nseq * PAIRS, LANES, LANES), F32)],
        compiler_params=_cparams("arbitrary", "arbitrary"),
        name="wkv",
    )(r, wl, k, v, a, b, s0)


def _rwkv_post_kernel(y_ref, r_ref, k_ref, v_ref, g_ref, x_ref, gate_ref, vecs_ref, rk_ref, wo_ref, o_ref):
    y = y_ref[...]
    inv = 1.0 / HEAD
    mu = _seg_sum(y) * inv
    d = y - mu
    var = _seg_sum(d * d) * inv
    yn = d * lax.rsqrt(var + GN_EPS) * vecs_ref[10:11, :] + vecs_ref[11:12, :]
    v = v_ref[...]
    bonus = _seg_sum(r_ref[...] * k_ref[...] * rk_ref[...]) * v
    z = (yn + bonus) * g_ref[...]
    o_ref[...] = x_ref[...] + gate_ref[...] * _dot(z, wo_ref[...])


def _rwkv_post(y, r, k, v, g, x, gate, vecs, rk, wo, li, *, tm):
    n = x.shape[0]
    mspecs, margs = _mod_specs([gate], tm)
    return pl.pallas_call(
        _rwkv_post_kernel,
        grid=(n // tm,),
        in_specs=[_row_spec(tm, D_MODEL)] * 6 + mspecs + [
                  _full_spec(vecs.shape), _full_spec(rk.shape),
                  pl.BlockSpec((None, D_MODEL, D_MODEL), lambda i: (li, 0, 0))],
        out_specs=_row_spec(tm, D_MODEL),
        out_shape=jax.ShapeDtypeStruct((n, D_MODEL), F32),
        compiler_params=_cparams("arbitrary"),
        name="rwkv_post",
    )(y, r, k, v, g, x, *margs, vecs, rk, wo)


def _ffn_kernel(x_ref, sh_ref, sc_ref, gate_ref, ng_ref, wg_ref, wu_ref, wd_ref, o_ref, h_scr, acc_scr):
    cidx = pl.program_id(1)

    @pl.when(cidx == 0)
    def _():
        h_scr[...] = _rms_mod(x_ref[...], ng_ref[...], sh_ref[...], sc_ref[...]).astype(BF16)
        acc_scr[...] = jnp.zeros_like(acc_scr)

    hb = h_scr[...]
    gt = jnp.dot(hb, wg_ref[...], preferred_element_type=F32)
    up = jnp.dot(hb, wu_ref[...], preferred_element_type=F32)
    acc_scr[...] += _dot(gt * _sigmoid(gt) * up, wd_ref[...])

    @pl.when(cidx == pl.num_programs(1) - 1)
    def _():
        o_ref[...] = x_ref[...] + gate_ref[...] * acc_scr[...]


def _ffn(x, sh, sc, gate, ng, wgu, wd, li, *, tm, tf=1408):
    n = x.shape[0]
    mspecs, margs = _mod_specs([sh, sc, gate], tm)
    nc = D_FF // tf
    return pl.pallas_call(
        _ffn_kernel,
        grid=(n // tm, nc),
        in_specs=[_row_spec(tm, D_MODEL)] + mspecs + [
                  _full_spec(ng.shape),
                  pl.BlockSpec((None, D_MODEL, tf), lambda i, j: (li, 0, j)),
                  pl.BlockSpec((None, D_MODEL, tf), lambda i, j: (li, 0, nc + j)),
                  pl.BlockSpec((None, tf, D_MODEL), lambda i, j: (li, j, 0))],
        out_specs=_row_spec(tm, D_MODEL),
        out_shape=jax.ShapeDtypeStruct((n, D_MODEL), F32),
        scratch_shapes=[pltpu.VMEM((tm, D_MODEL), BF16), pltpu.VMEM((tm, D_MODEL), F32)],
        compiler_params=_cparams("arbitrary", "arbitrary"),
        name="ffn",
    )(x, *margs, ng, wgu, wgu, wd)


def _moe_kernel(x_ref, sh_ref, sc_ref, gate_ref, ng_ref, rw_ref, rb_ref, wgu_ref, wd_ref, o_ref,
                h_scr, comb_scr, acc_scr, before_scr):
    e = pl.program_id(1)
    tm = x_ref.shape[0]
    lane = lax.broadcasted_iota(jnp.int32, (tm, LANES), 1)

    @pl.when((pl.program_id(0) == 0) & (e == 0))
    def _():
        ti = lax.broadcasted_iota(jnp.int32, (tm, tm), 0)
        tj = lax.broadcasted_iota(jnp.int32, (tm, tm), 1)
        before_scr[...] = jnp.where(tj < ti, 1.0, 0.0).astype(BF16)

    @pl.when(e == 0)
    def _():
        h = _rms_mod(x_ref[...], ng_ref[...], sh_ref[...], sc_ref[...]).astype(BF16)
        h_scr[...] = h
        logits = jnp.dot(h, rw_ref[...], preferred_element_type=F32) + rb_ref[...]
        m1 = jnp.max(logits, axis=1, keepdims=True)
        i1 = jnp.min(jnp.where(logits == m1, lane, LANES), axis=1, keepdims=True)
        rest = jnp.where(lane == i1, NEG * 2, logits)
        m2 = jnp.max(rest, axis=1, keepdims=True)
        i2 = jnp.min(jnp.where(rest == m2, lane, LANES), axis=1, keepdims=True)
        e2 = jnp.exp(m2 - m1)
        den = 1.0 + e2
        comb_scr[...] = jnp.where(lane == i1, 1.0 / den, 0.0) + jnp.where(lane == i2, e2 / den, 0.0)
        acc_scr[...] = jnp.zeros_like(acc_scr)

    ce = jnp.sum(jnp.where(lane == e, comb_scr[...], 0.0), axis=1, keepdims=True)
    picked = jnp.where(jnp.broadcast_to(ce, (tm, LANES)) > 0.0, 1.0, 0.0)
    rank = jnp.dot(before_scr[...], picked.astype(BF16), preferred_element_type=F32)
    count = jnp.sum(picked[:, 0:1]).astype(jnp.int32)
    lane_f = lane.astype(F32)

    def block(bi, carry):
        slot = rank - (bi * MOE_ROWS).astype(F32)
        onehot = jnp.concatenate(
            [jnp.where((picked > 0.0) & (slot == lane_f + float(c * LANES)), 1.0, 0.0).astype(BF16)
             for c in range(MOE_ROWS // LANES)], axis=1)
        he = lax.dot_general(onehot, h_scr[...], (((0,), (0,)), ((), ())), preferred_element_type=F32)
        gu = jnp.dot(he.astype(BF16), wgu_ref[...], preferred_element_type=F32)
        gt = gu[:, :EXPERT_FF]
        ye = _dot(gt * _sigmoid(gt) * gu[:, EXPERT_FF:], wd_ref[...])
        acc_scr[...] += ce * jnp.dot(onehot, ye.astype(BF16), preferred_element_type=F32)
        return carry

    nblocks = lax.shift_right_logical(count + (MOE_ROWS - 1), MOE_ROWS.bit_length() - 1)
    lax.fori_loop(0, nblocks, block, 0)

    @pl.when(e == pl.num_programs(1) - 1)
    def _():
        o_ref[...] = x_ref[...] + gate_ref[...] * acc_scr[...]


def _moe(x, sh, sc, gate, ng, rw, rb, wgu, wd, li, *, tm):
    n = x.shape[0]
    mspecs, margs = _mod_specs([sh, sc, gate], tm)
    return pl.pallas_call(
        _moe_kernel,
        grid=(n // tm, N_EXPERTS),
        in_specs=[_row_spec(tm, D_MODEL)] + mspecs + [
                  _full_spec(ng.shape), _full_spec(rw.shape), _full_spec(rb.shape),
                  pl.BlockSpec((None, None, D_MODEL, 2 * EXPERT_FF), lambda i, e: (li, e, 0, 0)),
                  pl.BlockSpec((None, None, EXPERT_FF, D_MODEL), lambda i, e: (li, e, 0, 0))],
        out_specs=_row_spec(tm, D_MODEL),
        out_shape=jax.ShapeDtypeStruct((n, D_MODEL), F32),
        scratch_shapes=[pltpu.VMEM((tm, D_MODEL), BF16), pltpu.VMEM((tm, LANES), F32),
                        pltpu.VMEM((tm, D_MODEL), F32), pltpu.VMEM((tm, tm), BF16)],
        compiler_params=_cparams("arbitrary", "arbitrary"),
        name="moe",
    )(x, *margs, ng, rw, rb, wgu, wd)


def _head_norm_rope(t, g, cos, sin_up, sin_dn):
    ms = _seg_sum(t * t) * (1.0 / HEAD)
    return _rope_cols(t * lax.rsqrt(ms + RMS_EPS) * g, cos, sin_up, sin_dn)


def _kv_variants(t):
    rows = t.shape[0]
    half = lax.broadcasted_iota(jnp.int32, (rows, LANES), 1) // HEAD
    out = []
    for g in range(N_KV):
        col = t[:, (g // 2) * LANES:(g // 2 + 1) * LANES]
        swapped = pltpu.roll(col, HEAD, 1)
        for he in range(2):
            src = col if he == g % 2 else swapped
            out.append(jnp.where(half == he, src, 0.0).astype(BF16))
    return jnp.concatenate(out, axis=1)


def _kv_proj_kernel(x_ref, sh_ref, sc_ref, ng_ref, w_ref, kg_ref, cos_ref, su_ref, sd_ref, k_o, v_o, kz_o, vz_o):
    h = _rms_mod(x_ref[...], ng_ref[...], sh_ref[...], sc_ref[...])
    kv = _dot(h, w_ref[...])
    k = _head_norm_rope(kv[:, :KV_DIM], kg_ref[...], cos_ref[...], su_ref[...], sd_ref[...])
    v = kv[:, KV_DIM:]
    k_o[...] = k
    v_o[...] = v
    kz_o[...] = _kv_variants(k)
    vz_o[...] = _kv_variants(v)


def _kv_proj(x, sh, sc, ng, w, kg, rope, *, tm):
    n = x.shape[0]
    mspecs, margs = _mod_specs([sh, sc], tm)
    return pl.pallas_call(
        _kv_proj_kernel,
        grid=(n // tm,),
        in_specs=[_row_spec(tm, D_MODEL)] + mspecs + [_full_spec(ng.shape), _full_spec(w.shape),
                                                      _full_spec(kg.shape)] + _rope_specs(rope, tm),
        out_specs=[_row_spec(tm, KV_DIM)] * 2 + [_row_spec(tm, D_MODEL)] * 2,
        out_shape=[jax.ShapeDtypeStruct((n, KV_DIM), F32)] * 2 + [jax.ShapeDtypeStruct((n, D_MODEL), BF16)] * 2,
        compiler_params=_cparams("arbitrary"),
        name="kv_proj",
    )(x, *margs, ng, w, kg, *rope)


def _q_proj_kernel(x_ref, sh_ref, sc_ref, ng_ref, w_ref, qg_ref, cos_ref, su_ref, sd_ref, q_o):
    h = _rms_mod(x_ref[...], ng_ref[...], sh_ref[...], sc_ref[...])
    q = _head_norm_rope(_dot(h, w_ref[...]), qg_ref[...], cos_ref[...], su_ref[...], sd_ref[...])
    q_o[...] = (q * HEAD ** -0.5).astype(BF16)


def _q_proj(x, sh, sc, ng, w, li, qg, rope, *, tm):
    n = x.shape[0]
    mspecs, margs = _mod_specs([sh, sc], tm)
    return pl.pallas_call(
        _q_proj_kernel,
        grid=(n // tm,),
        in_specs=[_row_spec(tm, D_MODEL)] + mspecs + [
                  _full_spec(ng.shape), pl.BlockSpec((None, D_MODEL, D_MODEL), lambda i: (li, 0, 0)),
                  _full_spec(qg.shape)] + _rope_specs(rope, tm),
        out_specs=_row_spec(tm, D_MODEL),
        out_shape=jax.ShapeDtypeStruct((n, D_MODEL), BF16),
        compiler_params=_cparams("arbitrary"),
        name="q_proj",
    )(x, *margs, ng, w, qg, *rope)


def _o_proj_kernel(a_ref, x_ref, gate_ref, w_ref, o_ref):
    o_ref[...] = x_ref[...] + gate_ref[...] * jnp.dot(a_ref[...], w_ref[...], preferred_element_type=F32)


def _o_proj(a, x, gate, w, li, *, tm):
    n = x.shape[0]
    mspecs, margs = _mod_specs([gate], tm)
    return pl.pallas_call(
        _o_proj_kernel,
        grid=(n // tm,),
        in_specs=[_row_spec(tm, D_MODEL), _row_spec(tm, D_MODEL)] + mspecs + [
                  pl.BlockSpec((None, D_MODEL, D_MODEL), lambda i: (li, 0, 0))],
        out_specs=_row_spec(tm, D_MODEL),
        out_shape=jax.ShapeDtypeStruct((n, D_MODEL), F32),
        compiler_params=_cparams("arbitrary"),
        name="o_proj",
    )(a, x, *margs, w)


def _scores(q, kz):
    out = []
    for g in range(N_KV):
        qs = jnp.concatenate([q[:, (2 * g) * LANES:(2 * g + 1) * LANES],
                              q[:, (2 * g + 1) * LANES:(2 * g + 2) * LANES]], axis=0).astype(BF16)
        for he in range(2):
            col = 2 * g + he
            out.append(lax.dot_general(qs, kz[:, col * LANES:(col + 1) * LANES], (((1,), (1,)), ((), ())),
                                       preferred_element_type=F32))
    return out


def _softmax_pv(problems, valid, sink_ref):
    tq, tk = valid.shape[0] // 2, valid.shape[1]
    ones = jnp.ones((tk, LANES), BF16)
    blocks = [(pi, col) for pi in range(len(problems)) for col in range(2 * N_KV)]

    def sink_rows(col):
        ha = GROUP * (col // 2) + col % 2
        return jnp.concatenate([jnp.broadcast_to(sink_ref[ha:ha + 1, :], (tq, LANES)),
                                jnp.broadcast_to(sink_ref[ha + 2:ha + 3, :], (tq, LANES))], axis=0)

    sinks = [sink_rows(col) for col in range(2 * N_KV)]
    masked = [jnp.where(valid, problems[pi][0][col], NEG) for pi, col in blocks]
    mx = [jnp.maximum(jnp.broadcast_to(jnp.max(s, axis=1, keepdims=True), (2 * tq, LANES)), sinks[col])
          for s, (pi, col) in zip(masked, blocks)]
    probs = [jnp.concatenate([jnp.exp(s[:, c * LANES:(c + 1) * LANES] - m) for c in range(tk // LANES)],
                             axis=1).astype(BF16) for s, m in zip(masked, mx)]
    pv = [jnp.dot(p, jnp.concatenate([problems[pi][1][:, col * LANES:(col + 1) * LANES], ones], axis=1),
                  preferred_element_type=F32) for p, (pi, col) in zip(probs, blocks)]
    outs = []
    for pi in range(len(problems)):
        cols = [None] * PAIRS
        for col in range(2 * N_KV):
            idx = pi * 2 * N_KV + col
            den = pv[idx][:, LANES:] + jnp.exp(sinks[col] - mx[idx])
            o = pv[idx][:, :LANES] * (1.0 / den)
            for part, qc in ((o[:tq], 2 * (col // 2)), (o[tq:], 2 * (col // 2) + 1)):
                cols[qc] = part if cols[qc] is None else cols[qc] + part
        outs.append(jnp.concatenate(cols, axis=1).astype(BF16))
    return outs


def _attn_band_kernel(q_ref, kz_ref, kzp_ref, vz_ref, vzp_ref, sink_ref, o_ref):
    i = pl.program_id(1)
    nblk = q_ref.shape[0] // WINDOW
    r = lax.broadcasted_iota(jnp.int32, (2 * WINDOW, 2 * WINDOW), 0) & (WINDOW - 1)
    c = lax.broadcasted_iota(jnp.int32, (2 * WINDOW, 2 * WINDOW), 1)
    band = (c > r) & (c <= r + WINDOW)
    band0 = band & (c >= jnp.where(i > 0, 0, WINDOW))
    rows = lambda jb: slice(jb * WINDOW, (jb + 1) * WINDOW)

    def keys(ref, prev_ref, jb):
        prev = prev_ref[...] if jb == 0 else ref[rows(jb - 1), :]
        return jnp.concatenate([prev, ref[rows(jb), :]], axis=0)

    nxt = _scores(q_ref[rows(0), :], keys(kz_ref, kzp_ref, 0))
    for jb in range(nblk):
        cur = nxt
        if jb + 1 < nblk:
            nxt = _scores(q_ref[rows(jb + 1), :], keys(kz_ref, kzp_ref, jb + 1))
        o_ref[rows(jb), :] = _softmax_pv([(cur, keys(vz_ref, vzp_ref, jb))], band0 if jb == 0 else band,
                                         sink_ref)[0]


def _attn_band(q, kz, vz, sinks, *, tq):
    bsz, t, _ = q.shape
    tq = min(tq, t)
    per = tq // WINDOW
    cur = pl.BlockSpec((None, tq, D_MODEL), lambda b, i: (b, i, 0))
    prv = pl.BlockSpec((None, WINDOW, D_MODEL), lambda b, i: (b, jnp.maximum(i * per - 1, 0), 0))
    return pl.pallas_call(
        _attn_band_kernel,
        grid=(bsz, t // tq),
        in_specs=[cur, cur, prv, cur, prv, _full_spec(sinks.shape)],
        out_specs=cur,
        out_shape=jax.ShapeDtypeStruct((bsz, t, D_MODEL), BF16),
        compiler_params=_cparams("arbitrary", "arbitrary"),
        name="attn_band",
    )(q, kz, kz, vz, vz, sinks)


def _attn_cache_kernel(q_ref, ck_ref, kn_ref, cv_ref, vn_ref, sink_ref, o_ref):
    nseq, tq, _ = q_ref.shape
    buf = ck_ref.shape[1]
    tk = 2 * buf
    zpad = jnp.zeros((tk - buf - tq, KV_DIM), F32)
    r = lax.broadcasted_iota(jnp.int32, (2 * tq, tk), 0) & (tq - 1)
    c = lax.broadcasted_iota(jnp.int32, (2 * tq, tk), 1)
    valid = ((c < buf) & (c > r)) | ((c >= buf) & (c <= buf + r))

    def operand(cache_ref, new_ref, b):
        return _kv_variants(jnp.concatenate([cache_ref[b], new_ref[b], zpad], axis=0))

    problems = [(_scores(q_ref[b], operand(ck_ref, kn_ref, b)), operand(cv_ref, vn_ref, b)) for b in range(nseq)]
    for b, o in enumerate(_softmax_pv(problems, valid, sink_ref)):
        o_ref[b] = o


def _attn_cache(q, ck, kn, cv, vn, sinks, *, group=8):
    bsz, tq, _ = q.shape
    buf = ck.shape[1]
    group = min(group, bsz)
    spec = lambda rows, w: pl.BlockSpec((group, rows, w), lambda b: (b, 0, 0))
    return pl.pallas_call(
        _attn_cache_kernel,
        grid=(bsz // group,),
        in_specs=[spec(tq, D_MODEL), spec(buf, KV_DIM), spec(tq, KV_DIM), spec(buf, KV_DIM),
                  spec(tq, KV_DIM), _full_spec(sinks.shape)],
        out_specs=spec(tq, D_MODEL),
        out_shape=jax.ShapeDtypeStruct((bsz, tq, D_MODEL), BF16),
        compiler_params=_cparams("arbitrary"),
        name="attn_cache",
    )(q, ck, kn, cv, vn, sinks)


def _rope_tables(pos):
    half = ROPE_DIM // 2
    inv = ROPE_THETA ** (-jnp.arange(half, dtype=F32) * 2.0 / ROPE_DIM)
    ang = pos.astype(F32)[:, None] * inv
    cos, sin = jnp.cos(ang), jnp.sin(ang)
    t = pos.shape[0]
    pad = jnp.zeros((t, HEAD - ROPE_DIM), F32)
    z = jnp.zeros((t, half), F32)
    cos_h = jnp.concatenate([cos, cos, pad + 1.0], axis=1)
    up_h = jnp.concatenate([-sin, z, pad], axis=1)
    dn_h = jnp.concatenate([z, sin, pad], axis=1)
    tile = lambda a: jnp.concatenate([a, a], axis=1)
    return tile(cos_h), tile(up_h), tile(dn_h)


def _prep_weights(p):
    bf = lambda a: a.astype(BF16)
    pad_cols = lambda a: jnp.pad(a, ((0, 0), (0, LANES - a.shape[1])))
    pad_rows = lambda a: jnp.pad(a, ((0, LANES - a.shape[0]), (0, 0)))
    n_a = p['rwkv_w_rkv'].shape[0]
    w = dict(
        ada_w=bf(p['ada_w']), ada_b=p['ada_b'][:, None, :],
        kv_ada_w=bf(p['kv_ada_w'])[None], kv_ada_b=p['kv_ada_b'][None, None, :],
        rwkv_w_rkv=bf(p['rwkv_w_rkv']),
        rwkv_lora=[(bf(pad_cols(p['rwkv_w1'][l])), bf(pad_rows(p['rwkv_w2'][l])),
                    bf(pad_cols(p['rwkv_a1'][l])), bf(pad_rows(p['rwkv_a2'][l])),
                    bf(p['rwkv_g1'][l]), bf(p['rwkv_g2'][l])) for l in range(n_a)],
        rwkv_vecs=[jnp.pad(p['rwkv_vecs'][l], ((0, 4), (0, 0))) for l in range(n_a)],
        rwkv_vres=[None] + [(p['rwkv_v0'][l][None, :], bf(pad_cols(p['rwkv_v1'][l])),
                             bf(pad_rows(p['rwkv_v2'][l]))) for l in range(n_a - 1)],
        rwkv_wo=bf(p['rwkv_w_o']),
        rwkv_rk=[p['rwkv_r_k'][l].reshape(1, D_MODEL) for l in range(n_a)],
        w_kv=bf(p['attn_w_kv']),
        k_g=jnp.tile(p['k_norm_g'], N_KV)[None, :],
        w_q=bf(p['attn_w_q']),
        q_g=[jnp.tile(p['q_norm_g'][j], N_HEADS)[None, :] for j in range(p['attn_w_q'].shape[0])],
        w_o=bf(p['attn_w_o']),
        sinks=[jnp.broadcast_to(p['attn_sinks'][j][:, None], (N_HEADS, LANES)) for j in range(p['attn_w_q'].shape[0])],
        ffn_gu=bf(p['ffn_w_gu']), ffn_d=bf(p['ffn_w_down']),
        moe_rw=[bf(pad_cols(p['moe_router_w'][i])) for i in range(p['moe_router_w'].shape[0])],
        moe_rb=[jnp.pad(p['moe_router_b'][i], (0, LANES - N_EXPERTS), constant_values=NEG)[None, :]
                for i in range(p['moe_router_w'].shape[0])],
        moe_gu=bf(p['moe_w_gu']), moe_d=bf(p['moe_w_down']),
        norm_g=p['norm_g'], kv_norm_g=p['kv_norm_g'][None, :],
    )
    return w


def _trunk(x, ada, kv_ada, pos, states, cache, w, *, tm):
    bsz, t, _ = x.shape
    n = bsz * t
    depth = ada.shape[0] // 6
    n_a = len(w['rwkv_lora'])
    tm = min(tm, n)
    tm_small = min(tm, TM_RWKV)
    tiles = (tm, tm_small)
    ada_t = _mod_tables(ada, t, tiles)
    kv_ada_t = _mod_tables(kv_ada, t, tiles)
    rope = _rope_tables(pos)
    if t < tm:
        rope = tuple(jnp.tile(a, (bsz, 1)) for a in rope)
    tpad = (-t) % 8
    seq = lambda z: z.reshape(bsz, t, z.shape[-1])
    pad8 = lambda z: jnp.pad(seq(z), ((0, 0), (0, tpad), (0, 0))) if tpad else seq(z)
    xf = x.reshape(n, D_MODEL)
    new_states = []
    v_first = None
    kv_out = None
    for l in range(depth):
        sh1, sc1, g1, sh2, sc2, g2 = [(ada_t, 6 * l + i) for i in range(6)]
        ng1 = w['norm_g'][l, 0][None, :]
        ng2 = w['norm_g'][l, 1][None, :]
        if l < n_a:
            s0, hl0 = states[l]
            vres = None if l == 0 else (*w['rwkv_vres'][l], v_first)
            r, wl, k, v, a, b, g, h_last = _rwkv_proj(
                xf, sh1, sc1, ng1, w['rwkv_vecs'][l], (_mod_tables(hl0[None], t, tiles), 0),
                (w['rwkv_w_rkv'], l, w['rwkv_lora'][l]), vres, seq_len=t, tm=tm_small)
            if l == 0:
                v_first = v
            y, s_new = _wkv(pad8(r), pad8(wl), pad8(k), pad8(v), pad8(a), pad8(b), s0)
            y = y[:, :t].reshape(n, D_MODEL)
            xf = _rwkv_post(y, r, k, v, g, xf, g1, w['rwkv_vecs'][l], w['rwkv_rk'][l], w['rwkv_wo'], l,
                            tm=tm_small)
            new_states.append((s_new, h_last))
        else:
            j = l - n_a
            if j == 0:
                k_new, v_new, kz, vz = _kv_proj(xf, (kv_ada_t, 0), (kv_ada_t, 1), w['kv_norm_g'], w['w_kv'],
                                                w['k_g'], rope, tm=tm)
                k_new, v_new = seq(k_new), seq(v_new)
                if cache is None:
                    buf = min(WINDOW, t)
                    kv_out = (k_new[:, t - buf:], v_new[:, t - buf:])
                else:
                    ck, cv = cache
                    buf = ck.shape[1]
                    kv_out = (jnp.concatenate([ck, k_new], axis=1)[:, -buf:],
                              jnp.concatenate([cv, v_new], axis=1)[:, -buf:])
            q = _q_proj(xf, sh1, sc1, ng1, w['w_q'], j, w['q_g'][j], rope, tm=tm)
            if cache is None:
                o = _attn_band(seq(q), seq(kz), seq(vz), w['sinks'][j], tq=tm)
            else:
                o = _attn_cache(pad8(q.astype(F32)), ck, pad8(k_new), cv, pad8(v_new), w['sinks'][j])[:, :t]
            xf = _o_proj(o.reshape(n, D_MODEL), xf, g1, w['w_o'], j, tm=tm)
        if l % 2 == 0:
            xf = _ffn(xf, sh2, sc2, g2, ng2, w['ffn_gu'], w['ffn_d'], l // 2, tm=tm)
        else:
            xf = _moe(xf, sh2, sc2, g2, ng2, w['moe_rw'][l // 2], w['moe_rb'][l // 2], w['moe_gu'], w['moe_d'],
                      l // 2, tm=tm)
    return xf.reshape(bsz, t, D_MODEL), new_states, kv_out


def kernel(x_prompt, x_sample, c_prompt, c_sample, state_wkv_0, state_shift_0, state_wkv_1, state_shift_1, cache_k, cache_v, ada_w, ada_b, norm_g, rwkv_vecs, rwkv_w_rkv, rwkv_w_o, rwkv_w1, rwkv_w2, rwkv_a1, rwkv_a2, rwkv_g1, rwkv_g2, rwkv_r_k, rwkv_v0, rwkv_v1, rwkv_v2, kv_ada_w, kv_ada_b, kv_norm_g, attn_w_kv, k_norm_g, attn_w_q, q_norm_g, attn_w_o, attn_sinks, ffn_w_gu, ffn_w_down, moe_router_w, moe_router_b, moe_w_gu, moe_w_down):
    p = dict(ada_w=ada_w, ada_b=ada_b, norm_g=norm_g, rwkv_vecs=rwkv_vecs, rwkv_w_rkv=rwkv_w_rkv,
             rwkv_w_o=rwkv_w_o, rwkv_w1=rwkv_w1, rwkv_w2=rwkv_w2, rwkv_a1=rwkv_a1, rwkv_a2=rwkv_a2,
             rwkv_g1=rwkv_g1, rwkv_g2=rwkv_g2, rwkv_r_k=rwkv_r_k, rwkv_v0=rwkv_v0, rwkv_v1=rwkv_v1,
             rwkv_v2=rwkv_v2, kv_ada_w=kv_ada_w, kv_ada_b=kv_ada_b, kv_norm_g=kv_norm_g,
             attn_w_kv=attn_w_kv, k_norm_g=k_norm_g, attn_w_q=attn_w_q, q_norm_g=q_norm_g,
             attn_w_o=attn_w_o, attn_sinks=attn_sinks, ffn_w_gu=ffn_w_gu, ffn_w_down=ffn_w_down,
             moe_router_w=moe_router_w, moe_router_b=moe_router_b, moe_w_gu=moe_w_gu,
             moe_w_down=moe_w_down)
    return _forward(x_prompt, x_sample, c_prompt, c_sample, state_wkv_0, state_shift_0, state_wkv_1,
                    state_shift_1, cache_k, cache_v, p)


def _forward(x_prompt, x_sample, c_prompt, c_sample, state_wkv_0, state_shift_0, state_wkv_1, state_shift_1,
             cache_k, cache_v, p):
    w = _prep_weights(p)
    bp, tp, _ = x_prompt.shape
    bs, ts, _ = x_sample.shape
    c_all = jnp.concatenate([c_prompt, c_sample], axis=0)
    c_all = jnp.pad(c_all, ((0, (-c_all.shape[0]) % 8), (0, 0)))
    ada = _ada(c_all, w['ada_w'], w['ada_b'])
    kv_ada = _ada(c_all, w['kv_ada_w'], w['kv_ada_b'])
    n_a = len(w['rwkv_lora'])
    zero_states = [(jnp.zeros((bp, N_HEADS, HEAD, HEAD), F32), jnp.zeros((bp, D_MODEL), F32)) for _ in range(n_a)]
    cache_k2 = cache_k.reshape(bs, cache_k.shape[1], KV_DIM)
    cache_v2 = cache_v.reshape(bs, cache_v.shape[1], KV_DIM)
    y_p, rw_p, (k_p, v_p) = _trunk(x_prompt, ada[:, :bp], kv_ada[:, :bp], jnp.arange(tp), zero_states, None, w,
                                   tm=TM)
    y_s, rw_s, (k_s, v_s) = _trunk(x_sample, ada[:, bp:bp + bs], kv_ada[:, bp:bp + bs], PAST_LEN + jnp.arange(ts),
                                   [(state_wkv_0, state_shift_0), (state_wkv_1, state_shift_1)],
                                   (cache_k2, cache_v2), w, tm=TM)
    heads = lambda z: z.reshape(z.shape[0], z.shape[1], N_KV, HEAD)
    return (y_p, y_s, rw_p[0][0], rw_s[0][0], rw_p[0][1], rw_s[0][1],
            rw_p[1][0], rw_s[1][0], rw_p[1][1], rw_s[1][1], heads(k_p), heads(k_s), heads(v_p), heads(v_s))
```

```python
import functools
import math

import jax
import jax.numpy as jnp
from jax import lax
from jax.experimental import pallas as pl
from jax.experimental.pallas import tpu as pltpu

F32 = jnp.float32
BF16 = jnp.bfloat16

D_MODEL = 1024
HEAD = 64
N_HEADS = D_MODEL // HEAD
N_KV = 4
GROUP = N_HEADS // N_KV
KV_DIM = N_KV * HEAD
WINDOW = 128
ROPE_DIM = HEAD // 4
ROPE_THETA = 500000.0
D_FF = 2816
N_EXPERTS = 8
EXPERT_FF = D_MODEL
RMS_EPS = 1e-6
GN_EPS = 64e-5
NEG = -1e30
LANES = 128
PAIRS = D_MODEL // LANES
WKV_ROWS = 64
WKV_SETS = 2
WKV_GROUP = 16
VMEM_LIMIT = 56 * 1024 * 1024
TM = 512
TM_RWKV = 256
TM_MOE = 1024
MOE_ROWS = 128
PAST_LEN = 8192


def _cparams(*sem):
    return pltpu.CompilerParams(dimension_semantics=sem, vmem_limit_bytes=VMEM_LIMIT)


def _dot(a, b):
    return jnp.dot(a.astype(BF16), b.astype(BF16), preferred_element_type=F32)


def _dot_nt(a, b):
    return lax.dot_general(a.astype(BF16), b.astype(BF16), (((1,), (1,)), ((), ())),
                           preferred_element_type=F32)


def _dot_tn(a, b):
    return lax.dot_general(a.astype(BF16), b.astype(BF16), (((0,), (0,)), ((), ())),
                           preferred_element_type=F32)


def _split_dot(m01_wide, x):
    parts = m01_wide.shape[1] // x.shape[0]
    pieces = []
    rem = x
    for p in range(parts):
        piece = rem.astype(BF16)
        pieces.append(piece)
        if p + 1 < parts:
            rem = rem - piece.astype(F32)
    return jnp.dot(m01_wide, jnp.concatenate(pieces, axis=0), preferred_element_type=F32)


def _head_ones():
    r = lax.broadcasted_iota(jnp.int32, (LANES, LANES), 0) // HEAD
    c = lax.broadcasted_iota(jnp.int32, (LANES, LANES), 1) // HEAD
    return jnp.where(r == c, 1.0, 0.0).astype(BF16)


def _seg_sum(x):
    rows, width = x.shape
    ncol = width // LANES
    ones = _head_ones()
    stacked = jnp.concatenate([x[:, c * LANES:(c + 1) * LANES] for c in range(ncol)], axis=0)
    s = _split_dot_right(stacked, ones)
    return jnp.concatenate([s[c * rows:(c + 1) * rows] for c in range(ncol)], axis=1)


def _split_dot_right(x, m01):
    hi = x.astype(BF16)
    lo = (x - hi.astype(F32)).astype(BF16)
    return jnp.dot(jnp.concatenate([hi, lo], axis=1), jnp.concatenate([m01, m01], axis=0),
                   preferred_element_type=F32)


def _sigmoid(x):
    return 1.0 / (1.0 + jnp.exp(-x))


def _rms_mod(x, g, sh, sc):
    ms = jnp.mean(x * x, axis=-1, keepdims=True)
    y = x * lax.rsqrt(ms + RMS_EPS) * g
    return y * (1.0 + sc) + sh


def _rope_cols(x, cos, sin_up, sin_dn):
    cols = []
    for c in range(x.shape[1] // LANES):
        xc = x[:, c * LANES:(c + 1) * LANES]
        up = pltpu.roll(xc, LANES - ROPE_DIM // 2, 1)
        dn = pltpu.roll(xc, ROPE_DIM // 2, 1)
        cols.append(xc * cos + up * sin_up + dn * sin_dn)
    return jnp.concatenate(cols, axis=1)


def _full_spec(shape):
    nd = len(shape)
    return pl.BlockSpec(shape, lambda *_: (0,) * nd)


def _row_spec(tm, width):
    return pl.BlockSpec((tm, width), lambda i, *_: (i, 0))


def _mod_tables(vecs, seq_len, tiles):
    out, rep = {}, None
    for tile in set(tiles):
        if seq_len >= tile:
            out[tile] = (vecs[:, :, None, :], seq_len // tile)
        else:
            if rep is None:
                rep = jnp.repeat(vecs, seq_len, axis=1)
            out[tile] = (rep, 1)
    return out


def _mod_specs(mods, tm):
    specs, args = [], []
    for tables, idx in mods:
        table, per_block = tables[tm]
        if table.ndim == 4:
            spec = pl.BlockSpec((None, None) + table.shape[2:],
                                lambda i, *_, idx=idx, per_block=per_block: (idx, i // per_block, 0, 0))
        else:
            spec = pl.BlockSpec((None, tm, table.shape[2]), lambda i, *_, idx=idx: (idx, i, 0))
        specs.append(spec)
        args.append(table)
    return specs, args


def _rope_specs(rope, tm):
    cyc = rope[0].shape[0] // tm
    return [pl.BlockSpec((tm, LANES), lambda i, *_: (i % cyc, 0))] * 3


def _ada_kernel(c_ref, w_ref, b_ref, o_ref):
    c = c_ref[...]
    o_ref[...] = _dot(c * _sigmoid(c), w_ref[...]) + b_ref[...]


def _ada(c, w, b):
    m = c.shape[0]
    nl, _, n = w.shape
    nch = n // D_MODEL
    return pl.pallas_call(
        _ada_kernel,
        grid=(nl, nch),
        in_specs=[pl.BlockSpec((m, D_MODEL), lambda l, j: (0, 0)),
                  pl.BlockSpec((None, D_MODEL, D_MODEL), lambda l, j: (l, 0, j)),
                  pl.BlockSpec((None, 1, D_MODEL), lambda l, j: (l, 0, j))],
        out_specs=pl.BlockSpec((None, m, D_MODEL), lambda l, j: (l * nch + j, 0, 0)),
        out_shape=jax.ShapeDtypeStruct((nl * nch, m, D_MODEL), F32),
        compiler_params=_cparams("arbitrary", "arbitrary"),
        name="ada",
    )(c, w, b)


def _rwkv_proj_kernel(*refs, seq_len, tiles_per_seq, has_vres):
    it = iter(refs)
    x_ref, sh_ref, sc_ref, ng_ref, vecs_ref, hb_ref = (next(it) for _ in range(6))
    wr_ref, wk_ref, wv_ref, w1_ref, w2_ref, a1_ref, a2_ref, g1_ref, g2_ref = (next(it) for _ in range(9))
    if has_vres:
        v0_ref, v1_ref, v2_ref, vf_ref = (next(it) for _ in range(4))
    r_o, wl_o, k_o, v_o, a_o, b_o, g_o, hl_o = (next(it) for _ in range(8))
    carry = next(it)

    i = pl.program_id(0)
    tm = x_ref.shape[0]
    h = _rms_mod(x_ref[...], ng_ref[...], sh_ref[...], sc_ref[...])
    hl_o[...] = h[tm - hl_o.shape[0]:, :]
    row = lax.broadcasted_iota(jnp.int32, h.shape, 0)
    prev = pltpu.roll(h, 1, 0)
    if seq_len >= tm:
        @pl.when(i % tiles_per_seq == 0)
        def _():
            carry[0:1, :] = hb_ref[...]
        prev = jnp.where(row == 0, carry[0:1, :], prev)
        carry[0:1, :] = h[tm - 1:tm, :]
    else:
        prev = jnp.where(row % seq_len == 0, hb_ref[...], prev)
    xx = prev - h

    def mix(j):
        return (h + xx * vecs_ref[j:j + 1, :]).astype(BF16)

    r = _dot(mix(0), wr_ref[...])
    k = _dot(mix(2), wk_ref[...])
    xv = mix(3)
    v = _dot(xv, wv_ref[...])
    wz = vecs_ref[6:7, :] + _dot(jnp.tanh(_dot(mix(1), w1_ref[...])), w2_ref[...])
    wl_o[...] = -math.exp(-0.5) * _sigmoid(wz)
    a = _sigmoid(vecs_ref[7:8, :] + _dot(_dot(mix(4), a1_ref[...]), a2_ref[...]))
    g_o[...] = _dot(_sigmoid(_dot(mix(5), g1_ref[...])), g2_ref[...])
    if has_vres:
        mv = _sigmoid(v0_ref[...] + _dot(_dot(xv, v1_ref[...]), v2_ref[...]))
        v = v + (vf_ref[...] - v) * mv
    kk = k * vecs_ref[8:9, :]
    nrm = jnp.sqrt(_seg_sum(kk * kk))
    kk = kk / jnp.maximum(nrm, 1e-12)
    r_o[...] = r
    k_o[...] = k * (1.0 + (a - 1.0) * vecs_ref[9:10, :])
    v_o[...] = v
    a_o[...] = -kk
    b_o[...] = kk * a


def _rwkv_proj(x, sh, sc, ng, vecs, h_last, w, vres, *, seq_len, tm):
    n = x.shape[0]
    nt = n // tm
    mspecs, margs = _mod_specs([sh, sc, h_last], tm)
    if seq_len >= tm:
        mod_tiles, hl_rows, hl_shape = seq_len // tm, 1, (n // seq_len, 1, D_MODEL)
    else:
        mod_tiles, hl_rows, hl_shape = 1, tm, (nt, tm, D_MODEL)
    w_rkv, li, loras = w
    in_specs = [_row_spec(tm, D_MODEL), mspecs[0], mspecs[1], _full_spec(ng.shape), _full_spec(vecs.shape),
                mspecs[2]]
    in_specs += [pl.BlockSpec((None, None, D_MODEL, D_MODEL), lambda i, which=which: (li, which, 0, 0))
                 for which in range(3)]
    in_specs += [_full_spec(t.shape) for t in loras]
    args = [x, margs[0], margs[1], ng, vecs, margs[2], w_rkv, w_rkv, w_rkv, *loras]
    if vres is not None:
        v0, v1, v2, vf = vres
        in_specs += [_full_spec(v0.shape), _full_spec(v1.shape), _full_spec(v2.shape), _row_spec(tm, D_MODEL)]
        args += [v0, v1, v2, vf]
    tok = jax.ShapeDtypeStruct((n, D_MODEL), F32)
    *outs, hl = pl.pallas_call(
        functools.partial(_rwkv_proj_kernel, seq_len=seq_len, tiles_per_seq=mod_tiles,
                          has_vres=vres is not None),
        grid=(nt,),
        in_specs=in_specs,
        out_specs=[_row_spec(tm, D_MODEL)] * 7
                  + [pl.BlockSpec((None, hl_rows, D_MODEL), lambda i: (i // mod_tiles, 0, 0))],
        out_shape=[tok] * 7 + [jax.ShapeDtypeStruct(hl_shape, F32)],
        scratch_shapes=[pltpu.VMEM((8, D_MODEL), F32)],
        compiler_params=_cparams("arbitrary"),
        name="rwkv_proj",
    )(*args)
    if seq_len >= tm:
        hl = hl[:, 0, :]
    else:
        hl = hl.reshape(n // seq_len, seq_len, D_MODEL)[:, -1]
    return (*outs, hl)


def _wkv_kernel(r_ref, wl_ref, k_ref, v_ref, a_ref, b_ref, s0_ref, y_ref, so_ref, s_scr, *, nb, c):
    ci = pl.program_id(1)
    nrows = 2 * WKV_ROWS
    lane_head = lax.broadcasted_iota(jnp.int32, (c, LANES), 1) // HEAD
    ri = lax.broadcasted_iota(jnp.int32, (nrows, nrows), 0)
    cj = lax.broadcasted_iota(jnp.int32, (nrows, nrows), 1)
    same = (ri // c) == (cj // c)
    strict = jnp.where(same & (ri > cj), 1.0, 0.0)
    incl = jnp.where(same & (ri >= cj), 1.0, 0.0)
    eye = jnp.where(ri == cj, 1.0, 0.0)
    blk = jnp.where((ri // HEAD) == (cj // HEAD), 1.0, 0.0)
    ti = lax.broadcasted_iota(jnp.int32, (WKV_ROWS, 3 * WKV_ROWS), 0)
    tj = lax.broadcasted_iota(jnp.int32, (WKV_ROWS, 3 * WKV_ROWS), 1) & (WKV_ROWS - 1)
    cum01 = jnp.where(((ti // c) == (tj // c)) & (ti >= tj), 1.0, 0.0).astype(BF16)
    zero_h = jnp.zeros((HEAD, HEAD), F32)

    def seq_total(x):
        if nb == 1:
            return x[c - 1:c, :]
        return jnp.concatenate([jnp.broadcast_to(x[(g + 1) * c - 1:(g + 1) * c, :], (c, LANES))
                                for g in range(nb)], axis=0)

    def stack(x):
        parts = []
        for g in range(nb):
            xg = x[g * c:(g + 1) * c]
            parts.append(jnp.where(lane_head == 0, xg, 0.0))
            parts.append(jnp.where(lane_head == 1, xg, 0.0))
        return jnp.concatenate(parts, axis=0)

    def dup(x):
        parts = []
        for g in range(nb):
            xg = x[g * c:(g + 1) * c]
            parts += [xg, xg]
        return jnp.concatenate(parts, axis=0)

    def fold(x2):
        parts = []
        for g in range(nb):
            parts.append(x2[2 * g * c:(2 * g + 1) * c] + x2[(2 * g + 1) * c:(2 * g + 2) * c])
        return jnp.concatenate(parts, axis=0)

    nseq = s0_ref.shape[0]
    slot = lambda q, j: q * PAIRS + j

    @pl.when(ci == 0)
    def _():
        for q in range(nseq):
            for j in range(PAIRS):
                top = jnp.concatenate([s0_ref[q, 2 * j], zero_h], axis=1)
                bot = jnp.concatenate([zero_h, s0_ref[q, 2 * j + 1]], axis=1)
                s_scr[slot(q, j)] = jnp.concatenate([top, bot], axis=0)

    def load(ref, chain):
        s, j = chain
        return ref[s * nb:(s + 1) * nb, :, j * LANES:(j + 1) * LANES].reshape(WKV_ROWS, LANES)

    def state_dots(x2, chain):
        s, j = chain
        parts = [_dot_nt(x2[2 * g * c:2 * (g + 1) * c], s_scr[slot(s * nb + g, j)]) for g in range(nb)]
        return parts[0] if nb == 1 else jnp.concatenate(parts, axis=0)

    chains = [(s, j) for s in range(nseq // nb) for j in range(PAIRS)]
    for j0 in range(0, len(chains), WKV_GROUP):
        js = chains[j0:j0 + WKV_GROUP]
        each = lambda f, *cols: [f(*args) for args in zip(*cols)]
        r, wl, k, v, a, b = ([load(ref, ch) for ch in js] for ref in (r_ref, wl_ref, k_ref, v_ref, a_ref, b_ref))
        gc = each(lambda x: _split_dot(cum01, x), wl)
        gt = each(seq_total, gc)
        e_out = each(lambda x: jnp.exp(-x), gc)
        e_end = each(lambda t, x: jnp.exp(t - x), gt, gc)
        at2 = each(lambda x, y, z: stack(x * jnp.exp(y - z)), a, gc, wl)
        rt2 = each(lambda x, y: stack(x * jnp.exp(y)), r, gc)
        bt = each(lambda x, e: x * e, b, e_out)
        kt = each(lambda x, e: x * e, k, e_out)
        q = each(lambda a2, r2, x, y: _dot_nt(jnp.concatenate([a2, r2], axis=0),
                                              jnp.concatenate([dup(x), dup(y)], axis=0)),
                 at2, rt2, bt, kt)
        m_ab = each(lambda x: x[:nrows, :nrows] * strict, q)
        m_ak = each(lambda x: x[:nrows, nrows:] * strict, q)
        a_rb = each(lambda x: x[nrows:, :nrows] * incl, q)
        a_rk = each(lambda x: x[nrows:, nrows:] * incl, q)

        p = m_ab
        tinv = each(lambda x: eye + x, m_ab)
        span = 2
        while span < c:
            p = each(lambda x: _dot(x, x), p)
            tinv = each(lambda x, y: x + _dot(x, y), tinv, p)
            span *= 2
        v2 = each(stack, v)
        w2 = each(_dot, tinv, at2)
        mv = each(_dot, m_ak, v2)
        uv = each(_dot, tinv, mv)
        u2 = each(lambda x, j, y: state_dots(x, j) + y, w2, js, uv)
        rs = each(state_dots, rt2, js)
        y2 = each(lambda x, mb, mk, u_, v_: x + _dot(jnp.concatenate([mb, mk], axis=1),
                                                      jnp.concatenate([u_, v_], axis=0)),
                  rs, a_rb, a_rk, u2, v2)
        for (s, j), y2j in zip(js, y2):
            y_ref[s * nb:(s + 1) * nb, :, j * LANES:(j + 1) * LANES] = fold(y2j).reshape(nb, c, LANES)
        u = each(fold, u2)
        bh = each(lambda x, e: x * e, b, e_end)
        kh = each(lambda x, e: x * e, k, e_end)
        for idx, (s, j) in enumerate(js):
            for g in range(nb):
                rows = slice(g * c, (g + 1) * c)
                upd = _dot_tn(jnp.concatenate([u[idx][rows], v[idx][rows]], axis=0),
                              jnp.concatenate([bh[idx][rows], kh[idx][rows]], axis=0))
                decay = jnp.exp(gt[idx][g * c:g * c + 1, :])
                sl = slot(s * nb + g, j)
                s_scr[sl] = s_scr[sl] * decay + upd * blk

    @pl.when(ci == pl.num_programs(1) - 1)
    def _():
        for q in range(nseq):
            for j in range(PAIRS):
                s_fin = s_scr[slot(q, j)]
                so_ref[q, 2 * j] = s_fin[:HEAD, :HEAD]
                so_ref[q, 2 * j + 1] = s_fin[HEAD:, HEAD:]


def _wkv(r, wl, k, v, a, b, s0):
    bsz, t, _ = r.shape
    c = min(t, WKV_ROWS)
    nb = WKV_ROWS // c
    nseq = nb * math.gcd(WKV_SETS, bsz // nb)
    seq_spec = pl.BlockSpec((nseq, c, D_MODEL), lambda i, j: (i, j, 0))
    st_spec = pl.BlockSpec((nseq, N_HEADS, HEAD, HEAD), lambda i, j: (i, 0, 0, 0))
    return pl.pallas_call(
        functools.partial(_wkv_kernel, nb=nb, c=c),
        grid=(bsz // nseq, t // c),
        in_specs=[seq_spec] * 6 + [st_spec],
        out_specs=[seq_spec, st_spec],
        out_shape=[jax.ShapeDtypeStruct((bsz, t, D_MODEL), F32),
                   jax.ShapeDtypeStruct((bsz, N_HEADS, HEAD, HEAD), F32)],
        scratch_shapes=[pltpu.VMEM((nseq * PAIRS, LANES, LANES), F32)],
        compiler_params=_cparams("arbitrary", "arbitrary"),
        name="wkv",
    )(r, wl, k, v, a, b, s0)


def _rwkv_post_kernel(y_ref, r_ref, k_ref, v_ref, g_ref, x_ref, gate_ref, vecs_ref, rk_ref, wo_ref, o_ref):
    y = y_ref[...]
    inv = 1.0 / HEAD
    mu = _seg_sum(y) * inv
    d = y - mu
    var = _seg_sum(d * d) * inv
    yn = d * lax.rsqrt(var + GN_EPS) * vecs_ref[10:11, :] + vecs_ref[11:12, :]
    v = v_ref[...]
    bonus = _seg_sum(r_ref[...] * k_ref[...] * rk_ref[...]) * v
    z = (yn + bonus) * g_ref[...]
    o_ref[...] = x_ref[...] + gate_ref[...] * _dot(z, wo_ref[...])


def _rwkv_post(y, r, k, v, g, x, gate, vecs, rk, wo, li, *, tm):
    n = x.shape[0]
    mspecs, margs = _mod_specs([gate], tm)
    return pl.pallas_call(
        _rwkv_post_kernel,
        grid=(n // tm,),
        in_specs=[_row_spec(tm, D_MODEL)] * 6 + mspecs + [
                  _full_spec(vecs.shape), _full_spec(rk.shape),
                  pl.BlockSpec((None, D_MODEL, D_MODEL), lambda i: (li, 0, 0))],
        out_specs=_row_spec(tm, D_MODEL),
        out_shape=jax.ShapeDtypeStruct((n, D_MODEL), F32),
        compiler_params=_cparams("arbitrary"),
        name="rwkv_post",
    )(y, r, k, v, g, x, *margs, vecs, rk, wo)


def _ffn_kernel(x_ref, sh_ref, sc_ref, gate_ref, ng_ref, wg_ref, wu_ref, wd_ref, o_ref, h_scr, acc_scr):
    cidx = pl.program_id(1)

    @pl.when(cidx == 0)
    def _():
        h_scr[...] = _rms_mod(x_ref[...], ng_ref[...], sh_ref[...], sc_ref[...]).astype(BF16)
        acc_scr[...] = jnp.zeros_like(acc_scr)

    hb = h_scr[...]
    gt = jnp.dot(hb, wg_ref[...], preferred_element_type=F32)
    up = jnp.dot(hb, wu_ref[...], preferred_element_type=F32)
    acc_scr[...] += _dot(gt * _sigmoid(gt) * up, wd_ref[...])

    @pl.when(cidx == pl.num_programs(1) - 1)
    def _():
        o_ref[...] = x_ref[...] + gate_ref[...] * acc_scr[...]


def _ffn(x, sh, sc, gate, ng, wgu, wd, li, *, tm, tf=1408):
    n = x.shape[0]
    mspecs, margs = _mod_specs([sh, sc, gate], tm)
    nc = D_FF // tf
    return pl.pallas_call(
        _ffn_kernel,
        grid=(n // tm, nc),
        in_specs=[_row_spec(tm, D_MODEL)] + mspecs + [
                  _full_spec(ng.shape),
                  pl.BlockSpec((None, D_MODEL, tf), lambda i, j: (li, 0, j)),
                  pl.BlockSpec((None, D_MODEL, tf), lambda i, j: (li, 0, nc + j)),
                  pl.BlockSpec((None, tf, D_MODEL), lambda i, j: (li, j, 0))],
        out_specs=_row_spec(tm, D_MODEL),
        out_shape=jax.ShapeDtypeStruct((n, D_MODEL), F32),
        scratch_shapes=[pltpu.VMEM((tm, D_MODEL), BF16), pltpu.VMEM((tm, D_MODEL), F32)],
        compiler_params=_cparams("arbitrary", "arbitrary"),
        name="ffn",
    )(x, *margs, ng, wgu, wgu, wd)


def _moe_kernel(x_ref, sh_ref, sc_ref, gate_ref, ng_ref, rw_ref, rb_ref, wgu_ref, wd_ref, o_ref,
                h_scr, comb_scr, acc_scr, before_scr):
    e = pl.program_id(1)
    tm = x_ref.shape[0]
    lane = lax.broadcasted_iota(jnp.int32, (tm, LANES), 1)

    @pl.when((pl.program_id(0) == 0) & (e == 0))
    def _():
        ti = lax.broadcasted_iota(jnp.int32, (tm, tm), 0)
        tj = lax.broadcasted_iota(jnp.int32, (tm, tm), 1)
        before_scr[...] = jnp.where(tj < ti, 1.0, 0.0).astype(BF16)

    @pl.when(e == 0)
    def _():
        h = _rms_mod(x_ref[...], ng_ref[...], sh_ref[...], sc_ref[...]).astype(BF16)
        h_scr[...] = h
        logits = jnp.dot(h, rw_ref[...], preferred_element_type=F32) + rb_ref[...]
        m1 = jnp.max(logits, axis=1, keepdims=True)
        i1 = jnp.min(jnp.where(logits == m1, lane, LANES), axis=1, keepdims=True)
        rest = jnp.where(lane == i1, NEG * 2, logits)
        m2 = jnp.max(rest, axis=1, keepdims=True)
        i2 = jnp.min(jnp.where(rest == m2, lane, LANES), axis=1, keepdims=True)
        e2 = jnp.exp(m2 - m1)
        den = 1.0 + e2
        comb_scr[...] = jnp.where(lane == i1, 1.0 / den, 0.0) + jnp.where(lane == i2, e2 / den, 0.0)
        acc_scr[...] = jnp.zeros_like(acc_scr)

    ce = jnp.sum(jnp.where(lane == e, comb_scr[...], 0.0), axis=1, keepdims=True)
    picked = jnp.where(jnp.broadcast_to(ce, (tm, LANES)) > 0.0, 1.0, 0.0)
    rank = jnp.dot(before_scr[...], picked.astype(BF16), preferred_element_type=F32)
    count = jnp.sum(picked[:, 0:1]).astype(jnp.int32)
    lane_f = lane.astype(F32)

    def block(bi, carry):
        slot = rank - (bi * MOE_ROWS).astype(F32)
        onehot = jnp.concatenate(
            [jnp.where((picked > 0.0) & (slot == lane_f + float(c * LANES)), 1.0, 0.0).astype(BF16)
             for c in range(MOE_ROWS // LANES)], axis=1)
        he = lax.dot_general(onehot, h_scr[...], (((0,), (0,)), ((), ())), preferred_element_type=F32)
        gu = jnp.dot(he.astype(BF16), wgu_ref[...], preferred_element_type=F32)
        gt = gu[:, :EXPERT_FF]
        ye = _dot(gt * _sigmoid(gt) * gu[:, EXPERT_FF:], wd_ref[...])
        acc_scr[...] += ce * jnp.dot(onehot, ye.astype(BF16), preferred_element_type=F32)
        return carry

    nblocks = lax.shift_right_logical(count + (MOE_ROWS - 1), MOE_ROWS.bit_length() - 1)
    lax.fori_loop(0, nblocks, block, 0)

    @pl.when(e == pl.num_programs(1) - 1)
    def _():
        o_ref[...] = x_ref[...] + gate_ref[...] * acc_scr[...]


def _moe(x, sh, sc, gate, ng, rw, rb, wgu, wd, li, *, tm):
    n = x.shape[0]
    mspecs, margs = _mod_specs([sh, sc, gate], tm)
    return pl.pallas_call(
        _moe_kernel,
        grid=(n // tm, N_EXPERTS),
        in_specs=[_row_spec(tm, D_MODEL)] + mspecs + [
                  _full_spec(ng.shape), _full_spec(rw.shape), _full_spec(rb.shape),
                  pl.BlockSpec((None, None, D_MODEL, 2 * EXPERT_FF), lambda i, e: (li, e, 0, 0)),
                  pl.BlockSpec((None, None, EXPERT_FF, D_MODEL), lambda i, e: (li, e, 0, 0))],
        out_specs=_row_spec(tm, D_MODEL),
        out_shape=jax.ShapeDtypeStruct((n, D_MODEL), F32),
        scratch_shapes=[pltpu.VMEM((tm, D_MODEL), BF16), pltpu.VMEM((tm, LANES), F32),
                        pltpu.VMEM((tm, D_MODEL), F32), pltpu.VMEM((tm, tm), BF16)],
        compiler_params=_cparams("arbitrary", "arbitrary"),
        name="moe",
    )(x, *margs, ng, rw, rb, wgu, wd)


def _head_norm_rope(t, g, cos, sin_up, sin_dn):
    ms = _seg_sum(t * t) * (1.0 / HEAD)
    return _rope_cols(t * lax.rsqrt(ms + RMS_EPS) * g, cos, sin_up, sin_dn)


def _kv_variants(t):
    rows = t.shape[0]
    half = lax.broadcasted_iota(jnp.int32, (rows, LANES), 1) // HEAD
    out = []
    for g in range(N_KV):
        col = t[:, (g // 2) * LANES:(g // 2 + 1) * LANES]
        swapped = pltpu.roll(col, HEAD, 1)
        for he in range(2):
            src = col if he == g % 2 else swapped
            out.append(jnp.where(half == he, src, 0.0).astype(BF16))
    return jnp.concatenate(out, axis=1)


def _kv_proj_kernel(x_ref, sh_ref, sc_ref, ng_ref, w_ref, kg_ref, cos_ref, su_ref, sd_ref, k_o, v_o, kz_o, vz_o):
    h = _rms_mod(x_ref[...], ng_ref[...], sh_ref[...], sc_ref[...])
    kv = _dot(h, w_ref[...])
    k = _head_norm_rope(kv[:, :KV_DIM], kg_ref[...], cos_ref[...], su_ref[...], sd_ref[...])
    v = kv[:, KV_DIM:]
    k_o[...] = k
    v_o[...] = v
    kz_o[...] = _kv_variants(k)
    vz_o[...] = _kv_variants(v)


def _kv_proj(x, sh, sc, ng, w, kg, rope, *, tm):
    n = x.shape[0]
    mspecs, margs = _mod_specs([sh, sc], tm)
    return pl.pallas_call(
        _kv_proj_kernel,
        grid=(n // tm,),
        in_specs=[_row_spec(tm, D_MODEL)] + mspecs + [_full_spec(ng.shape), _full_spec(w.shape),
                                                      _full_spec(kg.shape)] + _rope_specs(rope, tm),
        out_specs=[_row_spec(tm, KV_DIM)] * 2 + [_row_spec(tm, D_MODEL)] * 2,
        out_shape=[jax.ShapeDtypeStruct((n, KV_DIM), F32)] * 2 + [jax.ShapeDtypeStruct((n, D_MODEL), BF16)] * 2,
        compiler_params=_cparams("arbitrary"),
        name="kv_proj",
    )(x, *margs, ng, w, kg, *rope)


def _q_proj_kernel(x_ref, sh_ref, sc_ref, ng_ref, w_ref, qg_ref, cos_ref, su_ref, sd_ref, q_o):
    h = _rms_mod(x_ref[...], ng_ref[...], sh_ref[...], sc_ref[...])
    q = _head_norm_rope(_dot(h, w_ref[...]), qg_ref[...], cos_ref[...], su_ref[...], sd_ref[...])
    q_o[...] = (q * HEAD ** -0.5).astype(BF16)


def _q_proj(x, sh, sc, ng, w, li, qg, rope, *, tm):
    n = x.shape[0]
    mspecs, margs = _mod_specs([sh, sc], tm)
    return pl.pallas_call(
        _q_proj_kernel,
        grid=(n // tm,),
        in_specs=[_row_spec(tm, D_MODEL)] + mspecs + [
                  _full_spec(ng.shape), pl.BlockSpec((None, D_MODEL, D_MODEL), lambda i: (li, 0, 0)),
                  _full_spec(qg.shape)] + _rope_specs(rope, tm),
        out_specs=_row_spec(tm, D_MODEL),
        out_shape=jax.ShapeDtypeStruct((n, D_MODEL), BF16),
        compiler_params=_cparams("arbitrary"),
        name="q_proj",
    )(x, *margs, ng, w, qg, *rope)


def _o_proj_kernel(a_ref, x_ref, gate_ref, w_ref, o_ref):
    o_ref[...] = x_ref[...] + gate_ref[...] * jnp.dot(a_ref[...], w_ref[...], preferred_element_type=F32)


def _o_proj(a, x, gate, w, li, *, tm):
    n = x.shape[0]
    mspecs, margs = _mod_specs([gate], tm)
    return pl.pallas_call(
        _o_proj_kernel,
        grid=(n // tm,),
        in_specs=[_row_spec(tm, D_MODEL), _row_spec(tm, D_MODEL)] + mspecs + [
                  pl.BlockSpec((None, D_MODEL, D_MODEL), lambda i: (li, 0, 0))],
        out_specs=_row_spec(tm, D_MODEL),
        out_shape=jax.ShapeDtypeStruct((n, D_MODEL), F32),
        compiler_params=_cparams("arbitrary"),
        name="o_proj",
    )(a, x, *margs, w)


def _scores(q, kz):
    out = []
    for g in range(N_KV):
        qs = jnp.concatenate([q[:, (2 * g) * LANES:(2 * g + 1) * LANES],
                              q[:, (2 * g + 1) * LANES:(2 * g + 2) * LANES]], axis=0).astype(BF16)
        for he in range(2):
            col = 2 * g + he
            out.append(lax.dot_general(qs, kz[:, col * LANES:(col + 1) * LANES], (((1,), (1,)), ((), ())),
                                       preferred_element_type=F32))
    return out


def _softmax_pv(problems, valid, sink_ref):
    tq, tk = valid.shape[0] // 2, valid.shape[1]
    ones = jnp.ones((tk, LANES), BF16)
    blocks = [(pi, col) for pi in range(len(problems)) for col in range(2 * N_KV)]

    def sink_rows(col):
        ha = GROUP * (col // 2) + col % 2
        return jnp.concatenate([jnp.broadcast_to(sink_ref[ha:ha + 1, :], (tq, LANES)),
                                jnp.broadcast_to(sink_ref[ha + 2:ha + 3, :], (tq, LANES))], axis=0)

    sinks = [sink_rows(col) for col in range(2 * N_KV)]
    masked = [jnp.where(valid, problems[pi][0][col], NEG) for pi, col in blocks]
    mx = [jnp.maximum(jnp.broadcast_to(jnp.max(s, axis=1, keepdims=True), (2 * tq, LANES)), sinks[col])
          for s, (pi, col) in zip(masked, blocks)]
    probs = [jnp.concatenate([jnp.exp(s[:, c * LANES:(c + 1) * LANES] - m) for c in range(tk // LANES)],
                             axis=1).astype(BF16) for s, m in zip(masked, mx)]
    pv = [jnp.dot(p, jnp.concatenate([problems[pi][1][:, col * LANES:(col + 1) * LANES], ones], axis=1),
                  preferred_element_type=F32) for p, (pi, col) in zip(probs, blocks)]
    outs = []
    for pi in range(len(problems)):
        cols = [None] * PAIRS
        for col in range(2 * N_KV):
            idx = pi * 2 * N_KV + col
            den = pv[idx][:, LANES:] + jnp.exp(sinks[col] - mx[idx])
            o = pv[idx][:, :LANES] * (1.0 / den)
            for part, qc in ((o[:tq], 2 * (col // 2)), (o[tq:], 2 * (col // 2) + 1)):
                cols[qc] = part if cols[qc] is None else cols[qc] + part
        outs.append(jnp.concatenate(cols, axis=1).astype(BF16))
    return outs


def _attn_band_kernel(q_ref, kz_ref, kzp_ref, vz_ref, vzp_ref, sink_ref, o_ref):
    i = pl.program_id(1)
    nblk = q_ref.shape[0] // WINDOW
    r = lax.broadcasted_iota(jnp.int32, (2 * WINDOW, 2 * WINDOW), 0) & (WINDOW - 1)
    c = lax.broadcasted_iota(jnp.int32, (2 * WINDOW, 2 * WINDOW), 1)
    band = (c > r) & (c <= r + WINDOW)
    band0 = band & (c >= jnp.where(i > 0, 0, WINDOW))
    rows = lambda jb: slice(jb * WINDOW, (jb + 1) * WINDOW)

    def keys(ref, prev_ref, jb):
        prev = prev_ref[...] if jb == 0 else ref[rows(jb - 1), :]
        return jnp.concatenate([prev, ref[rows(jb), :]], axis=0)

    nxt = _scores(q_ref[rows(0), :], keys(kz_ref, kzp_ref, 0))
    for jb in range(nblk):
        cur = nxt
        if jb + 1 < nblk:
            nxt = _scores(q_ref[rows(jb + 1), :], keys(kz_ref, kzp_ref, jb + 1))
        o_ref[rows(jb), :] = _softmax_pv([(cur, keys(vz_ref, vzp_ref, jb))], band0 if jb == 0 else band,
                                         sink_ref)[0]


def _attn_band(q, kz, vz, sinks, *, tq):
    bsz, t, _ = q.shape
    tq = min(tq, t)
    per = tq // WINDOW
    cur = pl.BlockSpec((None, tq, D_MODEL), lambda b, i: (b, i, 0))
    prv = pl.BlockSpec((None, WINDOW, D_MODEL), lambda b, i: (b, jnp.maximum(i * per - 1, 0), 0))
    return pl.pallas_call(
        _attn_band_kernel,
        grid=(bsz, t // tq),
        in_specs=[cur, cur, prv, cur, prv, _full_spec(sinks.shape)],
        out_specs=cur,
        out_shape=jax.ShapeDtypeStruct((bsz, t, D_MODEL), BF16),
        compiler_params=_cparams("arbitrary", "arbitrary"),
        name="attn_band",
    )(q, kz, kz, vz, vz, sinks)


def _attn_cache_kernel(q_ref, ck_ref, kn_ref, cv_ref, vn_ref, sink_ref, o_ref):
    nseq, tq, _ = q_ref.shape
    buf = ck_ref.shape[1]
    tk = 2 * buf
    zpad = jnp.zeros((tk - buf - tq, KV_DIM), F32)
    r = lax.broadcasted_iota(jnp.int32, (2 * tq, tk), 0) & (tq - 1)
    c = lax.broadcasted_iota(jnp.int32, (2 * tq, tk), 1)
    valid = ((c < buf) & (c > r)) | ((c >= buf) & (c <= buf + r))

    def operand(cache_ref, new_ref, b):
        return _kv_variants(jnp.concatenate([cache_ref[b], new_ref[b], zpad], axis=0))

    problems = [(_scores(q_ref[b], operand(ck_ref, kn_ref, b)), operand(cv_ref, vn_ref, b)) for b in range(nseq)]
    for b, o in enumerate(_softmax_pv(problems, valid, sink_ref)):
        o_ref[b] = o


def _attn_cache(q, ck, kn, cv, vn, sinks, *, group=8):
    bsz, tq, _ = q.shape
    buf = ck.shape[1]
    group = min(group, bsz)
    spec = lambda rows, w: pl.BlockSpec((group, rows, w), lambda b: (b, 0, 0))
    return pl.pallas_call(
        _attn_cache_kernel,
        grid=(bsz // group,),
        in_specs=[spec(tq, D_MODEL), spec(buf, KV_DIM), spec(tq, KV_DIM), spec(buf, KV_DIM),
                  spec(tq, KV_DIM), _full_spec(sinks.shape)],
        out_specs=spec(tq, D_MODEL),
        out_shape=jax.ShapeDtypeStruct((bsz, tq, D_MODEL), BF16),
        compiler_params=_cparams("arbitrary"),
        name="attn_cache",
    )(q, ck, kn, cv, vn, sinks)


def _rope_tables(pos):
    half = ROPE_DIM // 2
    inv = ROPE_THETA ** (-jnp.arange(half, dtype=F32) * 2.0 / ROPE_DIM)
    inv_lane = jnp.tile(jnp.concatenate([inv, inv, jnp.zeros((HEAD - ROPE_DIM,), F32)]), LANES // HEAD)
    d = jnp.arange(LANES) % HEAD
    ang = pos.astype(F32)[:, None] * inv_lane[None, :]
    cos, sin = jnp.cos(ang), jnp.sin(ang)
    return (jnp.where(d < ROPE_DIM, cos, 1.0), jnp.where(d < half, -sin, 0.0),
            jnp.where((d >= half) & (d < ROPE_DIM), sin, 0.0))


def _prep_weights(p):
    bf = lambda a: a.astype(BF16)
    pad_cols = lambda a: jnp.pad(a, ((0, 0), (0, LANES - a.shape[1])))
    pad_rows = lambda a: jnp.pad(a, ((0, LANES - a.shape[0]), (0, 0)))
    n_a = p['rwkv_w_rkv'].shape[0]
    w = dict(
        ada_w=p['ada_w'], ada_b=p['ada_b'][:, None, :],
        kv_ada_w=p['kv_ada_w'][None], kv_ada_b=p['kv_ada_b'][None, None, :],
        rwkv_w_rkv=bf(p['rwkv_w_rkv']),
        rwkv_lora=[(bf(pad_cols(p['rwkv_w1'][l])), bf(pad_rows(p['rwkv_w2'][l])),
                    bf(pad_cols(p['rwkv_a1'][l])), bf(pad_rows(p['rwkv_a2'][l])),
                    bf(p['rwkv_g1'][l]), bf(p['rwkv_g2'][l])) for l in range(n_a)],
        rwkv_vecs=[jnp.pad(p['rwkv_vecs'][l], ((0, 4), (0, 0))) for l in range(n_a)],
        rwkv_vres=[None] + [(p['rwkv_v0'][l][None, :], bf(pad_cols(p['rwkv_v1'][l])),
                             bf(pad_rows(p['rwkv_v2'][l]))) for l in range(n_a - 1)],
        rwkv_wo=bf(p['rwkv_w_o']),
        rwkv_rk=[p['rwkv_r_k'][l].reshape(1, D_MODEL) for l in range(n_a)],
        w_kv=bf(p['attn_w_kv']),
        k_g=jnp.tile(p['k_norm_g'], N_KV)[None, :],
        w_q=bf(p['attn_w_q']),
        q_g=[jnp.tile(p['q_norm_g'][j], N_HEADS)[None, :] for j in range(p['attn_w_q'].shape[0])],
        w_o=bf(p['attn_w_o']),
        sinks=[jnp.broadcast_to(p['attn_sinks'][j][:, None], (N_HEADS, LANES)) for j in range(p['attn_w_q'].shape[0])],
        ffn_gu=bf(p['ffn_w_gu']), ffn_d=bf(p['ffn_w_down']),
        moe_rw=[bf(pad_cols(p['moe_router_w'][i])) for i in range(p['moe_router_w'].shape[0])],
        moe_rb=[jnp.pad(p['moe_router_b'][i], (0, LANES - N_EXPERTS), constant_values=NEG)[None, :]
                for i in range(p['moe_router_w'].shape[0])],
        moe_gu=bf(p['moe_w_gu']), moe_d=bf(p['moe_w_down']),
        norm_g=p['norm_g'], kv_norm_g=p['kv_norm_g'][None, :],
    )
    return w


def _trunk(x, ada, kv_ada, pos, states, cache, w, *, tm):
    bsz, t, _ = x.shape
    n = bsz * t
    depth = ada.shape[0] // 6
    n_a = len(w['rwkv_lora'])
    tm = min(tm, n)
    tm_small = min(tm, TM_RWKV)
    tm_moe = min(n, TM_MOE)
    tiles = (tm, tm_small, tm_moe)
    ada_t = _mod_tables(ada, t, tiles)
    kv_ada_t = _mod_tables(kv_ada, t, tiles)
    rope = _rope_tables(pos)
    if t < tm:
        rope = tuple(jnp.tile(a, (bsz, 1)) for a in rope)
    tpad = (-t) % 8
    seq = lambda z: z.reshape(bsz, t, z.shape[-1])
    pad8 = lambda z: jnp.pad(seq(z), ((0, 0), (0, tpad), (0, 0))) if tpad else seq(z)
    xf = x.reshape(n, D_MODEL)
    new_states = []
    v_first = None
    kv_out = None
    for l in range(depth):
        sh1, sc1, g1, sh2, sc2, g2 = [(ada_t, 6 * l + i) for i in range(6)]
        ng1 = w['norm_g'][l, 0][None, :]
        ng2 = w['norm_g'][l, 1][None, :]
        if l < n_a:
            s0, hl0 = states[l]
            vres = None if l == 0 else (*w['rwkv_vres'][l], v_first)
            r, wl, k, v, a, b, g, h_last = _rwkv_proj(
                xf, sh1, sc1, ng1, w['rwkv_vecs'][l], (_mod_tables(hl0[None], t, tiles), 0),
                (w['rwkv_w_rkv'], l, w['rwkv_lora'][l]), vres, seq_len=t, tm=tm_small)
            if l == 0:
                v_first = v
            y, s_new = _wkv(pad8(r), pad8(wl), pad8(k), pad8(v), pad8(a), pad8(b), s0)
            y = y[:, :t].reshape(n, D_MODEL)
            xf = _rwkv_post(y, r, k, v, g, xf, g1, w['rwkv_vecs'][l], w['rwkv_rk'][l], w['rwkv_wo'], l,
                            tm=tm_small)
            new_states.append((s_new, h_last))
        else:
            j = l - n_a
            if j == 0:
                k_new, v_new, kz, vz = _kv_proj(xf, (kv_ada_t, 0), (kv_ada_t, 1), w['kv_norm_g'], w['w_kv'],
                                                w['k_g'], rope, tm=tm)
                k_new, v_new = seq(k_new), seq(v_new)
                if cache is None:
                    buf = min(WINDOW, t)
                    kv_out = (k_new[:, t - buf:], v_new[:, t - buf:])
                else:
                    ck, cv = cache
                    buf = ck.shape[1]
                    kv_out = (jnp.concatenate([ck, k_new], axis=1)[:, -buf:],
                              jnp.concatenate([cv, v_new], axis=1)[:, -buf:])
            q = _q_proj(xf, sh1, sc1, ng1, w['w_q'], j, w['q_g'][j], rope, tm=tm)
            if cache is None:
                o = _attn_band(seq(q), seq(kz), seq(vz), w['sinks'][j], tq=tm)
            else:
                o = _attn_cache(pad8(q.astype(F32)), ck, pad8(k_new), cv, pad8(v_new), w['sinks'][j])[:, :t]
            xf = _o_proj(o.reshape(n, D_MODEL), xf, g1, w['w_o'], j, tm=tm)
        if l % 2 == 0:
            xf = _ffn(xf, sh2, sc2, g2, ng2, w['ffn_gu'], w['ffn_d'], l // 2, tm=tm)
        else:
            xf = _moe(xf, sh2, sc2, g2, ng2, w['moe_rw'][l // 2], w['moe_rb'][l // 2], w['moe_gu'], w['moe_d'],
                      l // 2, tm=tm_moe)
    return xf.reshape(bsz, t, D_MODEL), new_states, kv_out


def kernel(x_prompt, x_sample, c_prompt, c_sample, state_wkv_0, state_shift_0, state_wkv_1, state_shift_1, cache_k, cache_v, ada_w, ada_b, norm_g, rwkv_vecs, rwkv_w_rkv, rwkv_w_o, rwkv_w1, rwkv_w2, rwkv_a1, rwkv_a2, rwkv_g1, rwkv_g2, rwkv_r_k, rwkv_v0, rwkv_v1, rwkv_v2, kv_ada_w, kv_ada_b, kv_norm_g, attn_w_kv, k_norm_g, attn_w_q, q_norm_g, attn_w_o, attn_sinks, ffn_w_gu, ffn_w_down, moe_router_w, moe_router_b, moe_w_gu, moe_w_down):
    p = dict(ada_w=ada_w, ada_b=ada_b, norm_g=norm_g, rwkv_vecs=rwkv_vecs, rwkv_w_rkv=rwkv_w_rkv,
             rwkv_w_o=rwkv_w_o, rwkv_w1=rwkv_w1, rwkv_w2=rwkv_w2, rwkv_a1=rwkv_a1, rwkv_a2=rwkv_a2,
             rwkv_g1=rwkv_g1, rwkv_g2=rwkv_g2, rwkv_r_k=rwkv_r_k, rwkv_v0=rwkv_v0, rwkv_v1=rwkv_v1,
             rwkv_v2=rwkv_v2, kv_ada_w=kv_ada_w, kv_ada_b=kv_ada_b, kv_norm_g=kv_norm_g,
             attn_w_kv=attn_w_kv, k_norm_g=k_norm_g, attn_w_q=attn_w_q, q_norm_g=q_norm_g,
             attn_w_o=attn_w_o, attn_sinks=attn_sinks, ffn_w_gu=ffn_w_gu, ffn_w_down=ffn_w_down,
             moe_router_w=moe_router_w, moe_router_b=moe_router_b, moe_w_gu=moe_w_gu,
             moe_w_down=moe_w_down)
    return _forward(x_prompt, x_sample, c_prompt, c_sample, state_wkv_0, state_shift_0, state_wkv_1,
                    state_shift_1, cache_k, cache_v, p)


def _forward(x_prompt, x_sample, c_prompt, c_sample, state_wkv_0, state_shift_0, state_wkv_1, state_shift_1,
             cache_k, cache_v, p):
    w = _prep_weights(p)
    bp, tp, _ = x_prompt.shape
    bs, ts, _ = x_sample.shape
    c_all = jnp.concatenate([c_prompt, c_sample], axis=0)
    c_all = jnp.pad(c_all, ((0, (-c_all.shape[0]) % 8), (0, 0)))
    ada = _ada(c_all, w['ada_w'], w['ada_b'])
    kv_ada = _ada(c_all, w['kv_ada_w'], w['kv_ada_b'])
    n_a = len(w['rwkv_lora'])
    zero_states = [(jnp.zeros((bp, N_HEADS, HEAD, HEAD), F32), jnp.zeros((bp, D_MODEL), F32)) for _ in range(n_a)]
    cache_k2 = cache_k.reshape(bs, cache_k.shape[1], KV_DIM)
    cache_v2 = cache_v.reshape(bs, cache_v.shape[1], KV_DIM)
    y_p, rw_p, (k_p, v_p) = _trunk(x_prompt, ada[:, :bp], kv_ada[:, :bp], jnp.arange(tp), zero_states, None, w,
                                   tm=TM)
    y_s, rw_s, (k_s, v_s) = _trunk(x_sample, ada[:, bp:bp + bs], kv_ada[:, bp:bp + bs], PAST_LEN + jnp.arange(ts),
                                   [(state_wkv_0, state_shift_0), (state_wkv_1, state_shift_1)],
                                   (cache_k2, cache_v2), w, tm=TM)
    heads = lambda z: z.reshape(z.shape[0], z.shape[1], N_KV, HEAD)
    return (y_p, y_s, rw_p[0][0], rw_s[0][0], rw_p[0][1], rw_s[0][1],
            rw_p[1][0], rw_s[1][0], rw_p[1][1], rw_s[1][1], heads(k_p), heads(k_s), heads(v_p), heads(v_s))
```

```python
import functools
import math

import jax
import jax.numpy as jnp
from jax import lax
from jax.experimental import pallas as pl
from jax.experimental.pallas import tpu as pltpu

F32 = jnp.float32
BF16 = jnp.bfloat16

D_MODEL = 1024
HEAD = 64
N_HEADS = D_MODEL // HEAD
N_KV = 4
GROUP = N_HEADS // N_KV
KV_DIM = N_KV * HEAD
WINDOW = 128
ROPE_DIM = HEAD // 4
ROPE_THETA = 500000.0
D_FF = 2816
N_EXPERTS = 8
EXPERT_FF = D_MODEL
RMS_EPS = 1e-6
GN_EPS = 64e-5
NEG = -1e30
LANES = 128
PAIRS = D_MODEL // LANES
WKV_ROWS = 64
WKV_SETS = 2
WKV_GROUP = 16
VMEM_LIMIT = 56 * 1024 * 1024
TM = 1024
TM_FFN = 512
TM_RWKV = 256
TM_MOE = 1024
MOE_ROWS = 256
PAST_LEN = 8192


def _cparams(*sem):
    return pltpu.CompilerParams(dimension_semantics=sem, vmem_limit_bytes=VMEM_LIMIT)


def _dot(a, b):
    return jnp.dot(a.astype(BF16), b.astype(BF16), preferred_element_type=F32)


def _dot_nt(a, b):
    return lax.dot_general(a.astype(BF16), b.astype(BF16), (((1,), (1,)), ((), ())),
                           preferred_element_type=F32)


def _dot_tn(a, b):
    return lax.dot_general(a.astype(BF16), b.astype(BF16), (((0,), (0,)), ((), ())),
                           preferred_element_type=F32)


def _split_dot(m01_wide, x):
    parts = m01_wide.shape[1] // x.shape[0]
    pieces = []
    rem = x
    for p in range(parts):
        piece = rem.astype(BF16)
        pieces.append(piece)
        if p + 1 < parts:
            rem = rem - piece.astype(F32)
    return jnp.dot(m01_wide, jnp.concatenate(pieces, axis=0), preferred_element_type=F32)


def _head_ones():
    r = lax.broadcasted_iota(jnp.int32, (LANES, LANES), 0) // HEAD
    c = lax.broadcasted_iota(jnp.int32, (LANES, LANES), 1) // HEAD
    return jnp.where(r == c, 1.0, 0.0).astype(BF16)


def _seg_sum(x):
    rows, width = x.shape
    ncol = width // LANES
    ones = _head_ones()
    stacked = jnp.concatenate([x[:, c * LANES:(c + 1) * LANES] for c in range(ncol)], axis=0)
    s = _split_dot_right(stacked, ones)
    return jnp.concatenate([s[c * rows:(c + 1) * rows] for c in range(ncol)], axis=1)


def _split_dot_right(x, m01):
    hi = x.astype(BF16)
    lo = (x - hi.astype(F32)).astype(BF16)
    return jnp.dot(jnp.concatenate([hi, lo], axis=1), jnp.concatenate([m01, m01], axis=0),
                   preferred_element_type=F32)


def _sigmoid(x):
    return 1.0 / (1.0 + jnp.exp(-x))


def _rms_mod(x, g, sh, sc):
    ms = jnp.mean(x * x, axis=-1, keepdims=True)
    y = x * lax.rsqrt(ms + RMS_EPS) * g
    return y * (1.0 + sc) + sh


def _rope_cols(x, cos, sin_up, sin_dn):
    cols = []
    for c in range(x.shape[1] // LANES):
        xc = x[:, c * LANES:(c + 1) * LANES]
        up = pltpu.roll(xc, LANES - ROPE_DIM // 2, 1)
        dn = pltpu.roll(xc, ROPE_DIM // 2, 1)
        cols.append(xc * cos + up * sin_up + dn * sin_dn)
    return jnp.concatenate(cols, axis=1)


def _full_spec(shape):
    nd = len(shape)
    return pl.BlockSpec(shape, lambda *_: (0,) * nd)


def _row_spec(tm, width):
    return pl.BlockSpec((tm, width), lambda i, *_: (i, 0))


def _mod_tables(vecs, seq_len, tiles):
    out, rep = {}, None
    for tile in set(tiles):
        if seq_len >= tile:
            out[tile] = (vecs[:, :, None, :], seq_len // tile)
        else:
            if rep is None:
                rep = jnp.repeat(vecs, seq_len, axis=1)
            out[tile] = (rep, 1)
    return out


def _mod_specs(mods, tm):
    specs, args = [], []
    for tables, idx in mods:
        table, per_block = tables[tm]
        if table.ndim == 4:
            spec = pl.BlockSpec((None, None) + table.shape[2:],
                                lambda i, *_, idx=idx, per_block=per_block: (idx, i // per_block, 0, 0))
        else:
            spec = pl.BlockSpec((None, tm, table.shape[2]), lambda i, *_, idx=idx: (idx, i, 0))
        specs.append(spec)
        args.append(table)
    return specs, args


def _rope_specs(rope, tm):
    cyc = rope[0].shape[0] // tm
    return [pl.BlockSpec((tm, LANES), lambda i, *_: (i % cyc, 0))] * 3


def _ada_kernel(c_ref, w_ref, b_ref, o_ref):
    c = c_ref[...]
    o_ref[...] = _dot(c * _sigmoid(c), w_ref[...]) + b_ref[...]


def _ada(c, w, b):
    m = c.shape[0]
    nl, _, n = w.shape
    nch = n // D_MODEL
    return pl.pallas_call(
        _ada_kernel,
        grid=(nl, nch),
        in_specs=[pl.BlockSpec((m, D_MODEL), lambda l, j: (0, 0)),
                  pl.BlockSpec((None, D_MODEL, D_MODEL), lambda l, j: (l, 0, j)),
                  pl.BlockSpec((None, 1, D_MODEL), lambda l, j: (l, 0, j))],
        out_specs=pl.BlockSpec((None, m, D_MODEL), lambda l, j: (l * nch + j, 0, 0)),
        out_shape=jax.ShapeDtypeStruct((nl * nch, m, D_MODEL), F32),
        compiler_params=_cparams("arbitrary", "arbitrary"),
        name="ada",
    )(c, w, b)


def _rwkv_proj_kernel(*refs, seq_len, tiles_per_seq, has_vres):
    it = iter(refs)
    x_ref, sh_ref, sc_ref, ng_ref, vecs_ref, hb_ref = (next(it) for _ in range(6))
    wr_ref, wk_ref, wv_ref, w1_ref, w2_ref, a1_ref, a2_ref, g1_ref, g2_ref = (next(it) for _ in range(9))
    if has_vres:
        v0_ref, v1_ref, v2_ref, vf_ref = (next(it) for _ in range(4))
    r_o, wl_o, k_o, v_o, a_o, b_o, g_o, hl_o = (next(it) for _ in range(8))
    carry = next(it)

    i = pl.program_id(0)
    tm = x_ref.shape[0]
    h = _rms_mod(x_ref[...], ng_ref[...], sh_ref[...], sc_ref[...])
    hl_o[...] = h[tm - hl_o.shape[0]:, :]
    row = lax.broadcasted_iota(jnp.int32, h.shape, 0)
    prev = pltpu.roll(h, 1, 0)
    if seq_len >= tm:
        @pl.when(i % tiles_per_seq == 0)
        def _():
            carry[0:1, :] = hb_ref[...]
        prev = jnp.where(row == 0, carry[0:1, :], prev)
        carry[0:1, :] = h[tm - 1:tm, :]
    else:
        prev = jnp.where(row % seq_len == 0, hb_ref[...], prev)
    xx = prev - h

    def mix(j):
        return (h + xx * vecs_ref[j:j + 1, :]).astype(BF16)

    r = _dot(mix(0), wr_ref[...])
    k = _dot(mix(2), wk_ref[...])
    xv = mix(3)
    v = _dot(xv, wv_ref[...])
    wz = vecs_ref[6:7, :] + _dot(jnp.tanh(_dot(mix(1), w1_ref[...])), w2_ref[...])
    wl_o[...] = -math.exp(-0.5) * _sigmoid(wz)
    a = _sigmoid(vecs_ref[7:8, :] + _dot(_dot(mix(4), a1_ref[...]), a2_ref[...]))
    g_o[...] = _dot(_sigmoid(_dot(mix(5), g1_ref[...])), g2_ref[...])
    if has_vres:
        mv = _sigmoid(v0_ref[...] + _dot(_dot(xv, v1_ref[...]), v2_ref[...]))
        v = v + (vf_ref[...] - v) * mv
    kk = k * vecs_ref[8:9, :]
    nrm = jnp.sqrt(_seg_sum(kk * kk))
    kk = kk / jnp.maximum(nrm, 1e-12)
    r_o[...] = r
    k_o[...] = k * (1.0 + (a - 1.0) * vecs_ref[9:10, :])
    v_o[...] = v
    a_o[...] = -kk
    b_o[...] = kk * a


def _rwkv_proj(x, sh, sc, ng, vecs, h_last, w, vres, *, seq_len, tm):
    n = x.shape[0]
    nt = n // tm
    mspecs, margs = _mod_specs([sh, sc, h_last], tm)
    if seq_len >= tm:
        mod_tiles, hl_rows, hl_shape = seq_len // tm, 1, (n // seq_len, 1, D_MODEL)
    else:
        mod_tiles, hl_rows, hl_shape = 1, tm, (nt, tm, D_MODEL)
    w_rkv, li, loras = w
    in_specs = [_row_spec(tm, D_MODEL), mspecs[0], mspecs[1], _full_spec(ng.shape), _full_spec(vecs.shape),
                mspecs[2]]
    in_specs += [pl.BlockSpec((None, None, D_MODEL, D_MODEL), lambda i, which=which: (li, which, 0, 0))
                 for which in range(3)]
    in_specs += [_full_spec(t.shape) for t in loras]
    args = [x, margs[0], margs[1], ng, vecs, margs[2], w_rkv, w_rkv, w_rkv, *loras]
    if vres is not None:
        v0, v1, v2, vf = vres
        in_specs += [_full_spec(v0.shape), _full_spec(v1.shape), _full_spec(v2.shape), _row_spec(tm, D_MODEL)]
        args += [v0, v1, v2, vf]
    tok = jax.ShapeDtypeStruct((n, D_MODEL), F32)
    *outs, hl = pl.pallas_call(
        functools.partial(_rwkv_proj_kernel, seq_len=seq_len, tiles_per_seq=mod_tiles,
                          has_vres=vres is not None),
        grid=(nt,),
        in_specs=in_specs,
        out_specs=[_row_spec(tm, D_MODEL)] * 7
                  + [pl.BlockSpec((None, hl_rows, D_MODEL), lambda i: (i // mod_tiles, 0, 0))],
        out_shape=[tok] * 7 + [jax.ShapeDtypeStruct(hl_shape, F32)],
        scratch_shapes=[pltpu.VMEM((8, D_MODEL), F32)],
        compiler_params=_cparams("arbitrary"),
        name="rwkv_proj",
    )(*args)
    if seq_len >= tm:
        hl = hl[:, 0, :]
    else:
        hl = hl.reshape(n // seq_len, seq_len, D_MODEL)[:, -1]
    return (*outs, hl)


def _wkv_kernel(r_ref, wl_ref, k_ref, v_ref, a_ref, b_ref, s0_ref, y_ref, so_ref, s_scr, *, nb, c):
    ci = pl.program_id(1)
    nrows = 2 * WKV_ROWS
    lane_head = lax.broadcasted_iota(jnp.int32, (c, LANES), 1) // HEAD
    ri = lax.broadcasted_iota(jnp.int32, (nrows, nrows), 0)
    cj = lax.broadcasted_iota(jnp.int32, (nrows, nrows), 1)
    same = (ri // c) == (cj // c)
    strict = jnp.where(same & (ri > cj), 1.0, 0.0)
    incl = jnp.where(same & (ri >= cj), 1.0, 0.0)
    eye = jnp.where(ri == cj, 1.0, 0.0)
    blk = jnp.where((ri // HEAD) == (cj // HEAD), 1.0, 0.0)
    ti = lax.broadcasted_iota(jnp.int32, (WKV_ROWS, 3 * WKV_ROWS), 0)
    tj = lax.broadcasted_iota(jnp.int32, (WKV_ROWS, 3 * WKV_ROWS), 1) & (WKV_ROWS - 1)
    cum01 = jnp.where(((ti // c) == (tj // c)) & (ti >= tj), 1.0, 0.0).astype(BF16)
    zero_h = jnp.zeros((HEAD, HEAD), F32)

    def seq_total(x):
        if nb == 1:
            return x[c - 1:c, :]
        return jnp.concatenate([jnp.broadcast_to(x[(g + 1) * c - 1:(g + 1) * c, :], (c, LANES))
                                for g in range(nb)], axis=0)

    def stack(x):
        parts = []
        for g in range(nb):
            xg = x[g * c:(g + 1) * c]
            parts.append(jnp.where(lane_head == 0, xg, 0.0))
            parts.append(jnp.where(lane_head == 1, xg, 0.0))
        return jnp.concatenate(parts, axis=0)

    def dup(x):
        parts = []
        for g in range(nb):
            xg = x[g * c:(g + 1) * c]
            parts += [xg, xg]
        return jnp.concatenate(parts, axis=0)

    def fold(x2):
        parts = []
        for g in range(nb):
            parts.append(x2[2 * g * c:(2 * g + 1) * c] + x2[(2 * g + 1) * c:(2 * g + 2) * c])
        return jnp.concatenate(parts, axis=0)

    nseq = s0_ref.shape[0]
    slot = lambda q, j: q * PAIRS + j

    @pl.when(ci == 0)
    def _():
        for q in range(nseq):
            for j in range(PAIRS):
                top = jnp.concatenate([s0_ref[q, 2 * j], zero_h], axis=1)
                bot = jnp.concatenate([zero_h, s0_ref[q, 2 * j + 1]], axis=1)
                s_scr[slot(q, j)] = jnp.concatenate([top, bot], axis=0)

    def load(ref, chain):
        s, j = chain
        return ref[s * nb:(s + 1) * nb, :, j * LANES:(j + 1) * LANES].reshape(WKV_ROWS, LANES)

    def state_dots(x2, chain):
        s, j = chain
        parts = [_dot_nt(x2[2 * g * c:2 * (g + 1) * c], s_scr[slot(s * nb + g, j)]) for g in range(nb)]
        return parts[0] if nb == 1 else jnp.concatenate(parts, axis=0)

    chains = [(s, j) for s in range(nseq // nb) for j in range(PAIRS)]
    for j0 in range(0, len(chains), WKV_GROUP):
        js = chains[j0:j0 + WKV_GROUP]
        each = lambda f, *cols: [f(*args) for args in zip(*cols)]
        r, wl, k, v, a, b = ([load(ref, ch) for ch in js] for ref in (r_ref, wl_ref, k_ref, v_ref, a_ref, b_ref))
        gc = each(lambda x: _split_dot(cum01, x), wl)
        gt = each(seq_total, gc)
        e_out = each(lambda x: jnp.exp(-x), gc)
        e_end = each(lambda t, x: jnp.exp(t - x), gt, gc)
        at2 = each(lambda x, y, z: stack(x * jnp.exp(y - z)), a, gc, wl)
        rt2 = each(lambda x, y: stack(x * jnp.exp(y)), r, gc)
        bt = each(lambda x, e: x * e, b, e_out)
        kt = each(lambda x, e: x * e, k, e_out)
        q = each(lambda a2, r2, x, y: _dot_nt(jnp.concatenate([a2, r2], axis=0),
                                              jnp.concatenate([dup(x), dup(y)], axis=0)),
                 at2, rt2, bt, kt)
        m_ab = each(lambda x: x[:nrows, :nrows] * strict, q)
        m_ak = each(lambda x: x[:nrows, nrows:] * strict, q)
        a_rb = each(lambda x: x[nrows:, :nrows] * incl, q)
        a_rk = each(lambda x: x[nrows:, nrows:] * incl, q)

        p = m_ab
        tinv = each(lambda x: eye + x, m_ab)
        span = 2
        while span < c:
            p = each(lambda x: _dot(x, x), p)
            tinv = each(lambda x, y: x + _dot(x, y), tinv, p)
            span *= 2
        v2 = each(stack, v)
        w2 = each(_dot, tinv, at2)
        mv = each(_dot, m_ak, v2)
        uv = each(_dot, tinv, mv)
        u2 = each(lambda x, j, y: state_dots(x, j) + y, w2, js, uv)
        rs = each(state_dots, rt2, js)
        y2 = each(lambda x, mb, mk, u_, v_: x + _dot(jnp.concatenate([mb, mk], axis=1),
                                                      jnp.concatenate([u_, v_], axis=0)),
                  rs, a_rb, a_rk, u2, v2)
        for (s, j), y2j in zip(js, y2):
            y_ref[s * nb:(s + 1) * nb, :, j * LANES:(j + 1) * LANES] = fold(y2j).reshape(nb, c, LANES)
        u = each(fold, u2)
        bh = each(lambda x, e: x * e, b, e_end)
        kh = each(lambda x, e: x * e, k, e_end)
        for idx, (s, j) in enumerate(js):
            for g in range(nb):
                rows = slice(g * c, (g + 1) * c)
                upd = _dot_tn(jnp.concatenate([u[idx][rows], v[idx][rows]], axis=0),
                              jnp.concatenate([bh[idx][rows], kh[idx][rows]], axis=0))
                decay = jnp.exp(gt[idx][g * c:g * c + 1, :])
                sl = slot(s * nb + g, j)
                s_scr[sl] = s_scr[sl] * decay + upd * blk

    @pl.when(ci == pl.num_programs(1) - 1)
    def _():
        for q in range(nseq):
            for j in range(PAIRS):
                s_fin = s_scr[slot(q, j)]
                so_ref[q, 2 * j] = s_fin[:HEAD, :HEAD]
                so_ref[q, 2 * j + 1] = s_fin[HEAD:, HEAD:]


def _wkv(r, wl, k, v, a, b, s0):
    bsz, t, _ = r.shape
    c = min(t, WKV_ROWS)
    nb = WKV_ROWS // c
    nseq = nb * math.gcd(WKV_SETS, bsz // nb)
    seq_spec = pl.BlockSpec((nseq, c, D_MODEL), lambda i, j: (i, j, 0))
    st_spec = pl.BlockSpec((nseq, N_HEADS, HEAD, HEAD), lambda i, j: (i, 0, 0, 0))
    return pl.pallas_call(
        functools.partial(_wkv_kernel, nb=nb, c=c),
        grid=(bsz // nseq, t // c),
        in_specs=[seq_spec] * 6 + [st_spec],
        out_specs=[seq_spec, st_spec],
        out_shape=[jax.ShapeDtypeStruct((bsz, t, D_MODEL), F32),
                   jax.ShapeDtypeStruct((bsz, N_HEADS, HEAD, HEAD), F32)],
        scratch_shapes=[pltpu.VMEM((nseq * PAIRS, LANES, LANES), F32)],
        compiler_params=_cparams("arbitrary", "arbitrary"),
        name="wkv",
    )(r, wl, k, v, a, b, s0)


def _rwkv_post_kernel(y_ref, r_ref, k_ref, v_ref, g_ref, x_ref, gate_ref, vecs_ref, rk_ref, wo_ref, o_ref):
    y = y_ref[...]
    inv = 1.0 / HEAD
    mu = _seg_sum(y) * inv
    d = y - mu
    var = _seg_sum(d * d) * inv
    yn = d * lax.rsqrt(var + GN_EPS) * vecs_ref[10:11, :] + vecs_ref[11:12, :]
    v = v_ref[...]
    bonus = _seg_sum(r_ref[...] * k_ref[...] * rk_ref[...]) * v
    z = (yn + bonus) * g_ref[...]
    o_ref[...] = x_ref[...] + gate_ref[...] * _dot(z, wo_ref[...])


def _rwkv_post(y, r, k, v, g, x, gate, vecs, rk, wo, li, *, tm):
    n = x.shape[0]
    mspecs, margs = _mod_specs([gate], tm)
    return pl.pallas_call(
        _rwkv_post_kernel,
        grid=(n // tm,),
        in_specs=[_row_spec(tm, D_MODEL)] * 6 + mspecs + [
                  _full_spec(vecs.shape), _full_spec(rk.shape),
                  pl.BlockSpec((None, D_MODEL, D_MODEL), lambda i: (li, 0, 0))],
        out_specs=_row_spec(tm, D_MODEL),
        out_shape=jax.ShapeDtypeStruct((n, D_MODEL), F32),
        compiler_params=_cparams("arbitrary"),
        name="rwkv_post",
    )(y, r, k, v, g, x, *margs, vecs, rk, wo)


def _ffn_kernel(x_ref, sh_ref, sc_ref, gate_ref, ng_ref, wg_ref, wu_ref, wd_ref, o_ref, h_scr, acc_scr):
    cidx = pl.program_id(1)

    @pl.when(cidx == 0)
    def _():
        h_scr[...] = _rms_mod(x_ref[...], ng_ref[...], sh_ref[...], sc_ref[...]).astype(BF16)
        acc_scr[...] = jnp.zeros_like(acc_scr)

    hb = h_scr[...]
    gt = jnp.dot(hb, wg_ref[...], preferred_element_type=F32)
    up = jnp.dot(hb, wu_ref[...], preferred_element_type=F32)
    acc_scr[...] += _dot(gt * _sigmoid(gt) * up, wd_ref[...])

    @pl.when(cidx == pl.num_programs(1) - 1)
    def _():
        o_ref[...] = x_ref[...] + gate_ref[...] * acc_scr[...]


def _ffn(x, sh, sc, gate, ng, wgu, wd, li, *, tm, tf=1408):
    n = x.shape[0]
    mspecs, margs = _mod_specs([sh, sc, gate], tm)
    nc = D_FF // tf
    return pl.pallas_call(
        _ffn_kernel,
        grid=(n // tm, nc),
        in_specs=[_row_spec(tm, D_MODEL)] + mspecs + [
                  _full_spec(ng.shape),
                  pl.BlockSpec((None, D_MODEL, tf), lambda i, j: (li, 0, j)),
                  pl.BlockSpec((None, D_MODEL, tf), lambda i, j: (li, 0, nc + j)),
                  pl.BlockSpec((None, tf, D_MODEL), lambda i, j: (li, j, 0))],
        out_specs=_row_spec(tm, D_MODEL),
        out_shape=jax.ShapeDtypeStruct((n, D_MODEL), F32),
        scratch_shapes=[pltpu.VMEM((tm, D_MODEL), BF16), pltpu.VMEM((tm, D_MODEL), F32)],
        compiler_params=_cparams("arbitrary", "arbitrary"),
        name="ffn",
    )(x, *margs, ng, wgu, wgu, wd)


def _moe_kernel(x_ref, sh_ref, sc_ref, gate_ref, ng_ref, rw_ref, rb_ref, wgu_ref, wd_ref, o_ref,
                h_scr, comb_scr, acc_scr, before_scr):
    e = pl.program_id(1)
    tm = x_ref.shape[0]
    lane = lax.broadcasted_iota(jnp.int32, (tm, LANES), 1)

    @pl.when((pl.program_id(0) == 0) & (e == 0))
    def _():
        ti = lax.broadcasted_iota(jnp.int32, (tm, tm), 0)
        tj = lax.broadcasted_iota(jnp.int32, (tm, tm), 1)
        before_scr[...] = jnp.where(tj < ti, 1.0, 0.0).astype(BF16)

    @pl.when(e == 0)
    def _():
        h = _rms_mod(x_ref[...], ng_ref[...], sh_ref[...], sc_ref[...]).astype(BF16)
        h_scr[...] = h
        logits = jnp.dot(h, rw_ref[...], preferred_element_type=F32) + rb_ref[...]
        m1 = jnp.max(logits, axis=1, keepdims=True)
        i1 = jnp.min(jnp.where(logits == m1, lane, LANES), axis=1, keepdims=True)
        rest = jnp.where(lane == i1, NEG * 2, logits)
        m2 = jnp.max(rest, axis=1, keepdims=True)
        i2 = jnp.min(jnp.where(rest == m2, lane, LANES), axis=1, keepdims=True)
        e2 = jnp.exp(m2 - m1)
        den = 1.0 + e2
        comb_scr[...] = jnp.where(lane == i1, 1.0 / den, 0.0) + jnp.where(lane == i2, e2 / den, 0.0)
        acc_scr[...] = jnp.zeros_like(acc_scr)

    ce = jnp.sum(jnp.where(lane == e, comb_scr[...], 0.0), axis=1, keepdims=True)
    picked = jnp.where(jnp.broadcast_to(ce, (tm, LANES)) > 0.0, 1.0, 0.0)
    rank = jnp.dot(before_scr[...], picked.astype(BF16), preferred_element_type=F32)
    count = jnp.sum(picked[:, 0:1]).astype(jnp.int32)
    lane_f = lane.astype(F32)

    def block(bi, carry):
        slot = rank - (bi * MOE_ROWS).astype(F32)
        onehot = jnp.concatenate(
            [jnp.where((picked > 0.0) & (slot == lane_f + float(c * LANES)), 1.0, 0.0).astype(BF16)
             for c in range(MOE_ROWS // LANES)], axis=1)
        he = lax.dot_general(onehot, h_scr[...], (((0,), (0,)), ((), ())), preferred_element_type=F32)
        gu = jnp.dot(he.astype(BF16), wgu_ref[...], preferred_element_type=F32)
        gt = gu[:, :EXPERT_FF]
        ye = _dot(gt * _sigmoid(gt) * gu[:, EXPERT_FF:], wd_ref[...])
        acc_scr[...] += ce * jnp.dot(onehot, ye.astype(BF16), preferred_element_type=F32)
        return carry

    nblocks = lax.shift_right_logical(count + (MOE_ROWS - 1), MOE_ROWS.bit_length() - 1)
    lax.fori_loop(0, nblocks, block, 0)

    @pl.when(e == pl.num_programs(1) - 1)
    def _():
        o_ref[...] = x_ref[...] + gate_ref[...] * acc_scr[...]


def _moe(x, sh, sc, gate, ng, rw, rb, wgu, wd, li, *, tm):
    n = x.shape[0]
    mspecs, margs = _mod_specs([sh, sc, gate], tm)
    return pl.pallas_call(
        _moe_kernel,
        grid=(n // tm, N_EXPERTS),
        in_specs=[_row_spec(tm, D_MODEL)] + mspecs + [
                  _full_spec(ng.shape), _full_spec(rw.shape), _full_spec(rb.shape),
                  pl.BlockSpec((None, None, D_MODEL, 2 * EXPERT_FF), lambda i, e: (li, e, 0, 0)),
                  pl.BlockSpec((None, None, EXPERT_FF, D_MODEL), lambda i, e: (li, e, 0, 0))],
        out_specs=_row_spec(tm, D_MODEL),
        out_shape=jax.ShapeDtypeStruct((n, D_MODEL), F32),
        scratch_shapes=[pltpu.VMEM((tm, D_MODEL), BF16), pltpu.VMEM((tm, LANES), F32),
                        pltpu.VMEM((tm, D_MODEL), F32), pltpu.VMEM((tm, tm), BF16)],
        compiler_params=_cparams("arbitrary", "arbitrary"),
        name="moe",
    )(x, *margs, ng, rw, rb, wgu, wd)


def _head_norm_rope(t, g, cos, sin_up, sin_dn):
    ms = _seg_sum(t * t) * (1.0 / HEAD)
    return _rope_cols(t * lax.rsqrt(ms + RMS_EPS) * g, cos, sin_up, sin_dn)


def _kv_variants(t):
    rows = t.shape[0]
    half = lax.broadcasted_iota(jnp.int32, (rows, LANES), 1) // HEAD
    out = []
    for g in range(N_KV):
        col = t[:, (g // 2) * LANES:(g // 2 + 1) * LANES]
        swapped = pltpu.roll(col, HEAD, 1)
        for he in range(2):
            src = col if he == g % 2 else swapped
            out.append(jnp.where(half == he, src, 0.0).astype(BF16))
    return jnp.concatenate(out, axis=1)


def _kv_proj_kernel(x_ref, sh_ref, sc_ref, ng_ref, w_ref, kg_ref, cos_ref, su_ref, sd_ref, k_o, v_o, kz_o, vz_o):
    h = _rms_mod(x_ref[...], ng_ref[...], sh_ref[...], sc_ref[...])
    kv = _dot(h, w_ref[...])
    k = _head_norm_rope(kv[:, :KV_DIM], kg_ref[...], cos_ref[...], su_ref[...], sd_ref[...])
    v = kv[:, KV_DIM:]
    k_o[...] = k
    v_o[...] = v
    kz_o[...] = _kv_variants(k)
    vz_o[...] = _kv_variants(v)


def _kv_proj(x, sh, sc, ng, w, kg, rope, *, tm):
    n = x.shape[0]
    mspecs, margs = _mod_specs([sh, sc], tm)
    return pl.pallas_call(
        _kv_proj_kernel,
        grid=(n // tm,),
        in_specs=[_row_spec(tm, D_MODEL)] + mspecs + [_full_spec(ng.shape), _full_spec(w.shape),
                                                      _full_spec(kg.shape)] + _rope_specs(rope, tm),
        out_specs=[_row_spec(tm, KV_DIM)] * 2 + [_row_spec(tm, D_MODEL)] * 2,
        out_shape=[jax.ShapeDtypeStruct((n, KV_DIM), F32)] * 2 + [jax.ShapeDtypeStruct((n, D_MODEL), BF16)] * 2,
        compiler_params=_cparams("arbitrary"),
        name="kv_proj",
    )(x, *margs, ng, w, kg, *rope)


def _q_proj_kernel(x_ref, sh_ref, sc_ref, ng_ref, w_ref, qg_ref, cos_ref, su_ref, sd_ref, q_o):
    h = _rms_mod(x_ref[...], ng_ref[...], sh_ref[...], sc_ref[...])
    q = _head_norm_rope(_dot(h, w_ref[...]), qg_ref[...], cos_ref[...], su_ref[...], sd_ref[...])
    q_o[...] = (q * HEAD ** -0.5).astype(BF16)


def _q_proj(x, sh, sc, ng, w, li, qg, rope, *, tm):
    n = x.shape[0]
    mspecs, margs = _mod_specs([sh, sc], tm)
    return pl.pallas_call(
        _q_proj_kernel,
        grid=(n // tm,),
        in_specs=[_row_spec(tm, D_MODEL)] + mspecs + [
                  _full_spec(ng.shape), pl.BlockSpec((None, D_MODEL, D_MODEL), lambda i: (li, 0, 0)),
                  _full_spec(qg.shape)] + _rope_specs(rope, tm),
        out_specs=_row_spec(tm, D_MODEL),
        out_shape=jax.ShapeDtypeStruct((n, D_MODEL), BF16),
        compiler_params=_cparams("arbitrary"),
        name="q_proj",
    )(x, *margs, ng, w, qg, *rope)


def _o_proj_kernel(a_ref, x_ref, gate_ref, w_ref, o_ref):
    o_ref[...] = x_ref[...] + gate_ref[...] * jnp.dot(a_ref[...], w_ref[...], preferred_element_type=F32)


def _o_proj(a, x, gate, w, li, *, tm):
    n = x.shape[0]
    mspecs, margs = _mod_specs([gate], tm)
    return pl.pallas_call(
        _o_proj_kernel,
        grid=(n // tm,),
        in_specs=[_row_spec(tm, D_MODEL), _row_spec(tm, D_MODEL)] + mspecs + [
                  pl.BlockSpec((None, D_MODEL, D_MODEL), lambda i: (li, 0, 0))],
        out_specs=_row_spec(tm, D_MODEL),
        out_shape=jax.ShapeDtypeStruct((n, D_MODEL), F32),
        compiler_params=_cparams("arbitrary"),
        name="o_proj",
    )(a, x, *margs, w)


def _scores(q, kz):
    out = []
    for g in range(N_KV):
        qs = jnp.concatenate([q[:, (2 * g) * LANES:(2 * g + 1) * LANES],
                              q[:, (2 * g + 1) * LANES:(2 * g + 2) * LANES]], axis=0).astype(BF16)
        for he in range(2):
            col = 2 * g + he
            out.append(lax.dot_general(qs, kz[:, col * LANES:(col + 1) * LANES], (((1,), (1,)), ((), ())),
                                       preferred_element_type=F32))
    return out


def _softmax_pv(problems, valid, sink_ref):
    tq, tk = valid.shape[0] // 2, valid.shape[1]
    ones = jnp.ones((tk, LANES), BF16)
    blocks = [(pi, col) for pi in range(len(problems)) for col in range(2 * N_KV)]

    def sink_rows(col):
        ha = GROUP * (col // 2) + col % 2
        return jnp.concatenate([jnp.broadcast_to(sink_ref[ha:ha + 1, :], (tq, LANES)),
                                jnp.broadcast_to(sink_ref[ha + 2:ha + 3, :], (tq, LANES))], axis=0)

    sinks = [sink_rows(col) for col in range(2 * N_KV)]
    masked = [jnp.where(valid, problems[pi][0][col], NEG) for pi, col in blocks]
    mx = [jnp.maximum(jnp.broadcast_to(jnp.max(s, axis=1, keepdims=True), (2 * tq, LANES)), sinks[col])
          for s, (pi, col) in zip(masked, blocks)]
    probs = [jnp.concatenate([jnp.exp(s[:, c * LANES:(c + 1) * LANES] - m) for c in range(tk // LANES)],
                             axis=1).astype(BF16) for s, m in zip(masked, mx)]
    pv = [jnp.dot(p, jnp.concatenate([problems[pi][1][:, col * LANES:(col + 1) * LANES], ones], axis=1),
                  preferred_element_type=F32) for p, (pi, col) in zip(probs, blocks)]
    outs = []
    for pi in range(len(problems)):
        cols = [None] * PAIRS
        for col in range(2 * N_KV):
            idx = pi * 2 * N_KV + col
            den = pv[idx][:, LANES:] + jnp.exp(sinks[col] - mx[idx])
            o = pv[idx][:, :LANES] * (1.0 / den)
            for part, qc in ((o[:tq], 2 * (col // 2)), (o[tq:], 2 * (col // 2) + 1)):
                cols[qc] = part if cols[qc] is None else cols[qc] + part
        outs.append(jnp.concatenate(cols, axis=1).astype(BF16))
    return outs


def _attn_band_kernel(q_ref, kz_ref, kzp_ref, vz_ref, vzp_ref, sink_ref, o_ref):
    i = pl.program_id(1)
    nblk = q_ref.shape[0] // WINDOW
    r = lax.broadcasted_iota(jnp.int32, (2 * WINDOW, 2 * WINDOW), 0) & (WINDOW - 1)
    c = lax.broadcasted_iota(jnp.int32, (2 * WINDOW, 2 * WINDOW), 1)
    band = (c > r) & (c <= r + WINDOW)
    band0 = band & (c >= jnp.where(i > 0, 0, WINDOW))
    rows = lambda jb: slice(jb * WINDOW, (jb + 1) * WINDOW)

    def keys(ref, prev_ref, jb):
        prev = prev_ref[...] if jb == 0 else ref[rows(jb - 1), :]
        return jnp.concatenate([prev, ref[rows(jb), :]], axis=0)

    nxt = _scores(q_ref[rows(0), :], keys(kz_ref, kzp_ref, 0))
    for jb in range(nblk):
        cur = nxt
        if jb + 1 < nblk:
            nxt = _scores(q_ref[rows(jb + 1), :], keys(kz_ref, kzp_ref, jb + 1))
        o_ref[rows(jb), :] = _softmax_pv([(cur, keys(vz_ref, vzp_ref, jb))], band0 if jb == 0 else band,
                                         sink_ref)[0]


def _attn_band(q, kz, vz, sinks, *, tq):
    bsz, t, _ = q.shape
    tq = min(tq, t)
    per = tq // WINDOW
    cur = pl.BlockSpec((None, tq, D_MODEL), lambda b, i: (b, i, 0))
    prv = pl.BlockSpec((None, WINDOW, D_MODEL), lambda b, i: (b, jnp.maximum(i * per - 1, 0), 0))
    return pl.pallas_call(
        _attn_band_kernel,
        grid=(bsz, t // tq),
        in_specs=[cur, cur, prv, cur, prv, _full_spec(sinks.shape)],
        out_specs=cur,
        out_shape=jax.ShapeDtypeStruct((bsz, t, D_MODEL), BF16),
        compiler_params=_cparams("arbitrary", "arbitrary"),
        name="attn_band",
    )(q, kz, kz, vz, vz, sinks)


def _attn_cache_kernel(q_ref, ck_ref, kn_ref, cv_ref, vn_ref, sink_ref, o_ref):
    nseq, tq, _ = q_ref.shape
    buf = ck_ref.shape[1]
    tk = 2 * buf
    zpad = jnp.zeros((tk - buf - tq, KV_DIM), F32)
    r = lax.broadcasted_iota(jnp.int32, (2 * tq, tk), 0) & (tq - 1)
    c = lax.broadcasted_iota(jnp.int32, (2 * tq, tk), 1)
    valid = ((c < buf) & (c > r)) | ((c >= buf) & (c <= buf + r))

    def operand(cache_ref, new_ref, b):
        return _kv_variants(jnp.concatenate([cache_ref[b], new_ref[b], zpad], axis=0))

    problems = [(_scores(q_ref[b], operand(ck_ref, kn_ref, b)), operand(cv_ref, vn_ref, b)) for b in range(nseq)]
    for b, o in enumerate(_softmax_pv(problems, valid, sink_ref)):
        o_ref[b] = o


def _attn_cache(q, ck, kn, cv, vn, sinks, *, group=8):
    bsz, tq, _ = q.shape
    buf = ck.shape[1]
    group = min(group, bsz)
    spec = lambda rows, w: pl.BlockSpec((group, rows, w), lambda b: (b, 0, 0))
    return pl.pallas_call(
        _attn_cache_kernel,
        grid=(bsz // group,),
        in_specs=[spec(tq, D_MODEL), spec(buf, KV_DIM), spec(tq, KV_DIM), spec(buf, KV_DIM),
                  spec(tq, KV_DIM), _full_spec(sinks.shape)],
        out_specs=spec(tq, D_MODEL),
        out_shape=jax.ShapeDtypeStruct((bsz, tq, D_MODEL), BF16),
        compiler_params=_cparams("arbitrary"),
        name="attn_cache",
    )(q, ck, kn, cv, vn, sinks)


def _rope_tables(pos):
    half = ROPE_DIM // 2
    inv = ROPE_THETA ** (-jnp.arange(half, dtype=F32) * 2.0 / ROPE_DIM)
    inv_lane = jnp.tile(jnp.concatenate([inv, inv, jnp.zeros((HEAD - ROPE_DIM,), F32)]), LANES // HEAD)
    d = jnp.arange(LANES) % HEAD
    ang = pos.astype(F32)[:, None] * inv_lane[None, :]
    cos, sin = jnp.cos(ang), jnp.sin(ang)
    return (jnp.where(d < ROPE_DIM, cos, 1.0), jnp.where(d < half, -sin, 0.0),
            jnp.where((d >= half) & (d < ROPE_DIM), sin, 0.0))


def _prep_weights(p):
    bf = lambda a: a.astype(BF16)
    pad_cols = lambda a: jnp.pad(a, ((0, 0), (0, LANES - a.shape[1])))
    pad_rows = lambda a: jnp.pad(a, ((0, LANES - a.shape[0]), (0, 0)))
    n_a = p['rwkv_w_rkv'].shape[0]
    w = dict(
        ada_w=p['ada_w'], ada_b=p['ada_b'][:, None, :],
        kv_ada_w=p['kv_ada_w'][None], kv_ada_b=p['kv_ada_b'][None, None, :],
        rwkv_w_rkv=bf(p['rwkv_w_rkv']),
        rwkv_lora=[(bf(pad_cols(p['rwkv_w1'][l])), bf(pad_rows(p['rwkv_w2'][l])),
                    bf(pad_cols(p['rwkv_a1'][l])), bf(pad_rows(p['rwkv_a2'][l])),
                    bf(p['rwkv_g1'][l]), bf(p['rwkv_g2'][l])) for l in range(n_a)],
        rwkv_vecs=[jnp.pad(p['rwkv_vecs'][l], ((0, 4), (0, 0))) for l in range(n_a)],
        rwkv_vres=[None] + [(p['rwkv_v0'][l][None, :], bf(pad_cols(p['rwkv_v1'][l])),
                             bf(pad_rows(p['rwkv_v2'][l]))) for l in range(n_a - 1)],
        rwkv_wo=bf(p['rwkv_w_o']),
        rwkv_rk=[p['rwkv_r_k'][l].reshape(1, D_MODEL) for l in range(n_a)],
        w_kv=bf(p['attn_w_kv']),
        k_g=jnp.tile(p['k_norm_g'], N_KV)[None, :],
        w_q=bf(p['attn_w_q']),
        q_g=[jnp.tile(p['q_norm_g'][j], N_HEADS)[None, :] for j in range(p['attn_w_q'].shape[0])],
        w_o=bf(p['attn_w_o']),
        sinks=[jnp.broadcast_to(p['attn_sinks'][j][:, None], (N_HEADS, LANES)) for j in range(p['attn_w_q'].shape[0])],
        ffn_gu=bf(p['ffn_w_gu']), ffn_d=bf(p['ffn_w_down']),
        moe_rw=[bf(pad_cols(p['moe_router_w'][i])) for i in range(p['moe_router_w'].shape[0])],
        moe_rb=[jnp.pad(p['moe_router_b'][i], (0, LANES - N_EXPERTS), constant_values=NEG)[None, :]
                for i in range(p['moe_router_w'].shape[0])],
        moe_gu=bf(p['moe_w_gu']), moe_d=bf(p['moe_w_down']),
        norm_g=p['norm_g'], kv_norm_g=p['kv_norm_g'][None, :],
    )
    return w


def _trunk(x, ada, kv_ada, pos, states, cache, w, *, tm):
    bsz, t, _ = x.shape
    n = bsz * t
    depth = ada.shape[0] // 6
    n_a = len(w['rwkv_lora'])
    tm = min(tm, n)
    tm_small = min(tm, TM_RWKV)
    tm_moe = min(n, TM_MOE)
    tm_ffn = min(n, TM_FFN)
    tiles = (tm, tm_small, tm_moe, tm_ffn)
    ada_t = _mod_tables(ada, t, tiles)
    kv_ada_t = _mod_tables(kv_ada, t, tiles)
    rope = _rope_tables(pos)
    if t < tm:
        rope = tuple(jnp.tile(a, (bsz, 1)) for a in rope)
    tpad = (-t) % 8
    seq = lambda z: z.reshape(bsz, t, z.shape[-1])
    pad8 = lambda z: jnp.pad(seq(z), ((0, 0), (0, tpad), (0, 0))) if tpad else seq(z)
    xf = x.reshape(n, D_MODEL)
    new_states = []
    v_first = None
    kv_out = None
    for l in range(depth):
        sh1, sc1, g1, sh2, sc2, g2 = [(ada_t, 6 * l + i) for i in range(6)]
        ng1 = w['norm_g'][l, 0][None, :]
        ng2 = w['norm_g'][l, 1][None, :]
        if l < n_a:
            s0, hl0 = states[l]
            vres = None if l == 0 else (*w['rwkv_vres'][l], v_first)
            r, wl, k, v, a, b, g, h_last = _rwkv_proj(
                xf, sh1, sc1, ng1, w['rwkv_vecs'][l], (_mod_tables(hl0[None], t, tiles), 0),
                (w['rwkv_w_rkv'], l, w['rwkv_lora'][l]), vres, seq_len=t, tm=tm_small)
            if l == 0:
                v_first = v
            y, s_new = _wkv(pad8(r), pad8(wl), pad8(k), pad8(v), pad8(a), pad8(b), s0)
            y = y[:, :t].reshape(n, D_MODEL)
            xf = _rwkv_post(y, r, k, v, g, xf, g1, w['rwkv_vecs'][l], w['rwkv_rk'][l], w['rwkv_wo'], l,
                            tm=tm_small)
            new_states.append((s_new, h_last))
        else:
            j = l - n_a
            if j == 0:
                k_new, v_new, kz, vz = _kv_proj(xf, (kv_ada_t, 0), (kv_ada_t, 1), w['kv_norm_g'], w['w_kv'],
                                                w['k_g'], rope, tm=tm)
                k_new, v_new = seq(k_new), seq(v_new)
                if cache is None:
                    buf = min(WINDOW, t)
                    kv_out = (k_new[:, t - buf:], v_new[:, t - buf:])
                else:
                    ck, cv = cache
                    buf = ck.shape[1]
                    kv_out = (jnp.concatenate([ck, k_new], axis=1)[:, -buf:],
                              jnp.concatenate([cv, v_new], axis=1)[:, -buf:])
            q = _q_proj(xf, sh1, sc1, ng1, w['w_q'], j, w['q_g'][j], rope, tm=tm)
            if cache is None:
                o = _attn_band(seq(q), seq(kz), seq(vz), w['sinks'][j], tq=tm)
            else:
                o = _attn_cache(pad8(q.astype(F32)), ck, pad8(k_new), cv, pad8(v_new), w['sinks'][j])[:, :t]
            xf = _o_proj(o.reshape(n, D_MODEL), xf, g1, w['w_o'], j, tm=tm)
        if l % 2 == 0:
            xf = _ffn(xf, sh2, sc2, g2, ng2, w['ffn_gu'], w['ffn_d'], l // 2, tm=tm_ffn)
        else:
            xf = _moe(xf, sh2, sc2, g2, ng2, w['moe_rw'][l // 2], w['moe_rb'][l // 2], w['moe_gu'], w['moe_d'],
                      l // 2, tm=tm_moe)
    return xf.reshape(bsz, t, D_MODEL), new_states, kv_out


def kernel(x_prompt, x_sample, c_prompt, c_sample, state_wkv_0, state_shift_0, state_wkv_1, state_shift_1, cache_k, cache_v, ada_w, ada_b, norm_g, rwkv_vecs, rwkv_w_rkv, rwkv_w_o, rwkv_w1, rwkv_w2, rwkv_a1, rwkv_a2, rwkv_g1, rwkv_g2, rwkv_r_k, rwkv_v0, rwkv_v1, rwkv_v2, kv_ada_w, kv_ada_b, kv_norm_g, attn_w_kv, k_norm_g, attn_w_q, q_norm_g, attn_w_o, attn_sinks, ffn_w_gu, ffn_w_down, moe_router_w, moe_router_b, moe_w_gu, moe_w_down):
    p = dict(ada_w=ada_w, ada_b=ada_b, norm_g=norm_g, rwkv_vecs=rwkv_vecs, rwkv_w_rkv=rwkv_w_rkv,
             rwkv_w_o=rwkv_w_o, rwkv_w1=rwkv_w1, rwkv_w2=rwkv_w2, rwkv_a1=rwkv_a1, rwkv_a2=rwkv_a2,
             rwkv_g1=rwkv_g1, rwkv_g2=rwkv_g2, rwkv_r_k=rwkv_r_k, rwkv_v0=rwkv_v0, rwkv_v1=rwkv_v1,
             rwkv_v2=rwkv_v2, kv_ada_w=kv_ada_w, kv_ada_b=kv_ada_b, kv_norm_g=kv_norm_g,
             attn_w_kv=attn_w_kv, k_norm_g=k_norm_g, attn_w_q=attn_w_q, q_norm_g=q_norm_g,
             attn_w_o=attn_w_o, attn_sinks=attn_sinks, ffn_w_gu=ffn_w_gu, ffn_w_down=ffn_w_down,
             moe_router_w=moe_router_w, moe_router_b=moe_router_b, moe_w_gu=moe_w_gu,
             moe_w_down=moe_w_down)
    return _forward(x_prompt, x_sample, c_prompt, c_sample, state_wkv_0, state_shift_0, state_wkv_1,
                    state_shift_1, cache_k, cache_v, p)


def _forward(x_prompt, x_sample, c_prompt, c_sample, state_wkv_0, state_shift_0, state_wkv_1, state_shift_1,
             cache_k, cache_v, p):
    w = _prep_weights(p)
    bp, tp, _ = x_prompt.shape
    bs, ts, _ = x_sample.shape
    c_all = jnp.concatenate([c_prompt, c_sample], axis=0)
    c_all = jnp.pad(c_all, ((0, (-c_all.shape[0]) % 8), (0, 0)))
    ada = _ada(c_all, w['ada_w'], w['ada_b'])
    kv_ada = _ada(c_all, w['kv_ada_w'], w['kv_ada_b'])
    n_a = len(w['rwkv_lora'])
    zero_states = [(jnp.zeros((bp, N_HEADS, HEAD, HEAD), F32), jnp.zeros((bp, D_MODEL), F32)) for _ in range(n_a)]
    cache_k2 = cache_k.reshape(bs, cache_k.shape[1], KV_DIM)
    cache_v2 = cache_v.reshape(bs, cache_v.shape[1], KV_DIM)
    y_p, rw_p, (k_p, v_p) = _trunk(x_prompt, ada[:, :bp], kv_ada[:, :bp], jnp.arange(tp), zero_states, None, w,
                                   tm=TM)
    y_s, rw_s, (k_s, v_s) = _trunk(x_sample, ada[:, bp:bp + bs], kv_ada[:, bp:bp + bs], PAST_LEN + jnp.arange(ts),
                                   [(state_wkv_0, state_shift_0), (state_wkv_1, state_shift_1)],
                                   (cache_k2, cache_v2), w, tm=TM)
    heads = lambda z: z.reshape(z.shape[0], z.shape[1], N_KV, HEAD)
    return (y_p, y_s, rw_p[0][0], rw_s[0][0], rw_p[0][1], rw_s[0][1],
            rw_p[1][0], rw_s[1][0], rw_p[1][1], rw_s[1][1], heads(k_p), heads(k_s), heads(v_p), heads(v_s))
```

```python
import functools
import math

import jax
import jax.numpy as jnp
from jax import lax
from jax.experimental import pallas as pl
from jax.experimental.pallas import tpu as pltpu

F32 = jnp.float32
BF16 = jnp.bfloat16

D_MODEL = 1024
HEAD = 64
N_HEADS = D_MODEL // HEAD
N_KV = 4
GROUP = N_HEADS // N_KV
KV_DIM = N_KV * HEAD
WINDOW = 128
ROPE_DIM = HEAD // 4
ROPE_THETA = 500000.0
D_FF = 2816
N_EXPERTS = 8
EXPERT_FF = D_MODEL
RMS_EPS = 1e-6
GN_EPS = 64e-5
NEG = -1e30
LANES = 128
PAIRS = D_MODEL // LANES
WKV_ROWS = 64
WKV_SETS = 2
WKV_GROUP = 16
VMEM_LIMIT = 56 * 1024 * 1024
TM = 1024
TM_FFN = 512
TM_RWKV = 256
TM_MOE = 1024
MOE_ROWS = 256
PAST_LEN = 8192


def _cparams(*sem):
    return pltpu.CompilerParams(dimension_semantics=sem, vmem_limit_bytes=VMEM_LIMIT)


def _dot(a, b):
    return jnp.dot(a.astype(BF16), b.astype(BF16), preferred_element_type=F32)


def _dot_nt(a, b):
    return lax.dot_general(a.astype(BF16), b.astype(BF16), (((1,), (1,)), ((), ())),
                           preferred_element_type=F32)


def _dot_tn(a, b):
    return lax.dot_general(a.astype(BF16), b.astype(BF16), (((0,), (0,)), ((), ())),
                           preferred_element_type=F32)


def _split_dot(m01_wide, x):
    parts = m01_wide.shape[1] // x.shape[0]
    pieces = []
    rem = x
    for p in range(parts):
        piece = rem.astype(BF16)
        pieces.append(piece)
        if p + 1 < parts:
            rem = rem - piece.astype(F32)
    return jnp.dot(m01_wide, jnp.concatenate(pieces, axis=0), preferred_element_type=F32)


def _head_ones():
    r = lax.broadcasted_iota(jnp.int32, (LANES, LANES), 0) // HEAD
    c = lax.broadcasted_iota(jnp.int32, (LANES, LANES), 1) // HEAD
    return jnp.where(r == c, 1.0, 0.0).astype(BF16)


def _seg_sum(x):
    rows, width = x.shape
    ncol = width // LANES
    ones = _head_ones()
    stacked = jnp.concatenate([x[:, c * LANES:(c + 1) * LANES] for c in range(ncol)], axis=0)
    s = _split_dot_right(stacked, ones)
    return jnp.concatenate([s[c * rows:(c + 1) * rows] for c in range(ncol)], axis=1)


def _split_dot_right(x, m01):
    hi = x.astype(BF16)
    lo = (x - hi.astype(F32)).astype(BF16)
    return jnp.dot(jnp.concatenate([hi, lo], axis=1), jnp.concatenate([m01, m01], axis=0),
                   preferred_element_type=F32)


def _sigmoid(x):
    return 1.0 / (1.0 + jnp.exp(-x))


def _rms_mod(x, g, sh, sc):
    ms = jnp.mean(x * x, axis=-1, keepdims=True)
    y = x * lax.rsqrt(ms + RMS_EPS) * g
    return y * (1.0 + sc) + sh


def _rope_cols(x, cos, sin_up, sin_dn):
    cols = []
    for c in range(x.shape[1] // LANES):
        xc = x[:, c * LANES:(c + 1) * LANES]
        up = pltpu.roll(xc, LANES - ROPE_DIM // 2, 1)
        dn = pltpu.roll(xc, ROPE_DIM // 2, 1)
        cols.append(xc * cos + up * sin_up + dn * sin_dn)
    return jnp.concatenate(cols, axis=1)


def _full_spec(shape):
    nd = len(shape)
    return pl.BlockSpec(shape, lambda *_: (0,) * nd)


def _row_spec(tm, width):
    return pl.BlockSpec((tm, width), lambda i, *_: (i, 0))


def _mod_tables(vecs, seq_len, tiles):
    out, rep = {}, None
    for tile in set(tiles):
        if seq_len >= tile:
            out[tile] = (vecs[:, :, None, :], seq_len // tile)
        else:
            if rep is None:
                rep = jnp.repeat(vecs, seq_len, axis=1)
            out[tile] = (rep, 1)
    return out


def _mod_specs(mods, tm):
    specs, args = [], []
    for tables, idx in mods:
        table, per_block = tables[tm]
        if table.ndim == 4:
            spec = pl.BlockSpec((None, None) + table.shape[2:],
                                lambda i, *_, idx=idx, per_block=per_block: (idx, i // per_block, 0, 0))
        else:
            spec = pl.BlockSpec((None, tm, table.shape[2]), lambda i, *_, idx=idx: (idx, i, 0))
        specs.append(spec)
        args.append(table)
    return specs, args


def _rope_specs(rope, tm):
    cyc = rope[0].shape[0] // tm
    return [pl.BlockSpec((tm, LANES), lambda i, *_: (i % cyc, 0))] * 3


def _ada_kernel(c_ref, w_ref, b_ref, o_ref):
    c = c_ref[...]
    o_ref[...] = _dot(c * _sigmoid(c), w_ref[...]) + b_ref[...]


def _ada(c, w, b):
    m = c.shape[0]
    nl, _, n = w.shape
    nch = n // D_MODEL
    return pl.pallas_call(
        _ada_kernel,
        grid=(nl, nch),
        in_specs=[pl.BlockSpec((m, D_MODEL), lambda l, j: (0, 0)),
                  pl.BlockSpec((None, D_MODEL, D_MODEL), lambda l, j: (l, 0, j)),
                  pl.BlockSpec((None, 1, D_MODEL), lambda l, j: (l, 0, j))],
        out_specs=pl.BlockSpec((None, m, D_MODEL), lambda l, j: (l * nch + j, 0, 0)),
        out_shape=jax.ShapeDtypeStruct((nl * nch, m, D_MODEL), F32),
        compiler_params=_cparams("arbitrary", "arbitrary"),
        name="ada",
    )(c, w, b)


def _rwkv_proj_kernel(*refs, seq_len, tiles_per_seq, has_vres):
    it = iter(refs)
    x_ref, sh_ref, sc_ref, ng_ref, vecs_ref, hb_ref = (next(it) for _ in range(6))
    wr_ref, wk_ref, wv_ref, w1_ref, w2_ref, a1_ref, a2_ref, g1_ref, g2_ref = (next(it) for _ in range(9))
    if has_vres:
        v0_ref, v1_ref, v2_ref, vf_ref = (next(it) for _ in range(4))
    r_o, wl_o, k_o, v_o, a_o, b_o, g_o, bonus_o, hl_o = (next(it) for _ in range(9))
    carry = next(it)

    i = pl.program_id(0)
    tm = x_ref.shape[0]
    h = _rms_mod(x_ref[...], ng_ref[...], sh_ref[...], sc_ref[...])
    hl_o[...] = h[tm - hl_o.shape[0]:, :]
    row = lax.broadcasted_iota(jnp.int32, h.shape, 0)
    prev = pltpu.roll(h, 1, 0)
    if seq_len >= tm:
        @pl.when(i % tiles_per_seq == 0)
        def _():
            carry[0:1, :] = hb_ref[...]
        prev = jnp.where(row == 0, carry[0:1, :], prev)
        carry[0:1, :] = h[tm - 1:tm, :]
    else:
        prev = jnp.where(row % seq_len == 0, hb_ref[...], prev)
    xx = prev - h

    def mix(j):
        return (h + xx * vecs_ref[j:j + 1, :]).astype(BF16)

    r = _dot(mix(0), wr_ref[...])
    k = _dot(mix(2), wk_ref[...])
    xv = mix(3)
    v = _dot(xv, wv_ref[...])
    wz = vecs_ref[6:7, :] + _dot(jnp.tanh(_dot(mix(1), w1_ref[...])), w2_ref[...])
    wl_o[...] = -math.exp(-0.5) * _sigmoid(wz)
    a = _sigmoid(vecs_ref[7:8, :] + _dot(_dot(mix(4), a1_ref[...]), a2_ref[...]))
    g_o[...] = _dot(_sigmoid(_dot(mix(5), g1_ref[...])), g2_ref[...])
    if has_vres:
        mv = _sigmoid(v0_ref[...] + _dot(_dot(xv, v1_ref[...]), v2_ref[...]))
        v = v + (vf_ref[...] - v) * mv
    kk = k * vecs_ref[8:9, :]
    nrm = jnp.sqrt(_seg_sum(kk * kk))
    kk = kk / jnp.maximum(nrm, 1e-12)
    k_h = k * (1.0 + (a - 1.0) * vecs_ref[9:10, :])
    r_o[...] = r
    k_o[...] = k_h
    v_o[...] = v
    a_o[...] = -kk
    b_o[...] = kk * a
    bonus_o[...] = _seg_sum(r * k_h * vecs_ref[12:13, :]) * v


def _rwkv_proj(x, sh, sc, ng, vecs, h_last, w, vres, *, seq_len, tm):
    n = x.shape[0]
    nt = n // tm
    mspecs, margs = _mod_specs([sh, sc, h_last], tm)
    if seq_len >= tm:
        mod_tiles, hl_rows, hl_shape = seq_len // tm, 1, (n // seq_len, 1, D_MODEL)
    else:
        mod_tiles, hl_rows, hl_shape = 1, tm, (nt, tm, D_MODEL)
    w_rkv, li, loras = w
    in_specs = [_row_spec(tm, D_MODEL), mspecs[0], mspecs[1], _full_spec(ng.shape), _full_spec(vecs.shape),
                mspecs[2]]
    in_specs += [pl.BlockSpec((None, None, D_MODEL, D_MODEL), lambda i, which=which: (li, which, 0, 0))
                 for which in range(3)]
    in_specs += [_full_spec(t.shape) for t in loras]
    args = [x, margs[0], margs[1], ng, vecs, margs[2], w_rkv, w_rkv, w_rkv, *loras]
    if vres is not None:
        v0, v1, v2, vf = vres
        in_specs += [_full_spec(v0.shape), _full_spec(v1.shape), _full_spec(v2.shape), _row_spec(tm, D_MODEL)]
        args += [v0, v1, v2, vf]
    tok = jax.ShapeDtypeStruct((n, D_MODEL), F32)
    *outs, hl = pl.pallas_call(
        functools.partial(_rwkv_proj_kernel, seq_len=seq_len, tiles_per_seq=mod_tiles,
                          has_vres=vres is not None),
        grid=(nt,),
        in_specs=in_specs,
        out_specs=[_row_spec(tm, D_MODEL)] * 8
                  + [pl.BlockSpec((None, hl_rows, D_MODEL), lambda i: (i // mod_tiles, 0, 0))],
        out_shape=[tok] * 8 + [jax.ShapeDtypeStruct(hl_shape, F32)],
        scratch_shapes=[pltpu.VMEM((8, D_MODEL), F32)],
        compiler_params=_cparams("arbitrary"),
        name="rwkv_proj",
    )(*args)
    if seq_len >= tm:
        hl = hl[:, 0, :]
    else:
        hl = hl.reshape(n // seq_len, seq_len, D_MODEL)[:, -1]
    return (*outs, hl)


def _wkv_kernel(r_ref, wl_ref, k_ref, v_ref, a_ref, b_ref, s0_ref, y_ref, so_ref, s_scr, *, nb, c):
    ci = pl.program_id(1)
    nrows = 2 * WKV_ROWS
    lane_head = lax.broadcasted_iota(jnp.int32, (c, LANES), 1) // HEAD
    ri = lax.broadcasted_iota(jnp.int32, (nrows, nrows), 0)
    cj = lax.broadcasted_iota(jnp.int32, (nrows, nrows), 1)
    same = (ri // c) == (cj // c)
    strict = jnp.where(same & (ri > cj), 1.0, 0.0)
    incl = jnp.where(same & (ri >= cj), 1.0, 0.0)
    eye = jnp.where(ri == cj, 1.0, 0.0)
    blk = jnp.where((ri // HEAD) == (cj // HEAD), 1.0, 0.0)
    ti = lax.broadcasted_iota(jnp.int32, (WKV_ROWS, 3 * WKV_ROWS), 0)
    tj = lax.broadcasted_iota(jnp.int32, (WKV_ROWS, 3 * WKV_ROWS), 1) & (WKV_ROWS - 1)
    cum01 = jnp.where(((ti // c) == (tj // c)) & (ti >= tj), 1.0, 0.0).astype(BF16)
    zero_h = jnp.zeros((HEAD, HEAD), F32)

    def seq_total(x):
        if nb == 1:
            return x[c - 1:c, :]
        return jnp.concatenate([jnp.broadcast_to(x[(g + 1) * c - 1:(g + 1) * c, :], (c, LANES))
                                for g in range(nb)], axis=0)

    def stack(x):
        parts = []
        for g in range(nb):
            xg = x[g * c:(g + 1) * c]
            parts.append(jnp.where(lane_head == 0, xg, 0.0))
            parts.append(jnp.where(lane_head == 1, xg, 0.0))
        return jnp.concatenate(parts, axis=0)

    def dup(x):
        parts = []
        for g in range(nb):
            xg = x[g * c:(g + 1) * c]
            parts += [xg, xg]
        return jnp.concatenate(parts, axis=0)

    def fold(x2):
        parts = []
        for g in range(nb):
            parts.append(x2[2 * g * c:(2 * g + 1) * c] + x2[(2 * g + 1) * c:(2 * g + 2) * c])
        return jnp.concatenate(parts, axis=0)

    nseq = s0_ref.shape[0]
    slot = lambda q, j: q * PAIRS + j

    @pl.when(ci == 0)
    def _():
        for q in range(nseq):
            for j in range(PAIRS):
                top = jnp.concatenate([s0_ref[q, 2 * j], zero_h], axis=1)
                bot = jnp.concatenate([zero_h, s0_ref[q, 2 * j + 1]], axis=1)
                s_scr[slot(q, j)] = jnp.concatenate([top, bot], axis=0)

    def load(ref, chain):
        s, j = chain
        return ref[s * nb:(s + 1) * nb, :, j * LANES:(j + 1) * LANES].reshape(WKV_ROWS, LANES)

    def state_dots(x2, chain):
        s, j = chain
        parts = [_dot_nt(x2[2 * g * c:2 * (g + 1) * c], s_scr[slot(s * nb + g, j)]) for g in range(nb)]
        return parts[0] if nb == 1 else jnp.concatenate(parts, axis=0)

    chains = [(s, j) for s in range(nseq // nb) for j in range(PAIRS)]
    for j0 in range(0, len(chains), WKV_GROUP):
        js = chains[j0:j0 + WKV_GROUP]
        each = lambda f, *cols: [f(*args) for args in zip(*cols)]
        r, wl, k, v, a, b = ([load(ref, ch) for ch in js] for ref in (r_ref, wl_ref, k_ref, v_ref, a_ref, b_ref))
        gc = each(lambda x: _split_dot(cum01, x), wl)
        gt = each(seq_total, gc)
        e_out = each(lambda x: jnp.exp(-x), gc)
        e_end = each(lambda t, x: jnp.exp(t - x), gt, gc)
        at2 = each(lambda x, y, z: stack(x * jnp.exp(y - z)), a, gc, wl)
        rt2 = each(lambda x, y: stack(x * jnp.exp(y)), r, gc)
        bt = each(lambda x, e: x * e, b, e_out)
        kt = each(lambda x, e: x * e, k, e_out)
        q = each(lambda a2, r2, x, y: _dot_nt(jnp.concatenate([a2, r2], axis=0),
                                              jnp.concatenate([dup(x), dup(y)], axis=0)),
                 at2, rt2, bt, kt)
        m_ab = each(lambda x: x[:nrows, :nrows] * strict, q)
        m_ak = each(lambda x: x[:nrows, nrows:] * strict, q)
        a_rb = each(lambda x: x[nrows:, :nrows] * incl, q)
        a_rk = each(lambda x: x[nrows:, nrows:] * incl, q)

        p = m_ab
        tinv = each(lambda x: eye + x, m_ab)
        span = 2
        while span < c:
            p = each(lambda x: _dot(x, x), p)
            tinv = each(lambda x, y: x + _dot(x, y), tinv, p)
            span *= 2
        v2 = each(stack, v)
        w2 = each(_dot, tinv, at2)
        mv = each(_dot, m_ak, v2)
        uv = each(_dot, tinv, mv)
        u2 = each(lambda x, j, y: state_dots(x, j) + y, w2, js, uv)
        rs = each(state_dots, rt2, js)
        y2 = each(lambda x, mb, mk, u_, v_: x + _dot(jnp.concatenate([mb, mk], axis=1),
                                                      jnp.concatenate([u_, v_], axis=0)),
                  rs, a_rb, a_rk, u2, v2)
        for (s, j), y2j in zip(js, y2):
            y_ref[s * nb:(s + 1) * nb, :, j * LANES:(j + 1) * LANES] = fold(y2j).reshape(nb, c, LANES)
        u = each(fold, u2)
        bh = each(lambda x, e: x * e, b, e_end)
        kh = each(lambda x, e: x * e, k, e_end)
        for idx, (s, j) in enumerate(js):
            for g in range(nb):
                rows = slice(g * c, (g + 1) * c)
                upd = _dot_tn(jnp.concatenate([u[idx][rows], v[idx][rows]], axis=0),
                              jnp.concatenate([bh[idx][rows], kh[idx][rows]], axis=0))
                decay = jnp.exp(gt[idx][g * c:g * c + 1, :])
                sl = slot(s * nb + g, j)
                s_scr[sl] = s_scr[sl] * decay + upd * blk

    @pl.when(ci == pl.num_programs(1) - 1)
    def _():
        for q in range(nseq):
            for j in range(PAIRS):
                s_fin = s_scr[slot(q, j)]
                so_ref[q, 2 * j] = s_fin[:HEAD, :HEAD]
                so_ref[q, 2 * j + 1] = s_fin[HEAD:, HEAD:]


def _wkv(r, wl, k, v, a, b, s0):
    bsz, t, _ = r.shape
    c = min(t, WKV_ROWS)
    nb = WKV_ROWS // c
    nseq = nb * math.gcd(WKV_SETS, bsz // nb)
    seq_spec = pl.BlockSpec((nseq, c, D_MODEL), lambda i, j: (i, j, 0))
    st_spec = pl.BlockSpec((nseq, N_HEADS, HEAD, HEAD), lambda i, j: (i, 0, 0, 0))
    return pl.pallas_call(
        functools.partial(_wkv_kernel, nb=nb, c=c),
        grid=(bsz // nseq, t // c),
        in_specs=[seq_spec] * 6 + [st_spec],
        out_specs=[seq_spec, st_spec],
        out_shape=[jax.ShapeDtypeStruct((bsz, t, D_MODEL), F32),
                   jax.ShapeDtypeStruct((bsz, N_HEADS, HEAD, HEAD), F32)],
        scratch_shapes=[pltpu.VMEM((nseq * PAIRS, LANES, LANES), F32)],
        compiler_params=_cparams("arbitrary", "arbitrary"),
        name="wkv",
    )(r, wl, k, v, a, b, s0)


def _rwkv_post_kernel(y_ref, bonus_ref, g_ref, x_ref, gate_ref, vecs_ref, wo_ref, o_ref):
    y = y_ref[...]
    inv = 1.0 / HEAD
    mu = _seg_sum(y) * inv
    d = y - mu
    var = _seg_sum(d * d) * inv
    yn = d * lax.rsqrt(var + GN_EPS) * vecs_ref[10:11, :] + vecs_ref[11:12, :]
    z = (yn + bonus_ref[...]) * g_ref[...]
    o_ref[...] = x_ref[...] + gate_ref[...] * _dot(z, wo_ref[...])


def _rwkv_post(y, bonus, g, x, gate, vecs, wo, li, *, tm):
    n = x.shape[0]
    mspecs, margs = _mod_specs([gate], tm)
    return pl.pallas_call(
        _rwkv_post_kernel,
        grid=(n // tm,),
        in_specs=[_row_spec(tm, D_MODEL)] * 4 + mspecs + [
                  _full_spec(vecs.shape), pl.BlockSpec((None, D_MODEL, D_MODEL), lambda i: (li, 0, 0))],
        out_specs=_row_spec(tm, D_MODEL),
        out_shape=jax.ShapeDtypeStruct((n, D_MODEL), F32),
        compiler_params=_cparams("arbitrary"),
        name="rwkv_post",
    )(y, bonus, g, x, *margs, vecs, wo)


def _ffn_kernel(x_ref, sh_ref, sc_ref, gate_ref, ng_ref, wg_ref, wu_ref, wd_ref, o_ref, h_scr, acc_scr):
    cidx = pl.program_id(1)

    @pl.when(cidx == 0)
    def _():
        h_scr[...] = _rms_mod(x_ref[...], ng_ref[...], sh_ref[...], sc_ref[...]).astype(BF16)
        acc_scr[...] = jnp.zeros_like(acc_scr)

    hb = h_scr[...]
    gt = jnp.dot(hb, wg_ref[...], preferred_element_type=F32)
    up = jnp.dot(hb, wu_ref[...], preferred_element_type=F32)
    acc_scr[...] += _dot(gt * _sigmoid(gt) * up, wd_ref[...])

    @pl.when(cidx == pl.num_programs(1) - 1)
    def _():
        o_ref[...] = x_ref[...] + gate_ref[...] * acc_scr[...]


def _ffn(x, sh, sc, gate, ng, wgu, wd, li, *, tm, tf=1408):
    n = x.shape[0]
    mspecs, margs = _mod_specs([sh, sc, gate], tm)
    nc = D_FF // tf
    return pl.pallas_call(
        _ffn_kernel,
        grid=(n // tm, nc),
        in_specs=[_row_spec(tm, D_MODEL)] + mspecs + [
                  _full_spec(ng.shape),
                  pl.BlockSpec((None, D_MODEL, tf), lambda i, j: (li, 0, j)),
                  pl.BlockSpec((None, D_MODEL, tf), lambda i, j: (li, 0, nc + j)),
                  pl.BlockSpec((None, tf, D_MODEL), lambda i, j: (li, j, 0))],
        out_specs=_row_spec(tm, D_MODEL),
        out_shape=jax.ShapeDtypeStruct((n, D_MODEL), F32),
        scratch_shapes=[pltpu.VMEM((tm, D_MODEL), BF16), pltpu.VMEM((tm, D_MODEL), F32)],
        compiler_params=_cparams("arbitrary", "arbitrary"),
        name="ffn",
    )(x, *margs, ng, wgu, wgu, wd)


def _moe_kernel(x_ref, sh_ref, sc_ref, gate_ref, ng_ref, rw_ref, rb_ref, wgu_ref, wd_ref, o_ref,
                h_scr, comb_scr, acc_scr, before_scr):
    e = pl.program_id(1)
    tm = x_ref.shape[0]
    lane = lax.broadcasted_iota(jnp.int32, (tm, LANES), 1)

    @pl.when((pl.program_id(0) == 0) & (e == 0))
    def _():
        ti = lax.broadcasted_iota(jnp.int32, (tm, tm), 0)
        tj = lax.broadcasted_iota(jnp.int32, (tm, tm), 1)
        before_scr[...] = jnp.where(tj < ti, 1.0, 0.0).astype(BF16)

    @pl.when(e == 0)
    def _():
        h = _rms_mod(x_ref[...], ng_ref[...], sh_ref[...], sc_ref[...]).astype(BF16)
        h_scr[...] = h
        logits = jnp.dot(h, rw_ref[...], preferred_element_type=F32) + rb_ref[...]
        m1 = jnp.max(logits, axis=1, keepdims=True)
        i1 = jnp.min(jnp.where(logits == m1, lane, LANES), axis=1, keepdims=True)
        rest = jnp.where(lane == i1, NEG * 2, logits)
        m2 = jnp.max(rest, axis=1, keepdims=True)
        i2 = jnp.min(jnp.where(rest == m2, lane, LANES), axis=1, keepdims=True)
        e2 = jnp.exp(m2 - m1)
        den = 1.0 + e2
        comb_scr[...] = jnp.where(lane == i1, 1.0 / den, 0.0) + jnp.where(lane == i2, e2 / den, 0.0)
        acc_scr[...] = jnp.zeros_like(acc_scr)

    ce = jnp.sum(jnp.where(lane == e, comb_scr[...], 0.0), axis=1, keepdims=True)
    picked = jnp.where(jnp.broadcast_to(ce, (tm, LANES)) > 0.0, 1.0, 0.0)
    rank = jnp.dot(before_scr[...], picked.astype(BF16), preferred_element_type=F32)
    count = jnp.sum(picked[:, 0:1]).astype(jnp.int32)
    lane_f = lane.astype(F32)

    def block(bi, carry):
        slot = rank - (bi * MOE_ROWS).astype(F32)
        onehot = jnp.concatenate(
            [jnp.where((picked > 0.0) & (slot == lane_f + float(c * LANES)), 1.0, 0.0).astype(BF16)
             for c in range(MOE_ROWS // LANES)], axis=1)
        he = lax.dot_general(onehot, h_scr[...], (((0,), (0,)), ((), ())), preferred_element_type=F32)
        gu = jnp.dot(he.astype(BF16), wgu_ref[...], preferred_element_type=F32)
        gt = gu[:, :EXPERT_FF]
        ye = _dot(gt * _sigmoid(gt) * gu[:, EXPERT_FF:], wd_ref[...])
        acc_scr[...] += ce * jnp.dot(onehot, ye.astype(BF16), preferred_element_type=F32)
        return carry

    nblocks = lax.shift_right_logical(count + (MOE_ROWS - 1), MOE_ROWS.bit_length() - 1)
    lax.fori_loop(0, nblocks, block, 0)

    @pl.when(e == pl.num_programs(1) - 1)
    def _():
        o_ref[...] = x_ref[...] + gate_ref[...] * acc_scr[...]


def _moe(x, sh, sc, gate, ng, rw, rb, wgu, wd, li, *, tm):
    n = x.shape[0]
    mspecs, margs = _mod_specs([sh, sc, gate], tm)
    return pl.pallas_call(
        _moe_kernel,
        grid=(n // tm, N_EXPERTS),
        in_specs=[_row_spec(tm, D_MODEL)] + mspecs + [
                  _full_spec(ng.shape), _full_spec(rw.shape), _full_spec(rb.shape),
                  pl.BlockSpec((None, None, D_MODEL, 2 * EXPERT_FF), lambda i, e: (li, e, 0, 0)),
                  pl.BlockSpec((None, None, EXPERT_FF, D_MODEL), lambda i, e: (li, e, 0, 0))],
        out_specs=_row_spec(tm, D_MODEL),
        out_shape=jax.ShapeDtypeStruct((n, D_MODEL), F32),
        scratch_shapes=[pltpu.VMEM((tm, D_MODEL), BF16), pltpu.VMEM((tm, LANES), F32),
                        pltpu.VMEM((tm, D_MODEL), F32), pltpu.VMEM((tm, tm), BF16)],
        compiler_params=_cparams("arbitrary", "arbitrary"),
        name="moe",
    )(x, *margs, ng, rw, rb, wgu, wd)


def _head_norm_rope(t, g, cos, sin_up, sin_dn):
    ms = _seg_sum(t * t) * (1.0 / HEAD)
    return _rope_cols(t * lax.rsqrt(ms + RMS_EPS) * g, cos, sin_up, sin_dn)


def _kv_variants(t):
    rows = t.shape[0]
    half = lax.broadcasted_iota(jnp.int32, (rows, LANES), 1) // HEAD
    out = []
    for g in range(N_KV):
        col = t[:, (g // 2) * LANES:(g // 2 + 1) * LANES]
        swapped = pltpu.roll(col, HEAD, 1)
        for he in range(2):
            src = col if he == g % 2 else swapped
            out.append(jnp.where(half == he, src, 0.0).astype(BF16))
    return jnp.concatenate(out, axis=1)


def _kv_proj_kernel(x_ref, sh_ref, sc_ref, ng_ref, w_ref, kg_ref, cos_ref, su_ref, sd_ref, k_o, v_o, kz_o, vz_o):
    h = _rms_mod(x_ref[...], ng_ref[...], sh_ref[...], sc_ref[...])
    kv = _dot(h, w_ref[...])
    k = _head_norm_rope(kv[:, :KV_DIM], kg_ref[...], cos_ref[...], su_ref[...], sd_ref[...])
    v = kv[:, KV_DIM:]
    k_o[...] = k
    v_o[...] = v
    kz_o[...] = _kv_variants(k)
    vz_o[...] = _kv_variants(v)


def _kv_proj(x, sh, sc, ng, w, kg, rope, *, tm):
    n = x.shape[0]
    mspecs, margs = _mod_specs([sh, sc], tm)
    return pl.pallas_call(
        _kv_proj_kernel,
        grid=(n // tm,),
        in_specs=[_row_spec(tm, D_MODEL)] + mspecs + [_full_spec(ng.shape), _full_spec(w.shape),
                                                      _full_spec(kg.shape)] + _rope_specs(rope, tm),
        out_specs=[_row_spec(tm, KV_DIM)] * 2 + [_row_spec(tm, D_MODEL)] * 2,
        out_shape=[jax.ShapeDtypeStruct((n, KV_DIM), F32)] * 2 + [jax.ShapeDtypeStruct((n, D_MODEL), BF16)] * 2,
        compiler_params=_cparams("arbitrary"),
        name="kv_proj",
    )(x, *margs, ng, w, kg, *rope)


def _q_proj_kernel(x_ref, sh_ref, sc_ref, ng_ref, w_ref, qg_ref, cos_ref, su_ref, sd_ref, q_o):
    h = _rms_mod(x_ref[...], ng_ref[...], sh_ref[...], sc_ref[...])
    q = _head_norm_rope(_dot(h, w_ref[...]), qg_ref[...], cos_ref[...], su_ref[...], sd_ref[...])
    q_o[...] = (q * HEAD ** -0.5).astype(BF16)


def _q_proj(x, sh, sc, ng, w, li, qg, rope, *, tm):
    n = x.shape[0]
    mspecs, margs = _mod_specs([sh, sc], tm)
    return pl.pallas_call(
        _q_proj_kernel,
        grid=(n // tm,),
        in_specs=[_row_spec(tm, D_MODEL)] + mspecs + [
                  _full_spec(ng.shape), pl.BlockSpec((None, D_MODEL, D_MODEL), lambda i: (li, 0, 0)),
                  _full_spec(qg.shape)] + _rope_specs(rope, tm),
        out_specs=_row_spec(tm, D_MODEL),
        out_shape=jax.ShapeDtypeStruct((n, D_MODEL), BF16),
        compiler_params=_cparams("arbitrary"),
        name="q_proj",
    )(x, *margs, ng, w, qg, *rope)


def _o_proj_kernel(a_ref, x_ref, gate_ref, w_ref, o_ref):
    o_ref[...] = x_ref[...] + gate_ref[...] * jnp.dot(a_ref[...], w_ref[...], preferred_element_type=F32)


def _o_proj(a, x, gate, w, li, *, tm):
    n = x.shape[0]
    mspecs, margs = _mod_specs([gate], tm)
    return pl.pallas_call(
        _o_proj_kernel,
        grid=(n // tm,),
        in_specs=[_row_spec(tm, D_MODEL), _row_spec(tm, D_MODEL)] + mspecs + [
                  pl.BlockSpec((None, D_MODEL, D_MODEL), lambda i: (li, 0, 0))],
        out_specs=_row_spec(tm, D_MODEL),
        out_shape=jax.ShapeDtypeStruct((n, D_MODEL), F32),
        compiler_params=_cparams("arbitrary"),
        name="o_proj",
    )(a, x, *margs, w)


def _scores(q, kz):
    out = []
    for g in range(N_KV):
        qs = jnp.concatenate([q[:, (2 * g) * LANES:(2 * g + 1) * LANES],
                              q[:, (2 * g + 1) * LANES:(2 * g + 2) * LANES]], axis=0).astype(BF16)
        for he in range(2):
            col = 2 * g + he
            out.append(lax.dot_general(qs, kz[:, col * LANES:(col + 1) * LANES], (((1,), (1,)), ((), ())),
                                       preferred_element_type=F32))
    return out


def _softmax_pv(problems, valid, sink_ref):
    tq, tk = valid.shape[0] // 2, valid.shape[1]
    ones = jnp.ones((tk, LANES), BF16)
    blocks = [(pi, col) for pi in range(len(problems)) for col in range(2 * N_KV)]

    def sink_rows(col):
        ha = GROUP * (col // 2) + col % 2
        return jnp.concatenate([jnp.broadcast_to(sink_ref[ha:ha + 1, :], (tq, LANES)),
                                jnp.broadcast_to(sink_ref[ha + 2:ha + 3, :], (tq, LANES))], axis=0)

    sinks = [sink_rows(col) for col in range(2 * N_KV)]
    masked = [jnp.where(valid, problems[pi][0][col], NEG) for pi, col in blocks]
    mx = [jnp.maximum(jnp.broadcast_to(jnp.max(s, axis=1, keepdims=True), (2 * tq, LANES)), sinks[col])
          for s, (pi, col) in zip(masked, blocks)]
    probs = [jnp.concatenate([jnp.exp(s[:, c * LANES:(c + 1) * LANES] - m) for c in range(tk // LANES)],
                             axis=1).astype(BF16) for s, m in zip(masked, mx)]
    pv = [jnp.dot(p, jnp.concatenate([problems[pi][1][:, col * LANES:(col + 1) * LANES], ones], axis=1),
                  preferred_element_type=F32) for p, (pi, col) in zip(probs, blocks)]
    outs = []
    for pi in range(len(problems)):
        cols = [None] * PAIRS
        for col in range(2 * N_KV):
            idx = pi * 2 * N_KV + col
            den = pv[idx][:, LANES:] + jnp.exp(sinks[col] - mx[idx])
            o = pv[idx][:, :LANES] * (1.0 / den)
            for part, qc in ((o[:tq], 2 * (col // 2)), (o[tq:], 2 * (col // 2) + 1)):
                cols[qc] = part if cols[qc] is None else cols[qc] + part
        outs.append(jnp.concatenate(cols, axis=1).astype(BF16))
    return outs


def _attn_band_kernel(q_ref, kz_ref, kzp_ref, vz_ref, vzp_ref, sink_ref, o_ref):
    i = pl.program_id(1)
    nblk = q_ref.shape[0] // WINDOW
    r = lax.broadcasted_iota(jnp.int32, (2 * WINDOW, 2 * WINDOW), 0) & (WINDOW - 1)
    c = lax.broadcasted_iota(jnp.int32, (2 * WINDOW, 2 * WINDOW), 1)
    band = (c > r) & (c <= r + WINDOW)
    band0 = band & (c >= jnp.where(i > 0, 0, WINDOW))
    rows = lambda jb: slice(jb * WINDOW, (jb + 1) * WINDOW)

    def keys(ref, prev_ref, jb):
        prev = prev_ref[...] if jb == 0 else ref[rows(jb - 1), :]
        return jnp.concatenate([prev, ref[rows(jb), :]], axis=0)

    nxt = _scores(q_ref[rows(0), :], keys(kz_ref, kzp_ref, 0))
    for jb in range(nblk):
        cur = nxt
        if jb + 1 < nblk:
            nxt = _scores(q_ref[rows(jb + 1), :], keys(kz_ref, kzp_ref, jb + 1))
        o_ref[rows(jb), :] = _softmax_pv([(cur, keys(vz_ref, vzp_ref, jb))], band0 if jb == 0 else band,
                                         sink_ref)[0]


def _attn_band(q, kz, vz, sinks, *, tq):
    bsz, t, _ = q.shape
    tq = min(tq, t)
    per = tq // WINDOW
    cur = pl.BlockSpec((None, tq, D_MODEL), lambda b, i: (b, i, 0))
    prv = pl.BlockSpec((None, WINDOW, D_MODEL), lambda b, i: (b, jnp.maximum(i * per - 1, 0), 0))
    return pl.pallas_call(
        _attn_band_kernel,
        grid=(bsz, t // tq),
        in_specs=[cur, cur, prv, cur, prv, _full_spec(sinks.shape)],
        out_specs=cur,
        out_shape=jax.ShapeDtypeStruct((bsz, t, D_MODEL), BF16),
        compiler_params=_cparams("arbitrary", "arbitrary"),
        name="attn_band",
    )(q, kz, kz, vz, vz, sinks)


def _attn_cache_kernel(q_ref, ck_ref, kn_ref, cv_ref, vn_ref, sink_ref, o_ref):
    nseq, tq, _ = q_ref.shape
    buf = ck_ref.shape[1]
    tk = 2 * buf
    zpad = jnp.zeros((tk - buf - tq, KV_DIM), F32)
    r = lax.broadcasted_iota(jnp.int32, (2 * tq, tk), 0) & (tq - 1)
    c = lax.broadcasted_iota(jnp.int32, (2 * tq, tk), 1)
    valid = ((c < buf) & (c > r)) | ((c >= buf) & (c <= buf + r))

    def operand(cache_ref, new_ref, b):
        return _kv_variants(jnp.concatenate([cache_ref[b], new_ref[b], zpad], axis=0))

    problems = [(_scores(q_ref[b], operand(ck_ref, kn_ref, b)), operand(cv_ref, vn_ref, b)) for b in range(nseq)]
    for b, o in enumerate(_softmax_pv(problems, valid, sink_ref)):
        o_ref[b] = o


def _attn_cache(q, ck, kn, cv, vn, sinks, *, group=8):
    bsz, tq, _ = q.shape
    buf = ck.shape[1]
    group = min(group, bsz)
    spec = lambda rows, w: pl.BlockSpec((group, rows, w), lambda b: (b, 0, 0))
    return pl.pallas_call(
        _attn_cache_kernel,
        grid=(bsz // group,),
        in_specs=[spec(tq, D_MODEL), spec(buf, KV_DIM), spec(tq, KV_DIM), spec(buf, KV_DIM),
                  spec(tq, KV_DIM), _full_spec(sinks.shape)],
        out_specs=spec(tq, D_MODEL),
        out_shape=jax.ShapeDtypeStruct((bsz, tq, D_MODEL), BF16),
        compiler_params=_cparams("arbitrary"),
        name="attn_cache",
    )(q, ck, kn, cv, vn, sinks)


def _rope_tables(pos):
    half = ROPE_DIM // 2
    inv = ROPE_THETA ** (-jnp.arange(half, dtype=F32) * 2.0 / ROPE_DIM)
    inv_lane = jnp.tile(jnp.concatenate([inv, inv, jnp.zeros((HEAD - ROPE_DIM,), F32)]), LANES // HEAD)
    d = jnp.arange(LANES) % HEAD
    ang = pos.astype(F32)[:, None] * inv_lane[None, :]
    cos, sin = jnp.cos(ang), jnp.sin(ang)
    return (jnp.where(d < ROPE_DIM, cos, 1.0), jnp.where(d < half, -sin, 0.0),
            jnp.where((d >= half) & (d < ROPE_DIM), sin, 0.0))


def _prep_weights(p):
    bf = lambda a: a.astype(BF16)
    pad_cols = lambda a: jnp.pad(a, ((0, 0), (0, LANES - a.shape[1])))
    pad_rows = lambda a: jnp.pad(a, ((0, LANES - a.shape[0]), (0, 0)))
    n_a = p['rwkv_w_rkv'].shape[0]
    w = dict(
        ada_w=p['ada_w'], ada_b=p['ada_b'][:, None, :],
        kv_ada_w=p['kv_ada_w'][None], kv_ada_b=p['kv_ada_b'][None, None, :],
        rwkv_w_rkv=bf(p['rwkv_w_rkv']),
        rwkv_lora=[(bf(pad_cols(p['rwkv_w1'][l])), bf(pad_rows(p['rwkv_w2'][l])),
                    bf(pad_cols(p['rwkv_a1'][l])), bf(pad_rows(p['rwkv_a2'][l])),
                    bf(p['rwkv_g1'][l]), bf(p['rwkv_g2'][l])) for l in range(n_a)],
        rwkv_vecs=[jnp.concatenate([p['rwkv_vecs'][l], p['rwkv_r_k'][l].reshape(1, D_MODEL),
                                    jnp.zeros((3, D_MODEL), F32)], axis=0) for l in range(n_a)],
        rwkv_vres=[None] + [(p['rwkv_v0'][l][None, :], bf(pad_cols(p['rwkv_v1'][l])),
                             bf(pad_rows(p['rwkv_v2'][l]))) for l in range(n_a - 1)],
        rwkv_wo=bf(p['rwkv_w_o']),
        rwkv_rk=[p['rwkv_r_k'][l].reshape(1, D_MODEL) for l in range(n_a)],
        w_kv=bf(p['attn_w_kv']),
        k_g=jnp.tile(p['k_norm_g'], N_KV)[None, :],
        w_q=bf(p['attn_w_q']),
        q_g=[jnp.tile(p['q_norm_g'][j], N_HEADS)[None, :] for j in range(p['attn_w_q'].shape[0])],
        w_o=bf(p['attn_w_o']),
        sinks=[jnp.broadcast_to(p['attn_sinks'][j][:, None], (N_HEADS, LANES)) for j in range(p['attn_w_q'].shape[0])],
        ffn_gu=bf(p['ffn_w_gu']), ffn_d=bf(p['ffn_w_down']),
        moe_rw=[bf(pad_cols(p['moe_router_w'][i])) for i in range(p['moe_router_w'].shape[0])],
        moe_rb=[jnp.pad(p['moe_router_b'][i], (0, LANES - N_EXPERTS), constant_values=NEG)[None, :]
                for i in range(p['moe_router_w'].shape[0])],
        moe_gu=bf(p['moe_w_gu']), moe_d=bf(p['moe_w_down']),
        norm_g=p['norm_g'], kv_norm_g=p['kv_norm_g'][None, :],
    )
    return w


def _trunk(x, ada, kv_ada, pos, states, cache, w, *, tm):
    bsz, t, _ = x.shape
    n = bsz * t
    depth = ada.shape[0] // 6
    n_a = len(w['rwkv_lora'])
    tm = min(tm, n)
    tm_small = min(tm, TM_RWKV)
    tm_moe = min(n, TM_MOE)
    tm_ffn = min(n, TM_FFN)
    tiles = (tm, tm_small, tm_moe, tm_ffn)
    ada_t = _mod_tables(ada, t, tiles)
    kv_ada_t = _mod_tables(kv_ada, t, tiles)
    rope = _rope_tables(pos)
    if t < tm:
        rope = tuple(jnp.tile(a, (bsz, 1)) for a in rope)
    tpad = (-t) % 8
    seq = lambda z: z.reshape(bsz, t, z.shape[-1])
    pad8 = lambda z: jnp.pad(seq(z), ((0, 0), (0, tpad), (0, 0))) if tpad else seq(z)
    xf = x.reshape(n, D_MODEL)
    new_states = []
    v_first = None
    kv_out = None
    for l in range(depth):
        sh1, sc1, g1, sh2, sc2, g2 = [(ada_t, 6 * l + i) for i in range(6)]
        ng1 = w['norm_g'][l, 0][None, :]
        ng2 = w['norm_g'][l, 1][None, :]
        if l < n_a:
            s0, hl0 = states[l]
            vres = None if l == 0 else (*w['rwkv_vres'][l], v_first)
            r, wl, k, v, a, b, g, bonus, h_last = _rwkv_proj(
                xf, sh1, sc1, ng1, w['rwkv_vecs'][l], (_mod_tables(hl0[None], t, tiles), 0),
                (w['rwkv_w_rkv'], l, w['rwkv_lora'][l]), vres, seq_len=t, tm=tm_small)
            if l == 0:
                v_first = v
            y, s_new = _wkv(pad8(r), pad8(wl), pad8(k), pad8(v), pad8(a), pad8(b), s0)
            y = y[:, :t].reshape(n, D_MODEL)
            xf = _rwkv_post(y, bonus, g, xf, g1, w['rwkv_vecs'][l], w['rwkv_wo'], l, tm=tm_ffn)
            new_states.append((s_new, h_last))
        else:
            j = l - n_a
            if j == 0:
                k_new, v_new, kz, vz = _kv_proj(xf, (kv_ada_t, 0), (kv_ada_t, 1), w['kv_norm_g'], w['w_kv'],
                                                w['k_g'], rope, tm=tm)
                k_new, v_new = seq(k_new), seq(v_new)
                if cache is None:
                    buf = min(WINDOW, t)
                    kv_out = (k_new[:, t - buf:], v_new[:, t - buf:])
                else:
                    ck, cv = cache
                    buf = ck.shape[1]
                    kv_out = (jnp.concatenate([ck, k_new], axis=1)[:, -buf:],
                              jnp.concatenate([cv, v_new], axis=1)[:, -buf:])
            q = _q_proj(xf, sh1, sc1, ng1, w['w_q'], j, w['q_g'][j], rope, tm=tm)
            if cache is None:
                o = _attn_band(seq(q), seq(kz), seq(vz), w['sinks'][j], tq=tm)
            else:
                o = _attn_cache(pad8(q.astype(F32)), ck, pad8(k_new), cv, pad8(v_new), w['sinks'][j])[:, :t]
            xf = _o_proj(o.reshape(n, D_MODEL), xf, g1, w['w_o'], j, tm=tm)
        if l % 2 == 0:
            xf = _ffn(xf, sh2, sc2, g2, ng2, w['ffn_gu'], w['ffn_d'], l // 2, tm=tm_ffn)
        else:
            xf = _moe(xf, sh2, sc2, g2, ng2, w['moe_rw'][l // 2], w['moe_rb'][l // 2], w['moe_gu'], w['moe_d'],
                      l // 2, tm=tm_moe)
    return xf.reshape(bsz, t, D_MODEL), new_states, kv_out


def kernel(x_prompt, x_sample, c_prompt, c_sample, state_wkv_0, state_shift_0, state_wkv_1, state_shift_1, cache_k, cache_v, ada_w, ada_b, norm_g, rwkv_vecs, rwkv_w_rkv, rwkv_w_o, rwkv_w1, rwkv_w2, rwkv_a1, rwkv_a2, rwkv_g1, rwkv_g2, rwkv_r_k, rwkv_v0, rwkv_v1, rwkv_v2, kv_ada_w, kv_ada_b, kv_norm_g, attn_w_kv, k_norm_g, attn_w_q, q_norm_g, attn_w_o, attn_sinks, ffn_w_gu, ffn_w_down, moe_router_w, moe_router_b, moe_w_gu, moe_w_down):
    p = dict(ada_w=ada_w, ada_b=ada_b, norm_g=norm_g, rwkv_vecs=rwkv_vecs, rwkv_w_rkv=rwkv_w_rkv,
             rwkv_w_o=rwkv_w_o, rwkv_w1=rwkv_w1, rwkv_w2=rwkv_w2, rwkv_a1=rwkv_a1, rwkv_a2=rwkv_a2,
             rwkv_g1=rwkv_g1, rwkv_g2=rwkv_g2, rwkv_r_k=rwkv_r_k, rwkv_v0=rwkv_v0, rwkv_v1=rwkv_v1,
             rwkv_v2=rwkv_v2, kv_ada_w=kv_ada_w, kv_ada_b=kv_ada_b, kv_norm_g=kv_norm_g,
             attn_w_kv=attn_w_kv, k_norm_g=k_norm_g, attn_w_q=attn_w_q, q_norm_g=q_norm_g,
             attn_w_o=attn_w_o, attn_sinks=attn_sinks, ffn_w_gu=ffn_w_gu, ffn_w_down=ffn_w_down,
             moe_router_w=moe_router_w, moe_router_b=moe_router_b, moe_w_gu=moe_w_gu,
             moe_w_down=moe_w_down)
    return _forward(x_prompt, x_sample, c_prompt, c_sample, state_wkv_0, state_shift_0, state_wkv_1,
                    state_shift_1, cache_k, cache_v, p)


def _forward(x_prompt, x_sample, c_prompt, c_sample, state_wkv_0, state_shift_0, state_wkv_1, state_shift_1,
             cache_k, cache_v, p):
    w = _prep_weights(p)
    bp, tp, _ = x_prompt.shape
    bs, ts, _ = x_sample.shape
    c_all = jnp.concatenate([c_prompt, c_sample], axis=0)
    c_all = jnp.pad(c_all, ((0, (-c_all.shape[0]) % 8), (0, 0)))
    ada = _ada(c_all, w['ada_w'], w['ada_b'])
    kv_ada = _ada(c_all, w['kv_ada_w'], w['kv_ada_b'])
    n_a = len(w['rwkv_lora'])
    zero_states = [(jnp.zeros((bp, N_HEADS, HEAD, HEAD), F32), jnp.zeros((bp, D_MODEL), F32)) for _ in range(n_a)]
    cache_k2 = cache_k.reshape(bs, cache_k.shape[1], KV_DIM)
    cache_v2 = cache_v.reshape(bs, cache_v.shape[1], KV_DIM)
    y_p, rw_p, (k_p, v_p) = _trunk(x_prompt, ada[:, :bp], kv_ada[:, :bp], jnp.arange(tp), zero_states, None, w,
                                   tm=TM)
    y_s, rw_s, (k_s, v_s) = _trunk(x_sample, ada[:, bp:bp + bs], kv_ada[:, bp:bp + bs], PAST_LEN + jnp.arange(ts),
                                   [(state_wkv_0, state_shift_0), (state_wkv_1, state_shift_1)],
                                   (cache_k2, cache_v2), w, tm=TM)
    heads = lambda z: z.reshape(z.shape[0], z.shape[1], N_KV, HEAD)
    return (y_p, y_s, rw_p[0][0], rw_s[0][0], rw_p[0][1], rw_s[0][1],
            rw_p[1][0], rw_s[1][0], rw_p[1][1], rw_s[1][1], heads(k_p), heads(k_s), heads(v_p), heads(v_s))
```
